```python
import jax, jax.numpy as jnp
from jax import lax
import numpy as np

D_MODEL = 1024
BATCH = 8
SEQ = 2048
DEPTH = 2

N_HEADS = 8
HEAD_DIM = 64
D_ATT = N_HEADS * HEAD_DIM
IDX_HEADS = 8
IDX_DIM = 32
TOPK_MAX = 256
Q_BLOCK = 128
D_RNN = 1024
RNN_BLOCKS = 16
RNN_BW = D_RNN // RNN_BLOCKS
CONV_W = 4
LRU_C = 8.0
D_FF = 2816
EPS = 1e-6

SPLITS = (D_ATT, HEAD_DIM, HEAD_DIM, IDX_HEADS * IDX_DIM, IDX_DIM, IDX_HEADS,
          D_RNN, D_RNN, D_MODEL, D_MODEL)
N_IN = sum(SPLITS)

kernel_name = "hybrid_dsa_rglru_macaron"


def rmsnorm(x, g):
    x32 = x.astype(jnp.float32)
    y = x32 * lax.rsqrt(jnp.mean(x32 * x32, axis=-1, keepdims=True) + EPS)
    return (y * g.astype(jnp.float32)).astype(x.dtype)


def swiglu(x, w_gate, w_up, w_down):
    return (jax.nn.silu(x @ w_gate) * (x @ w_up)) @ w_down


def split_columns(z):
    offsets = [int(o) for o in np.cumsum(SPLITS)[:-1]]
    return jnp.split(z, offsets, axis=-1)


def causal_dwconv(x, w, b):
    c = x.shape[-1]
    y = lax.conv_general_dilated(
        x, w[:, None, :], window_strides=(1,), padding=[(CONV_W - 1, 0)],
        dimension_numbers=("NWC", "WIO", "NWC"), feature_group_count=c)
    return y + b


def block_diag_linear(x, w, b):
    bsz, s, _ = x.shape
    xb = x.reshape(bsz, s, RNN_BLOCKS, RNN_BW)
    return jnp.einsum("bsni,nij->bsnj", xb, w).reshape(bsz, s, D_RNN) + b


def rg_lru(x, w_a, b_a, w_x, b_x, lam):
    r = jax.nn.sigmoid(block_diag_linear(x, w_a, b_a).astype(jnp.float32))
    i = jax.nn.sigmoid(block_diag_linear(x, w_x, b_x).astype(jnp.float32))
    log_a = -LRU_C * r * jax.nn.softplus(-lam.astype(jnp.float32))
    a = jnp.exp(log_a)
    u = jnp.sqrt(-jnp.expm1(2.0 * log_a)) * i * x.astype(jnp.float32)

    def combine(left, right):
        a1, b1 = left
        a2, b2 = right
        return a1 * a2, a2 * b1 + b2

    _, h = lax.associative_scan(combine, (a, u), axis=1)
    return h.astype(x.dtype)


def dsa_attention(q, k, v, qi, ki, wi):
    bsz, s = k.shape[0], k.shape[1]
    topk = min(TOPK_MAX, s // 4)
    nb = s // Q_BLOCK
    key_pos = jnp.arange(s)

    def to_blocks(t):
        return jnp.moveaxis(t.reshape((bsz, nb, Q_BLOCK) + t.shape[2:]), 1, 0)

    def one_block(args):
        q_b, qi_b, wi_b, start = args
        q_pos = start + jnp.arange(Q_BLOCK)
        causal = key_pos[None, :] <= q_pos[:, None]
        logits = jnp.einsum("bqhd,bsd->bqhs", qi_b.astype(jnp.float32),
                            ki.astype(jnp.float32)) * (IDX_DIM ** -0.5)
        score = jnp.einsum("bqhs,bqh->bqs", jax.nn.relu(logits), wi_b.astype(jnp.float32))
        score = jnp.where(causal[None], score, -jnp.inf)
        _, idx = lax.top_k(score, topk)
        valid = idx <= q_pos[None, :, None]
        k_sel = jax.vmap(lambda kk, ii: kk[ii])(k, idx)
        v_sel = jax.vmap(lambda vv, ii: vv[ii])(v, idx)
        sc = jnp.einsum("bqhd,bqkd->bqhk", q_b, k_sel).astype(jnp.float32) * (HEAD_DIM ** -0.5)
        sc = jnp.where(valid[:, :, None, :], sc, -jnp.inf)
        p = jax.nn.softmax(sc, axis=-1)
        return jnp.einsum("bqhk,bqkd->bqhd", p.astype(v.dtype), v_sel)

    starts = jnp.arange(nb) * Q_BLOCK
    out = lax.map(one_block, (to_blocks(q), to_blocks(qi), to_blocks(wi), starts))
    return jnp.moveaxis(out, 0, 1).reshape(bsz, s, N_HEADS * HEAD_DIM)


def hybrid_mixer(h, w_in, conv_w, conv_b, rg_wa, rg_ba, rg_wx, rg_bx, rg_lam,
                 w_att_proj, w_rnn_proj, w_out):
    bsz, s, _ = h.shape
    z = h @ w_in
    q, k, v, qi, ki, wi, xr, gr, ga_logit, gr_logit = split_columns(z)
    att = dsa_attention(q.reshape(bsz, s, N_HEADS, HEAD_DIM), k, v,
                        qi.reshape(bsz, s, IDX_HEADS, IDX_DIM), ki,
                        wi * (IDX_HEADS ** -0.5))
    xr = causal_dwconv(xr, conv_w, conv_b)
    rnn = rg_lru(xr, rg_wa, rg_ba, rg_wx, rg_bx, rg_lam) * jax.nn.gelu(gr)
    merged = (jax.nn.sigmoid(ga_logit) * (att @ w_att_proj)
              + jax.nn.sigmoid(gr_logit) * (rnn @ w_rnn_proj))
    return merged @ w_out


def setup_inputs(seed: int = 0) -> dict:
    key = jax.random.key(seed)
    ks = jax.random.split(key, 24)
    f32 = jnp.float32

    def w(k, shape, fan_in):
        return jax.random.normal(k, shape, f32) * (fan_in ** -0.5)

    def gain(k, shape):
        return 1.0 + 0.01 * jax.random.normal(k, shape, f32)

    u = jax.random.uniform(ks[13], (DEPTH, D_RNN), f32, minval=0.9, maxval=0.999)
    a0 = u ** (1.0 / LRU_C)
    lam = jnp.log(a0) - jnp.log1p(-a0)

    return {
        "x": jax.random.normal(ks[0], (BATCH, SEQ, D_MODEL), f32),
        "ffn1_norm": gain(ks[1], (DEPTH, D_MODEL)),
        "ffn1_wg": w(ks[2], (DEPTH, D_MODEL, D_FF), D_MODEL),
        "ffn1_wu": w(ks[3], (DEPTH, D_MODEL, D_FF), D_MODEL),
        "ffn1_wd": w(ks[4], (DEPTH, D_FF, D_MODEL), D_FF),
        "mix_norm": gain(ks[5], (DEPTH, D_MODEL)),
        "w_in": w(ks[6], (DEPTH, D_MODEL, N_IN), D_MODEL),
        "conv_w": w(ks[7], (DEPTH, CONV_W, D_RNN), CONV_W),
        "conv_b": 0.01 * jax.random.normal(ks[8], (DEPTH, D_RNN), f32),
        "rg_wa": w(ks[9], (DEPTH, RNN_BLOCKS, RNN_BW, RNN_BW), RNN_BW),
        "rg_ba": 0.01 * jax.random.normal(ks[10], (DEPTH, D_RNN), f32),
        "rg_wx": w(ks[11], (DEPTH, RNN_BLOCKS, RNN_BW, RNN_BW), RNN_BW),
        "rg_bx": 0.01 * jax.random.normal(ks[12], (DEPTH, D_RNN), f32),
        "rg_lam": lam,
        "w_att_proj": w(ks[14], (DEPTH, D_ATT, D_MODEL), D_ATT),
        "w_rnn_proj": w(ks[15], (DEPTH, D_RNN, D_MODEL), D_RNN),
        "w_out": w(ks[16], (DEPTH, D_MODEL, D_MODEL), D_MODEL),
        "ffn2_norm": gain(ks[17], (DEPTH, D_MODEL)),
        "ffn2_wg": w(ks[18], (DEPTH, D_MODEL, D_FF), D_MODEL),
        "ffn2_wu": w(ks[19], (DEPTH, D_MODEL, D_FF), D_MODEL),
        "ffn2_wd": w(ks[20], (DEPTH, D_FF, D_MODEL), D_FF),
        "final_norm": gain(ks[21], (D_MODEL,)),
    }


def reference(x, ffn1_norm, ffn1_wg, ffn1_wu, ffn1_wd, mix_norm, w_in, conv_w, conv_b,
              rg_wa, rg_ba, rg_wx, rg_bx, rg_lam, w_att_proj, w_rnn_proj, w_out,
              ffn2_norm, ffn2_wg, ffn2_wu, ffn2_wd, final_norm):
    for l in range(DEPTH):
        x = x + 0.5 * swiglu(rmsnorm(x, ffn1_norm[l]), ffn1_wg[l], ffn1_wu[l], ffn1_wd[l])
        x = x + hybrid_mixer(rmsnorm(x, mix_norm[l]), w_in[l], conv_w[l], conv_b[l],
                             rg_wa[l], rg_ba[l], rg_wx[l], rg_bx[l], rg_lam[l],
                             w_att_proj[l], w_rnn_proj[l], w_out[l])
        x = x + 0.5 * swiglu(rmsnorm(x, ffn2_norm[l]), ffn2_wg[l], ffn2_wu[l], ffn2_wd[l])
    return rmsnorm(x, final_norm)
```

```python
import functools

import jax
import jax.numpy as jnp
from jax import lax
from jax.experimental import pallas as pl
from jax.experimental.pallas import tpu as pltpu

F32 = jnp.float32
BF16 = jnp.bfloat16
I32 = jnp.int32

N_HEADS = 8
HEAD_DIM = 64
D_ATT = N_HEADS * HEAD_DIM
IDX_HEADS = 8
IDX_DIM = 32
TOPK_MAX = 256
RNN_BLOCKS = 16
CONV_W = 4
LRU_C = 8.0
EPS = 1e-6

LANES = 128
SUBLANES = 8
MXU_DIM = 256
VMEM_LIMIT = 56 * 1024 * 1024

INT_MIN = -2 ** 31
MASK_BIAS = -1e30

ROW_TILE = 512
Q_TILE = 256
K_CHUNK = 256


def _resident(shape):
    zeros = (0,) * len(shape)
    return pl.BlockSpec(shape, lambda *_: zeros, pipeline_mode=pl.Buffered(1))


def _rmsnorm(x, g):
    ms = jnp.mean(x * x, axis=-1, keepdims=True)
    return x * lax.rsqrt(ms + EPS) * g


def _dot(a, b):
    return jnp.dot(a, b, preferred_element_type=F32)


def _dot_nt(a, b):
    return lax.dot_general(a, b, (((1,), (1,)), ((), ())),
                           preferred_element_type=F32)


def _ff_chunks(d_ff):
    chunks, c0 = [], 0
    while c0 < d_ff:
        c1 = min(c0 + 4 * MXU_DIM, d_ff)
        chunks.append((c0, c1))
        c0 = c1
    return chunks


def _ffn_kernel(x_ref, g_ref, wg_ref, wu_ref, wd_ref, *rest, final):
    if final:
        fg_ref, o_ref = rest
    else:
        (o_ref,) = rest
    x = x_ref[...]
    h = _rmsnorm(x, g_ref[...]).astype(BF16)
    acc = None
    for c0, c1 in _ff_chunks(wg_ref.shape[1]):
        g = _dot(h, wg_ref[:, c0:c1])
        u = _dot(h, wu_ref[:, c0:c1])
        a = (g * jax.nn.sigmoid(g) * u).astype(BF16)
        d = _dot(a, wd_ref[c0:c1, :])
        acc = d if acc is None else acc + d
    y = x + 0.5 * acc
    if final:
        y = _rmsnorm(y, fg_ref[...])
    o_ref[...] = y


def _ffn(x, g, wg, wu, wd, final_g=None):
    rows, d = x.shape
    d_ff = wg.shape[1]
    final = final_g is not None
    row_spec = pl.BlockSpec((ROW_TILE, d), lambda i: (i, 0))
    in_specs = [row_spec, _resident((1, d)), _resident((d, d_ff)),
                _resident((d, d_ff)), _resident((d_ff, d))]
    args = [x, g, wg, wu, wd]
    if final:
        in_specs.append(_resident((1, d)))
        args.append(final_g)
    return pl.pallas_call(
        functools.partial(_ffn_kernel, final=final),
        out_shape=jax.ShapeDtypeStruct((rows, d), F32),
        grid=(rows // ROW_TILE,),
        in_specs=in_specs,
        out_specs=row_spec,
        compiler_params=pltpu.CompilerParams(
            dimension_semantics=("parallel",), vmem_limit_bytes=VMEM_LIMIT),
        name="ffn_final" if final else "ffn",
    )(*args)


def _inproj_kernel(x_ref, g_ref, w_ref, q_ref, kv_ref, qi_ref, ki_ref, wi_ref,
                   xr_ref, gr_ref, ga_ref, gl_ref, *, d_rnn, d_model):
    h = _rmsnorm(x_ref[...], g_ref[...]).astype(BF16)
    c = 0

    def seg(width):
        nonlocal c
        z = _dot(h, w_ref[:, c:c + width])
        c += width
        return z

    q_ref[...] = (seg(D_ATT) * (HEAD_DIM ** -0.5)).astype(BF16)
    kv_ref[...] = seg(4 * HEAD_DIM).astype(BF16)
    qi_ref[...] = seg(IDX_HEADS * IDX_DIM).astype(BF16)
    ki_ref[...] = seg(LANES).astype(BF16)
    wi_ref[...] = seg(LANES) * ((IDX_HEADS * IDX_DIM) ** -0.5)
    xr_ref[...] = seg(d_rnn)
    gr_ref[...] = seg(d_rnn)
    ga_ref[...] = seg(d_model)
    gl_ref[...] = seg(d_model)


def _inproj(x, g, w, d_rnn):
    rows, d = x.shape
    widths = (D_ATT, 4 * HEAD_DIM, IDX_HEADS * IDX_DIM, LANES, LANES,
              d_rnn, d_rnn, d, d)
    dtypes = (BF16, BF16, BF16, BF16, F32, F32, F32, F32, F32)
    assert sum(widths) == w.shape[1]
    return pl.pallas_call(
        functools.partial(_inproj_kernel, d_rnn=d_rnn, d_model=d),
        out_shape=[jax.ShapeDtypeStruct((rows, n), t)
                   for n, t in zip(widths, dtypes)],
        grid=(rows // ROW_TILE,),
        in_specs=[pl.BlockSpec((ROW_TILE, d), lambda i: (i, 0)),
                  _resident((1, d)), _resident(w.shape)],
        out_specs=[pl.BlockSpec((ROW_TILE, n), lambda i: (i, 0))
                   for n in widths],
        compiler_params=pltpu.CompilerParams(
            dimension_semantics=("parallel",), vmem_limit_bytes=VMEM_LIMIT),
        name="inproj",
    )(x, g, w)


VT_ROWS = HEAD_DIM + 16


def _key_to_float(key):
    bits = jnp.where(key < 0, INT_MIN - key, key)
    return pltpu.bitcast(bits, F32)


def _attn_kernel(q_ref, qi_ref, wi_ref, kv_ref, ki_ref, o_ref,
                 qm_scr, qim_scr, wit_scr, vt_scr, sc_scr, bias_scr, jmax_scr,
                 m_scr, alpha_scr, acc_scr, s_scr, p_scr, *, topk):
    tq, ck = Q_TILE, K_CHUNK
    n_kc = sc_scr.shape[0]
    seq = n_kc * ck
    qt = pl.program_id(1)
    n_chunks = (qt * tq) // ck + tq // ck
    lane = lax.broadcasted_iota(I32, (tq, LANES), 1)
    key_iota = lax.broadcasted_iota(I32, (ck, tq), 0)
    qry_iota = lax.broadcasted_iota(I32, (ck, tq), 1)
    kf = float(topk)

    for h in range(N_HEADS):
        qp = q_ref[:, (h // 2) * LANES:(h // 2 + 1) * LANES].astype(F32)
        lo = (h % 2) * HEAD_DIM
        qm_scr[h] = jnp.where((lane >= lo) & (lane < lo + HEAD_DIM), qp,
                              0.0).astype(BF16)
    for h in range(IDX_HEADS):
        qp = qi_ref[:, (h // 4) * LANES:(h // 4 + 1) * LANES].astype(F32)
        lo = (h % 4) * IDX_DIM
        qim_scr[h] = jnp.where((lane >= lo) & (lane < lo + IDX_DIM), qp,
                               0.0).astype(BF16)
    wit_scr[...] = jnp.transpose(wi_ref[...])

    @pl.when(qt == 0)
    def _():
        lane_c = lax.broadcasted_iota(I32, (ck, LANES), 1)
        for c in range(n_kc):
            v1 = jnp.where(lane_c < HEAD_DIM,
                           kv_ref[c * ck:(c + 1) * ck, LANES:2 * LANES].astype(F32),
                           1.0)
            vt_scr[c] = jnp.transpose(v1)[0:VT_ROWS, :].astype(BF16)

    def score_chunk(kc, carry):
        k0 = pl.multiple_of(kc * ck, ck)
        kic = ki_ref[pl.ds(k0, ck), :]
        acc = jnp.zeros((ck, tq), F32)
        for h in range(IDX_HEADS):
            acc = acc + (jnp.maximum(_dot_nt(kic, qim_scr[h]), 0.0)
                         * wit_scr[h:h + 1, :])
        causal = key_iota - qry_iota <= qt * tq - kc * ck
        sc_scr[kc] = jnp.where(causal, acc, -jnp.inf)
        return carry

    lax.fori_loop(0, n_chunks, score_chunk, 0)

    def count(pred):
        def chunk(kc, c):
            hit = jnp.where(pred(sc_scr[kc], kc), 1.0, 0.0)
            return c + jnp.sum(hit.reshape(ck // SUBLANES, SUBLANES, tq), axis=0)

        c = lax.fori_loop(0, n_chunks, chunk, jnp.zeros((SUBLANES, tq), F32))
        return jnp.sum(c, axis=0, keepdims=True)

    def bit_step(i, thr):
        cand = thr + lax.shift_left(jnp.int32(1), 31 - i)
        cand_f = _key_to_float(cand)
        tot = count(lambda s, kc: s >= cand_f)
        return jnp.where(tot >= kf, cand, thr)

    thr = lax.fori_loop(0, 32, bit_step, jnp.full((1, tq), INT_MIN, I32))

    thr_f = _key_to_float(thr)
    short = jnp.logical_not(thr_f > -jnp.inf)
    thr_f = jnp.where(short, -jnp.inf, thr_f)
    n_gt = count(lambda s, kc: s > thr_f)
    n_eq = count(lambda s, kc: s == thr_f)
    need = kf - n_gt
    straddle = jnp.where(short | (n_eq == need), 0.0, 1.0)
    jmax_scr[...] = jnp.where(short, -1, 2 ** 30)

    @pl.when(jnp.max(straddle) > 0.0)
    def _():
        n_bits = max(1, (seq - 1).bit_length())

        def jbit_step(i, jm):
            cand = jm + lax.shift_left(jnp.int32(1), n_bits - 1 - i)
            tot = count(lambda s, kc: (s == thr_f)
                        & (key_iota + kc * ck <= cand))
            return jnp.where(tot < need, cand, jm)

        jm = lax.fori_loop(0, n_bits, jbit_step, jnp.full((1, tq), -1, I32))
        jmax_scr[...] = jnp.where(short, -1, jm + 1)

    jmax = jmax_scr[...]

    def bias_chunk(kc, carry):
        s = sc_scr[kc]
        tie_ok = (s == thr_f) & (key_iota + kc * ck <= jmax)
        bias_scr[kc] = jnp.where((s > thr_f) | tie_ok, 0.0, MASK_BIAS)
        return carry

    lax.fori_loop(0, n_chunks, bias_chunk, 0)

    m_scr[...] = jnp.full(m_scr.shape, MASK_BIAS, F32)
    acc_scr[...] = jnp.zeros(acc_scr.shape, F32)

    def att_chunk(kc, carry):
        k0 = pl.multiple_of(kc * ck, ck)
        kv_c = kv_ref[pl.ds(k0, ck), 0:LANES]
        vk_c = kv_ref[pl.ds(k0, ck), LANES:2 * LANES]
        vt_c = vt_scr[kc]
        bias = bias_scr[kc]
        for h in range(N_HEADS):
            k_mat = kv_c if h % 2 == 0 else vk_c
            s_scr[h] = _dot_nt(k_mat, qm_scr[h]) + bias
        for h in range(N_HEADS):
            s = s_scr[h]
            m_old = m_scr[h:h + 1, :]
            m_cur = jnp.max(s.reshape(ck // SUBLANES, SUBLANES, tq), axis=0)
            m_new = jnp.maximum(m_old, jnp.max(m_cur, axis=0, keepdims=True))
            p_scr[h] = jnp.exp(s_scr[h] - m_new).astype(BF16)
            alpha_scr[h:h + 1, :] = jnp.exp(m_old - m_new)
            m_scr[h:h + 1, :] = m_new
        for h in range(N_HEADS):
            acc_scr[h] = (alpha_scr[h:h + 1, :] * acc_scr[h]
                          + _dot(vt_c, p_scr[h]))
        return carry

    lax.fori_loop(0, n_chunks, att_chunk, 0)

    outs = []
    for h in range(N_HEADS):
        a = acc_scr[h]
        outs.append(a[0:HEAD_DIM, :] / a[HEAD_DIM:HEAD_DIM + 1, :])
    o_ref[...] = jnp.transpose(jnp.concatenate(outs, axis=0)).astype(BF16)


def _attention(q, qi, wi, kv, ki, batch, seq):
    tq, ck = Q_TILE, K_CHUNK
    topk = min(TOPK_MAX, seq // 4)
    nq = seq // tq

    def view(a):
        return a.reshape(seq, batch * a.shape[1])

    def q_spec(c):
        return pl.BlockSpec((tq, c), lambda b, i: (i, b))

    def kv_spec(c):
        return pl.BlockSpec((seq, c), lambda b, i: (0, b))

    out = pl.pallas_call(
        functools.partial(_attn_kernel, topk=topk),
        out_shape=jax.ShapeDtypeStruct((seq, batch * D_ATT), BF16),
        grid=(batch, nq),
        in_specs=[q_spec(D_ATT), q_spec(IDX_HEADS * IDX_DIM), q_spec(LANES),
                  kv_spec(4 * HEAD_DIM), kv_spec(LANES)],
        out_specs=q_spec(D_ATT),
        scratch_shapes=[
            pltpu.VMEM((N_HEADS, tq, LANES), BF16),
            pltpu.VMEM((IDX_HEADS, tq, LANES), BF16),
            pltpu.VMEM((LANES, tq), F32),
            pltpu.VMEM((seq // ck, VT_ROWS, ck), BF16),
            pltpu.VMEM((seq // ck, ck, tq), F32),
            pltpu.VMEM((seq // ck, ck, tq), F32),
            pltpu.VMEM((1, tq), I32),
            pltpu.VMEM((N_HEADS, tq), F32),
            pltpu.VMEM((N_HEADS, tq), F32),
            pltpu.VMEM((N_HEADS, VT_ROWS, tq), F32),
            pltpu.VMEM((N_HEADS, ck, tq), F32),
            pltpu.VMEM((N_HEADS, ck, tq), BF16),
        ],
        compiler_params=pltpu.CompilerParams(
            dimension_semantics=("parallel", "arbitrary"),
            vmem_limit_bytes=VMEM_LIMIT),
        name="dsa_attention",
    )(view(q), view(qi), view(wi), view(kv), view(ki))
    return out.reshape(seq * batch, D_ATT)


def _rnn_kernel(xr_ref, gr_ref, cw_ref, cb_ref, wa_ref, ba_ref, wx_ref,
                bx_ref, lam_ref, o_ref, xs_scr, a_scr, u_scr, h_scr, hc_scr,
                *, batch):
    tm, c = xr_ref.shape
    halo = (CONV_W - 1) * batch

    @pl.when(pl.program_id(0) == 0)
    def _():
        xs_scr[0:halo, :] = jnp.zeros((halo, c), F32)
        hc_scr[...] = jnp.zeros(hc_scr.shape, F32)

    xs_scr[halo:halo + tm, :] = xr_ref[...]
    xc = cb_ref[...]
    for j in range(CONV_W):
        xc = xc + cw_ref[j:j + 1, :] * xs_scr[j * batch:j * batch + tm, :]
    xs_scr[0:halo, :] = xr_ref[tm - halo:tm, :]

    xb = xc.astype(BF16)
    n_diag = c // MXU_DIM
    ra = jnp.concatenate(
        [_dot(xb[:, n * MXU_DIM:(n + 1) * MXU_DIM], wa_ref[n])
         for n in range(n_diag)], axis=1)
    rx = jnp.concatenate(
        [_dot(xb[:, n * MXU_DIM:(n + 1) * MXU_DIM], wx_ref[n])
         for n in range(n_diag)], axis=1)
    r = jax.nn.sigmoid(ra + ba_ref[...])
    gate_i = jax.nn.sigmoid(rx + bx_ref[...])
    neg_lam = -lam_ref[...]
    softplus = jnp.maximum(neg_lam, 0.0) + jnp.log1p(jnp.exp(-jnp.abs(neg_lam)))
    log_a = -LRU_C * r * softplus
    a_scr[...] = jnp.exp(log_a)
    th = jnp.tanh(log_a)
    u_scr[...] = jnp.sqrt(-2.0 * th / (1.0 - th)) * gate_i * xc

    def step(t, h):
        r0 = pl.multiple_of(t * batch, batch)
        h = a_scr[pl.ds(r0, batch), :] * h + u_scr[pl.ds(r0, batch), :]
        h_scr[pl.ds(r0, batch), :] = h
        return h

    hc_scr[...] = lax.fori_loop(0, tm // batch, step, hc_scr[...])
    o_ref[...] = (h_scr[...] * jax.nn.gelu(gr_ref[...])).astype(BF16)


def _rnn(xr, gr, cw, cb, wa, ba, wx, bx, lam, batch):
    rows, c = xr.shape
    tm = ROW_TILE
    halo = (CONV_W - 1) * batch
    row_spec = pl.BlockSpec((tm, c), lambda i: (i, 0))
    return pl.pallas_call(
        functools.partial(_rnn_kernel, batch=batch),
        out_shape=jax.ShapeDtypeStruct((rows, c), BF16),
        grid=(rows // tm,),
        in_specs=[row_spec, row_spec, _resident(cw.shape), _resident((1, c)),
                  _resident(wa.shape), _resident((1, c)), _resident(wx.shape),
                  _resident((1, c)), _resident((1, c))],
        out_specs=row_spec,
        scratch_shapes=[pltpu.VMEM((halo + tm, c), F32),
                        pltpu.VMEM((tm, c), F32), pltpu.VMEM((tm, c), F32),
                        pltpu.VMEM((tm, c), F32), pltpu.VMEM((batch, c), F32)],
        compiler_params=pltpu.CompilerParams(
            dimension_semantics=("arbitrary",), vmem_limit_bytes=VMEM_LIMIT),
        name="rglru",
    )(xr, gr, cw, cb, wa, ba, wx, bx, lam)


def _merge_kernel(att_ref, rnn_ref, ga_ref, gl_ref, x_ref, wa_ref, wr_ref,
                  wo_ref, o_ref):
    pa = _dot(att_ref[...], wa_ref[...])
    pr = _dot(rnn_ref[...], wr_ref[...])
    merged = (jax.nn.sigmoid(ga_ref[...]) * pa
              + jax.nn.sigmoid(gl_ref[...]) * pr)
    o_ref[...] = x_ref[...] + _dot(merged.astype(BF16), wo_ref[...])


def _merge(att, rnn, ga, gl, x, wa, wr, wo):
    rows, d = x.shape

    def row_spec(c):
        return pl.BlockSpec((ROW_TILE, c), lambda i: (i, 0))

    return pl.pallas_call(
        _merge_kernel,
        out_shape=jax.ShapeDtypeStruct((rows, d), F32),
        grid=(rows // ROW_TILE,),
        in_specs=[row_spec(att.shape[1]), row_spec(rnn.shape[1]), row_spec(d),
                  row_spec(d), row_spec(d), _resident(wa.shape),
                  _resident(wr.shape), _resident(wo.shape)],
        out_specs=row_spec(d),
        compiler_params=pltpu.CompilerParams(
            dimension_semantics=("parallel",), vmem_limit_bytes=VMEM_LIMIT),
        name="merge",
    )(att, rnn, ga, gl, x, wa, wr, wo)


def _inproj_weight(w_in, d_rnn, d_model):
    splits = (D_ATT, HEAD_DIM, HEAD_DIM, IDX_HEADS * IDX_DIM, IDX_DIM,
              IDX_HEADS, d_rnn, d_rnn, d_model, d_model)
    parts, c = [], 0
    for n in splits:
        parts.append(w_in[:, c:c + n])
        c += n
    assert c == w_in.shape[1]
    wq, wk, wv, wqi, wki, wwi, wxr, wgr, wga, wgl = parts
    wi_pad = jnp.zeros((w_in.shape[0], LANES - IDX_HEADS), w_in.dtype)
    cols = [wq, wk, wv, wv, wk, wqi] + [wki] * (LANES // IDX_DIM)
    cols += [wwi, wi_pad, wxr, wgr, wga, wgl]
    return jnp.concatenate(cols, axis=1).astype(BF16)


def _block_diag_tiles(w):
    n_blocks, bw, _ = w.shape
    per = MXU_DIM // bw
    w = w.reshape(n_blocks // per, per, bw, bw)
    eye = jnp.eye(per, dtype=w.dtype)
    t = w[:, :, :, None, :] * eye[None, :, None, :, None]
    return t.reshape(n_blocks // per, MXU_DIM, MXU_DIM).astype(BF16)


def kernel(x, ffn1_norm, ffn1_wg, ffn1_wu, ffn1_wd, mix_norm, w_in, conv_w,
           conv_b, rg_wa, rg_ba, rg_wx, rg_bx, rg_lam, w_att_proj, w_rnn_proj,
           w_out, ffn2_norm, ffn2_wg, ffn2_wu, ffn2_wd, final_norm):
    batch, seq, d = x.shape
    depth = ffn1_norm.shape[0]
    d_rnn = conv_w.shape[-1]
    assert batch == SUBLANES and seq % Q_TILE == 0
    assert (batch * seq) % ROW_TILE == 0 and ROW_TILE % batch == 0

    def row(v):
        return v.reshape(1, -1)

    h = jnp.swapaxes(x, 0, 1).reshape(seq * batch, d)
    for l in range(depth):
        h = _ffn(h, row(ffn1_norm[l]), ffn1_wg[l].astype(BF16),
                 ffn1_wu[l].astype(BF16), ffn1_wd[l].astype(BF16))
        q, kv, qi, ki, wi, xr, gr, ga, gl = _inproj(
            h, row(mix_norm[l]), _inproj_weight(w_in[l], d_rnn, d), d_rnn)
        att = _attention(q, qi, wi, kv, ki, batch, seq)
        rnn = _rnn(xr, gr, conv_w[l], row(conv_b[l]),
                   _block_diag_tiles(rg_wa[l]), row(rg_ba[l]),
                   _block_diag_tiles(rg_wx[l]), row(rg_bx[l]), row(rg_lam[l]),
                   batch)
        h = _merge(att, rnn, ga, gl, h, w_att_proj[l].astype(BF16),
                   w_rnn_proj[l].astype(BF16), w_out[l].astype(BF16))
        last = l == depth - 1
        h = _ffn(h, row(ffn2_norm[l]), ffn2_wg[l].astype(BF16),
                 ffn2_wu[l].astype(BF16), ffn2_wd[l].astype(BF16),
                 final_g=row(final_norm) if last else None)
    return jnp.swapaxes(h.reshape(seq, batch, d), 0, 1)
```

```python
import functools

import jax
import jax.numpy as jnp
from jax import lax
from jax.experimental import pallas as pl
from jax.experimental.pallas import tpu as pltpu

F32 = jnp.float32
BF16 = jnp.bfloat16
I32 = jnp.int32

N_HEADS = 8
HEAD_DIM = 64
D_ATT = N_HEADS * HEAD_DIM
IDX_HEADS = 8
IDX_DIM = 32
TOPK_MAX = 256
RNN_BLOCKS = 16
CONV_W = 4
LRU_C = 8.0
EPS = 1e-6

LANES = 128
SUBLANES = 8
MXU_DIM = 256
VMEM_LIMIT = 56 * 1024 * 1024

INT_MIN = -2 ** 31
MASK_BIAS = -1e30

ROW_TILE = 512
Q_TILE = 256
K_CHUNK = 256


def _resident(shape):
    zeros = (0,) * len(shape)
    return pl.BlockSpec(shape, lambda *_: zeros, pipeline_mode=pl.Buffered(1))


def _rmsnorm(x, g):
    ms = jnp.mean(x * x, axis=-1, keepdims=True)
    return x * lax.rsqrt(ms + EPS) * g


def _dot(a, b):
    return jnp.dot(a, b, preferred_element_type=F32)


def _dot_nt(a, b):
    return lax.dot_general(a, b, (((1,), (1,)), ((), ())),
                           preferred_element_type=F32)


def _ff_chunks(d_ff):
    chunks, c0 = [], 0
    while c0 < d_ff:
        c1 = min(c0 + 4 * MXU_DIM, d_ff)
        chunks.append((c0, c1))
        c0 = c1
    return chunks


def _slabs(width):
    return width // LANES


def _batch_to_rows(x_ref, slab_scr, batch):
    tt, d = x_ref.shape[1:]
    for b in range(batch):
        for j in range(_slabs(d)):
            slab_scr[j, pl.ds(b, tt, stride=batch), :] = (
                x_ref[b, :, j * LANES:(j + 1) * LANES])
    return jnp.concatenate([slab_scr[j] for j in range(_slabs(d))], axis=1)


def _rows_to_batch(o_ref, y, slab_scr, batch):
    tt, d = o_ref.shape[1:]
    for j in range(_slabs(d)):
        slab_scr[j] = y[:, j * LANES:(j + 1) * LANES]
    for b in range(batch):
        for j in range(_slabs(d)):
            o_ref[b, :, j * LANES:(j + 1) * LANES] = (
                slab_scr[j, pl.ds(b, tt, stride=batch), :])


def _rows_to_seq(o_ref, z, slab_scr, slab0, batch):
    tt = o_ref.shape[0]
    c = z.shape[1]
    for j in range(_slabs(c)):
        slab_scr[slab0 + j] = z[:, j * LANES:(j + 1) * LANES]
    for b in range(batch):
        for j in range(_slabs(c)):
            o_ref[:, b * c + j * LANES:b * c + (j + 1) * LANES] = (
                slab_scr[slab0 + j, pl.ds(b, tt, stride=batch), :]
                .astype(o_ref.dtype))


def _seq_to_rows(x_ref, slab_scr, batch):
    tt = x_ref.shape[0]
    c = x_ref.shape[1] // batch
    for b in range(batch):
        for j in range(_slabs(c)):
            slab_scr[j, pl.ds(b, tt, stride=batch), :] = (
                x_ref[:, b * c + j * LANES:b * c + (j + 1) * LANES].astype(F32))
    return jnp.concatenate([slab_scr[j] for j in range(_slabs(c))], axis=1)


def _ffn_kernel(x_ref, g_ref, wg_ref, wu_ref, wd_ref, *rest, first, final,
                batch):
    rest = list(rest)
    fg_ref = rest.pop(0) if final else None
    o_ref = rest.pop(0)
    slab_scr = rest.pop(0) if (first or final) else None
    x = _batch_to_rows(x_ref, slab_scr, batch) if first else x_ref[...]
    h = _rmsnorm(x, g_ref[...]).astype(BF16)
    acc = None
    for c0, c1 in _ff_chunks(wg_ref.shape[1]):
        g = _dot(h, wg_ref[:, c0:c1])
        u = _dot(h, wu_ref[:, c0:c1])
        a = (g * jax.nn.sigmoid(g) * u).astype(BF16)
        d = _dot(a, wd_ref[c0:c1, :])
        acc = d if acc is None else acc + d
    y = x + 0.5 * acc
    if final:
        _rows_to_batch(o_ref, _rmsnorm(y, fg_ref[...]), slab_scr, batch)
    else:
        o_ref[...] = y


def _ffn(x, g, wg, wu, wd, batch, first=False, final_g=None):
    final = final_g is not None
    if first:
        _, seq, d = x.shape
        rows = seq * batch
    else:
        rows, d = x.shape
        seq = rows // batch
    d_ff = wg.shape[1]
    tt = ROW_TILE // batch
    row_spec = pl.BlockSpec((ROW_TILE, d), lambda i: (i, 0))
    batch_spec = pl.BlockSpec((batch, tt, d), lambda i: (0, i, 0))
    in_specs = [batch_spec if first else row_spec, _resident((1, d)),
                _resident((d, d_ff)), _resident((d, d_ff)),
                _resident((d_ff, d))]
    args = [x, g, wg, wu, wd]
    if final:
        in_specs.append(_resident((1, d)))
        args.append(final_g)
    scratch = ([pltpu.VMEM((_slabs(d), ROW_TILE, LANES), F32)]
               if (first or final) else [])
    out_shape = (batch, seq, d) if final else (rows, d)
    return pl.pallas_call(
        functools.partial(_ffn_kernel, first=first, final=final, batch=batch),
        out_shape=jax.ShapeDtypeStruct(out_shape, F32),
        grid=(rows // ROW_TILE,),
        in_specs=in_specs,
        out_specs=batch_spec if final else row_spec,
        scratch_shapes=scratch,
        compiler_params=pltpu.CompilerParams(
            dimension_semantics=("parallel",), vmem_limit_bytes=VMEM_LIMIT),
        name="ffn_first" if first else ("ffn_final" if final else "ffn"),
    )(*args)


ATT_WIDTHS = (D_ATT, 4 * HEAD_DIM, IDX_HEADS * IDX_DIM, LANES, LANES)
ATT_DTYPES = (BF16, BF16, BF16, BF16, F32)


def _mixer_in_kernel(x_ref, g_ref, w_ref, cw_ref, cb_ref, wa_ref, ba_ref,
                     wx_ref, bx_ref, lam_ref,
                     q_ref, kv_ref, qi_ref, ki_ref, wi_ref, ga_ref, gl_ref,
                     rnn_ref, slab_scr, xs_scr, a_scr, u_scr, h_scr, hc_scr,
                     *, batch):
    tm = x_ref.shape[0]
    c = cw_ref.shape[1]
    d_model = ga_ref.shape[1]
    halo = (CONV_W - 1) * batch
    h = _rmsnorm(x_ref[...], g_ref[...]).astype(BF16)
    col = 0

    def seg(width):
        nonlocal col
        z = _dot(h, w_ref[:, col:col + width])
        col += width
        return z

    @pl.when(pl.program_id(0) == 0)
    def _():
        xs_scr[0:halo, :] = jnp.zeros((halo, c), F32)
        hc_scr[...] = jnp.zeros(hc_scr.shape, F32)

    for n in range(c // MXU_DIM):
        cs = slice(n * MXU_DIM, (n + 1) * MXU_DIM)
        xr = _dot(h, w_ref[:, cs])
        xs_scr[halo:halo + tm, cs] = xr
        xc = cb_ref[:, cs]
        for j in range(CONV_W):
            xc = xc + cw_ref[j:j + 1, cs] * xs_scr[j * batch:j * batch + tm, cs]
        xs_scr[0:halo, cs] = xr[tm - halo:tm, :]

        xb = xc.astype(BF16)
        r = jax.nn.sigmoid(_dot(xb, wa_ref[n]) + ba_ref[:, cs])
        gate_i = jax.nn.sigmoid(_dot(xb, wx_ref[n]) + bx_ref[:, cs])
        neg_lam = -lam_ref[:, cs]
        softplus = (jnp.maximum(neg_lam, 0.0)
                    + jnp.log1p(jnp.exp(-jnp.abs(neg_lam))))
        log_a = -LRU_C * r * softplus
        a_scr[:, cs] = jnp.exp(log_a)
        th = jnp.tanh(log_a)
        u_scr[:, cs] = jnp.sqrt(-2.0 * th / (1.0 - th)) * gate_i * xc

        hcur = hc_scr[:, cs]
        for t in range(tm // batch):
            rs = slice(t * batch, (t + 1) * batch)
            hcur = a_scr[rs, cs] * hcur + u_scr[rs, cs]
            h_scr[rs, cs] = hcur
        hc_scr[:, cs] = hcur
        gr = _dot(h, w_ref[:, c + n * MXU_DIM:c + (n + 1) * MXU_DIM])
        rnn_ref[:, cs] = (h_scr[:, cs] * jax.nn.gelu(gr)).astype(BF16)
    col = 2 * c

    scales = (HEAD_DIM ** -0.5, None, None, None, (IDX_HEADS * IDX_DIM) ** -0.5)
    slab0 = 0
    for o_ref, width, scale in zip((q_ref, kv_ref, qi_ref, ki_ref, wi_ref),
                                   ATT_WIDTHS, scales):
        z = seg(width)
        if scale is not None:
            z = z * scale
        _rows_to_seq(o_ref, z, slab_scr, slab0, batch)
        slab0 += _slabs(width)

    ga_ref[...] = seg(d_model)
    gl_ref[...] = seg(d_model)


def _mixer_in(x, g, w, cw, cb, wa, ba, wx, bx, lam, batch):
    rows, d = x.shape
    c = cw.shape[1]
    tm = ROW_TILE
    tt = tm // batch
    seq = rows // batch
    halo = (CONV_W - 1) * batch
    assert w.shape[1] == 2 * c + sum(ATT_WIDTHS) + 2 * d

    def row_spec(n):
        return pl.BlockSpec((tm, n), lambda i: (i, 0))

    def seq_spec(n):
        return pl.BlockSpec((tt, batch * n), lambda i: (i, 0))

    out_shape = [jax.ShapeDtypeStruct((seq, batch * n), t)
                 for n, t in zip(ATT_WIDTHS, ATT_DTYPES)]
    out_shape += [jax.ShapeDtypeStruct((rows, d), F32)] * 2
    out_shape += [jax.ShapeDtypeStruct((rows, c), BF16)]
    out_specs = [seq_spec(n) for n in ATT_WIDTHS]
    out_specs += [row_spec(d), row_spec(d), row_spec(c)]
    return pl.pallas_call(
        functools.partial(_mixer_in_kernel, batch=batch),
        out_shape=out_shape,
        grid=(rows // tm,),
        in_specs=[row_spec(d), _resident((1, d)), _resident(w.shape),
                  _resident(cw.shape), _resident((1, c)), _resident(wa.shape),
                  _resident((1, c)), _resident(wx.shape), _resident((1, c)),
                  _resident((1, c))],
        out_specs=out_specs,
        scratch_shapes=[
            pltpu.VMEM((_slabs(sum(ATT_WIDTHS)), tm, LANES), F32),
            pltpu.VMEM((halo + tm, c), F32),
            pltpu.VMEM((tm, c), F32), pltpu.VMEM((tm, c), F32),
            pltpu.VMEM((tm, c), F32), pltpu.VMEM((batch, c), F32)],
        compiler_params=pltpu.CompilerParams(
            dimension_semantics=("arbitrary",), vmem_limit_bytes=VMEM_LIMIT),
        name="mixer_in",
    )(x, g, w, cw, cb, wa, ba, wx, bx, lam)


VT_ROWS = HEAD_DIM + 16


def _key_to_float(key):
    bits = jnp.where(key < 0, INT_MIN - key, key)
    return pltpu.bitcast(bits, F32)


def _attn_kernel(q_ref, qi_ref, wi_ref, kv_ref, ki_ref, o_ref,
                 qm_scr, qim_scr, wit_scr, vt_scr, sc_scr, bias_scr, jmax_scr,
                 m_scr, acc_scr, mcur_scr, s_scr, p_scr, alpha_scr, *, topk):
    tq, ck = Q_TILE, K_CHUNK
    n_kc = sc_scr.shape[0]
    seq = n_kc * ck
    qt = pl.program_id(1)
    n_chunks = (qt * tq) // ck + tq // ck
    lane = lax.broadcasted_iota(I32, (tq, LANES), 1)
    key_iota = lax.broadcasted_iota(I32, (ck, tq), 0)
    qry_iota = lax.broadcasted_iota(I32, (ck, tq), 1)
    kf = float(topk)

    for h in range(N_HEADS):
        qp = q_ref[:, (h // 2) * LANES:(h // 2 + 1) * LANES].astype(F32)
        lo = (h % 2) * HEAD_DIM
        qm_scr[h] = jnp.where((lane >= lo) & (lane < lo + HEAD_DIM), qp,
                              0.0).astype(BF16)
    for h in range(IDX_HEADS):
        qp = qi_ref[:, (h // 4) * LANES:(h // 4 + 1) * LANES].astype(F32)
        lo = (h % 4) * IDX_DIM
        qim_scr[h] = jnp.where((lane >= lo) & (lane < lo + IDX_DIM), qp,
                               0.0).astype(BF16)
    wit_scr[...] = jnp.transpose(wi_ref[...])

    @pl.when(qt == 0)
    def _():
        lane_c = lax.broadcasted_iota(I32, (ck, LANES), 1)
        for c in range(n_kc):
            v1 = jnp.where(lane_c < HEAD_DIM,
                           kv_ref[c * ck:(c + 1) * ck, LANES:2 * LANES].astype(F32),
                           1.0)
            vt_scr[c] = jnp.transpose(v1)[0:VT_ROWS, :].astype(BF16)

    def score_chunk(kc, carry):
        k0 = pl.multiple_of(kc * ck, ck)
        kic = ki_ref[pl.ds(k0, ck), :]
        acc = jnp.zeros((ck, tq), F32)
        for h in range(IDX_HEADS):
            acc = acc + (jnp.maximum(_dot_nt(kic, qim_scr[h]), 0.0)
                         * wit_scr[h:h + 1, :])
        causal = key_iota - qry_iota <= qt * tq - kc * ck
        sc_scr[kc] = jnp.where(causal, acc, -jnp.inf)
        return carry

    lax.fori_loop(0, n_chunks, score_chunk, 0)

    def count(pred):
        acc_rows = 2 * SUBLANES

        def chunk(kc, c):
            hit = jnp.where(pred(sc_scr[kc], kc), 1.0, 0.0)
            return c + jnp.sum(hit.reshape(ck // acc_rows, acc_rows, tq), axis=0)

        c = lax.fori_loop(0, n_chunks, chunk, jnp.zeros((acc_rows, tq), F32))
        return jnp.sum(c, axis=0, keepdims=True)

    def bit_step(i, thr):
        cand = thr + lax.shift_left(jnp.int32(1), 31 - i)
        cand_f = _key_to_float(cand)
        tot = count(lambda s, kc: s >= cand_f)
        return jnp.where(tot >= kf, cand, thr)

    thr = lax.fori_loop(0, 32, bit_step, jnp.full((1, tq), INT_MIN, I32))

    thr_f = _key_to_float(thr)
    short = jnp.logical_not(thr_f > -jnp.inf)
    thr_f = jnp.where(short, -jnp.inf, thr_f)
    n_gt = count(lambda s, kc: s > thr_f)
    n_eq = count(lambda s, kc: s == thr_f)
    need = kf - n_gt
    straddle = jnp.where(short | (n_eq == need), 0.0, 1.0)
    jmax_scr[...] = jnp.where(short, -1, 2 ** 30)

    @pl.when(jnp.max(straddle) > 0.0)
    def _():
        n_bits = max(1, (seq - 1).bit_length())

        def jbit_step(i, jm):
            cand = jm + lax.shift_left(jnp.int32(1), n_bits - 1 - i)
            tot = count(lambda s, kc: (s == thr_f)
                        & (key_iota + kc * ck <= cand))
            return jnp.where(tot < need, cand, jm)

        jm = lax.fori_loop(0, n_bits, jbit_step, jnp.full((1, tq), -1, I32))
        jmax_scr[...] = jnp.where(short, -1, jm + 1)

    jmax = jmax_scr[...]

    def bias_chunk(kc, carry):
        s = sc_scr[kc]
        tie_ok = (s == thr_f) & (key_iota + kc * ck <= jmax)
        bias_scr[kc] = jnp.where((s > thr_f) | tie_ok, 0.0, MASK_BIAS)
        return carry

    lax.fori_loop(0, n_chunks, bias_chunk, 0)

    m_scr[...] = jnp.full(m_scr.shape, MASK_BIAS, F32)
    acc_scr[...] = jnp.zeros(acc_scr.shape, F32)

    def att_chunk(kc, carry):
        k0 = pl.multiple_of(kc * ck, ck)
        vt_c = vt_scr[kc]
        bias = bias_scr[kc]
        for h in range(N_HEADS):
            c0 = 0 if h % 2 == 0 else LANES
            k_mat = kv_ref[pl.ds(k0, ck), c0:c0 + LANES]
            s = _dot_nt(k_mat, qm_scr[h]) + bias
            s_scr[h] = s
            mcur_scr[h] = jnp.max(s.reshape(ck // SUBLANES, SUBLANES, tq), axis=0)
        for h in range(N_HEADS):
            m_old = m_scr[h:h + 1, :]
            m_new = jnp.maximum(m_old,
                                jnp.max(mcur_scr[h], axis=0, keepdims=True))
            p_scr[h] = jnp.exp(s_scr[h] - m_new).astype(BF16)
            alpha_scr[h:h + 1, :] = jnp.exp(m_old - m_new)
            m_scr[h:h + 1, :] = m_new
        for h in range(N_HEADS):
            acc_scr[h] = (alpha_scr[h:h + 1, :] * acc_scr[h]
                          + _dot(vt_c, p_scr[h]))
        return carry

    lax.fori_loop(0, n_chunks, att_chunk, 0)

    outs = []
    for h in range(N_HEADS):
        a = acc_scr[h]
        outs.append(a[0:HEAD_DIM, :] / a[HEAD_DIM:HEAD_DIM + 1, :])
    o_ref[...] = jnp.transpose(jnp.concatenate(outs, axis=0)).astype(BF16)


def _attention(q, qi, wi, kv, ki, batch):
    tq, ck = Q_TILE, K_CHUNK
    seq = q.shape[0]
    topk = min(TOPK_MAX, seq // 4)
    nq = seq // tq

    def q_spec(c):
        return pl.BlockSpec((tq, c), lambda b, i: (i, b))

    def kv_spec(c):
        return pl.BlockSpec((seq, c), lambda b, i: (0, b))

    return pl.pallas_call(
        functools.partial(_attn_kernel, topk=topk),
        out_shape=jax.ShapeDtypeStruct((seq, batch * D_ATT), BF16),
        grid=(batch, nq),
        in_specs=[q_spec(D_ATT), q_spec(IDX_HEADS * IDX_DIM), q_spec(LANES),
                  kv_spec(4 * HEAD_DIM), kv_spec(LANES)],
        out_specs=q_spec(D_ATT),
        scratch_shapes=[
            pltpu.VMEM((N_HEADS, tq, LANES), BF16),
            pltpu.VMEM((IDX_HEADS, tq, LANES), BF16),
            pltpu.VMEM((LANES, tq), F32),
            pltpu.VMEM((seq // ck, VT_ROWS, ck), BF16),
            pltpu.VMEM((seq // ck, ck, tq), F32),
            pltpu.VMEM((seq // ck, ck, tq), F32),
            pltpu.VMEM((1, tq), I32),
            pltpu.VMEM((N_HEADS, tq), F32),
            pltpu.VMEM((N_HEADS, VT_ROWS, tq), F32),
            pltpu.VMEM((N_HEADS, SUBLANES, tq), F32),
            pltpu.VMEM((N_HEADS, ck, tq), F32),
            pltpu.VMEM((N_HEADS, ck, tq), BF16),
            pltpu.VMEM((N_HEADS, tq), F32),
        ],
        compiler_params=pltpu.CompilerParams(
            dimension_semantics=("parallel", "arbitrary"),
            vmem_limit_bytes=VMEM_LIMIT),
        name="dsa_attention",
    )(q, qi, wi, kv, ki)


def _merge_kernel(att_ref, rnn_ref, ga_ref, gl_ref, x_ref, wa_ref, wr_ref,
                  wo_ref, o_ref, slab_scr, *, batch):
    att = _seq_to_rows(att_ref, slab_scr, batch).astype(BF16)
    pa = _dot(att, wa_ref[...])
    pr = _dot(rnn_ref[...], wr_ref[...])
    merged = (jax.nn.sigmoid(ga_ref[...]) * pa
              + jax.nn.sigmoid(gl_ref[...]) * pr)
    o_ref[...] = x_ref[...] + _dot(merged.astype(BF16), wo_ref[...])


def _merge(att, rnn, ga, gl, x, wa, wr, wo, batch):
    rows, d = x.shape
    tt = ROW_TILE // batch

    def row_spec(c):
        return pl.BlockSpec((ROW_TILE, c), lambda i: (i, 0))

    return pl.pallas_call(
        functools.partial(_merge_kernel, batch=batch),
        out_shape=jax.ShapeDtypeStruct((rows, d), F32),
        grid=(rows // ROW_TILE,),
        in_specs=[pl.BlockSpec((tt, att.shape[1]), lambda i: (i, 0)),
                  row_spec(rnn.shape[1]), row_spec(d), row_spec(d),
                  row_spec(d), _resident(wa.shape), _resident(wr.shape),
                  _resident(wo.shape)],
        out_specs=row_spec(d),
        scratch_shapes=[pltpu.VMEM((_slabs(D_ATT), ROW_TILE, LANES), F32)],
        compiler_params=pltpu.CompilerParams(
            dimension_semantics=("parallel",), vmem_limit_bytes=VMEM_LIMIT),
        name="merge",
    )(att, rnn, ga, gl, x, wa, wr, wo)


def _inproj_weight(w_in, d_rnn, d_model):
    splits = (D_ATT, HEAD_DIM, HEAD_DIM, IDX_HEADS * IDX_DIM, IDX_DIM,
              IDX_HEADS, d_rnn, d_rnn, d_model, d_model)
    parts, c = [], 0
    for n in splits:
        parts.append(w_in[:, c:c + n])
        c += n
    assert c == w_in.shape[1]
    wq, wk, wv, wqi, wki, wwi, wxr, wgr, wga, wgl = parts
    wi_pad = jnp.zeros((w_in.shape[0], LANES - IDX_HEADS), w_in.dtype)
    cols = [wxr, wgr, wq, wk, wv, wv, wk, wqi] + [wki] * (LANES // IDX_DIM)
    cols += [wwi, wi_pad, wga, wgl]
    return jnp.concatenate(cols, axis=1).astype(BF16)


def _block_diag_tiles(w):
    n_blocks, bw, _ = w.shape
    per = MXU_DIM // bw
    w = w.reshape(n_blocks // per, per, bw, bw)
    eye = jnp.eye(per, dtype=w.dtype)
    t = w[:, :, :, None, :] * eye[None, :, None, :, None]
    return t.reshape(n_blocks // per, MXU_DIM, MXU_DIM).astype(BF16)


def kernel(x, ffn1_norm, ffn1_wg, ffn1_wu, ffn1_wd, mix_norm, w_in, conv_w,
           conv_b, rg_wa, rg_ba, rg_wx, rg_bx, rg_lam, w_att_proj, w_rnn_proj,
           w_out, ffn2_norm, ffn2_wg, ffn2_wu, ffn2_wd, final_norm):
    batch, seq, d = x.shape
    depth = ffn1_norm.shape[0]
    d_rnn = conv_w.shape[-1]
    assert batch == SUBLANES and seq % Q_TILE == 0
    assert (batch * seq) % ROW_TILE == 0 and ROW_TILE % batch == 0

    def row(v):
        return v.reshape(1, -1)

    h = x
    for l in range(depth):
        h = _ffn(h, row(ffn1_norm[l]), ffn1_wg[l].astype(BF16),
                 ffn1_wu[l].astype(BF16), ffn1_wd[l].astype(BF16), batch,
                 first=(l == 0))
        q, kv, qi, ki, wi, ga, gl, rnn = _mixer_in(
            h, row(mix_norm[l]), _inproj_weight(w_in[l], d_rnn, d),
            conv_w[l], row(conv_b[l]), _block_diag_tiles(rg_wa[l]),
            row(rg_ba[l]), _block_diag_tiles(rg_wx[l]), row(rg_bx[l]),
            row(rg_lam[l]), batch)
        att = _attention(q, qi, wi, kv, ki, batch)
        h = _merge(att, rnn, ga, gl, h, w_att_proj[l].astype(BF16),
                   w_rnn_proj[l].astype(BF16), w_out[l].astype(BF16), batch)
        last = l == depth - 1
        h = _ffn(h, row(ffn2_norm[l]), ffn2_wg[l].astype(BF16),
                 ffn2_wu[l].astype(BF16), ffn2_wd[l].astype(BF16), batch,
                 final_g=row(final_norm) if last else None)
    return h
```

```python
import functools

import jax
import jax.numpy as jnp
from jax import lax
from jax.experimental import pallas as pl
from jax.experimental.pallas import tpu as pltpu

F32 = jnp.float32
BF16 = jnp.bfloat16
I32 = jnp.int32

N_HEADS = 8
HEAD_DIM = 64
D_ATT = N_HEADS * HEAD_DIM
IDX_HEADS = 8
IDX_DIM = 32
TOPK_MAX = 256
RNN_BLOCKS = 16
CONV_W = 4
LRU_C = 8.0
EPS = 1e-6

LANES = 128
SUBLANES = 8
MXU_DIM = 256
VMEM_LIMIT = 56 * 1024 * 1024

INT_MIN = -2 ** 31
MASK_BIAS = -1e30

ROW_TILE = 512
Q_TILE = 256
K_CHUNK = 256


def _resident(shape):
    zeros = (0,) * len(shape)
    return pl.BlockSpec(shape, lambda *_: zeros, pipeline_mode=pl.Buffered(1))


def _rmsnorm(x, g):
    ms = jnp.mean(x * x, axis=-1, keepdims=True)
    return x * lax.rsqrt(ms + EPS) * g


def _dot(a, b):
    return jnp.dot(a, b, preferred_element_type=F32)


def _dot_nt(a, b):
    return lax.dot_general(a, b, (((1,), (1,)), ((), ())),
                           preferred_element_type=F32)


def _ff_chunks(d_ff):
    chunks, c0 = [], 0
    while c0 < d_ff:
        c1 = min(c0 + 4 * MXU_DIM, d_ff)
        chunks.append((c0, c1))
        c0 = c1
    return chunks


def _slabs(width):
    return width // LANES


def _batch_to_rows(x_ref, slab_scr, batch):
    tt, d = x_ref.shape[1:]
    for b in range(batch):
        for j in range(_slabs(d)):
            slab_scr[j, pl.ds(b, tt, stride=batch), :] = (
                x_ref[b, :, j * LANES:(j + 1) * LANES])
    return jnp.concatenate([slab_scr[j] for j in range(_slabs(d))], axis=1)


def _rows_to_batch(o_ref, y, slab_scr, batch):
    tt, d = o_ref.shape[1:]
    for j in range(_slabs(d)):
        slab_scr[j] = y[:, j * LANES:(j + 1) * LANES]
    for b in range(batch):
        for j in range(_slabs(d)):
            o_ref[b, :, j * LANES:(j + 1) * LANES] = (
                slab_scr[j, pl.ds(b, tt, stride=batch), :])


def _rows_to_seq(o_ref, z, slab_scr, slab0, batch):
    tt = o_ref.shape[0]
    c = z.shape[1]
    for j in range(_slabs(c)):
        slab_scr[slab0 + j] = z[:, j * LANES:(j + 1) * LANES]
    for b in range(batch):
        for j in range(_slabs(c)):
            o_ref[:, b * c + j * LANES:b * c + (j + 1) * LANES] = (
                slab_scr[slab0 + j, pl.ds(b, tt, stride=batch), :]
                .astype(o_ref.dtype))


def _seq_to_rows(x_ref, slab_scr, batch):
    tt = x_ref.shape[0]
    c = x_ref.shape[1] // batch
    for b in range(batch):
        for j in range(_slabs(c)):
            slab_scr[j, pl.ds(b, tt, stride=batch), :] = (
                x_ref[:, b * c + j * LANES:b * c + (j + 1) * LANES].astype(F32))
    return jnp.concatenate([slab_scr[j] for j in range(_slabs(c))], axis=1)


def _ffn_kernel(x_ref, g_ref, wg_ref, wu_ref, wd_ref, *rest, first, final,
                batch):
    rest = list(rest)
    fg_ref = rest.pop(0) if final else None
    o_ref = rest.pop(0)
    slab_scr = rest.pop(0) if (first or final) else None
    x = _batch_to_rows(x_ref, slab_scr, batch) if first else x_ref[...]
    h = _rmsnorm(x, g_ref[...]).astype(BF16)
    acc = None
    for c0, c1 in _ff_chunks(wg_ref.shape[1]):
        g = _dot(h, wg_ref[:, c0:c1])
        u = _dot(h, wu_ref[:, c0:c1])
        a = (g * jax.nn.sigmoid(g) * u).astype(BF16)
        d = _dot(a, wd_ref[c0:c1, :])
        acc = d if acc is None else acc + d
    y = x + 0.5 * acc
    if final:
        _rows_to_batch(o_ref, _rmsnorm(y, fg_ref[...]), slab_scr, batch)
    else:
        o_ref[...] = y


def _ffn(x, g, wg, wu, wd, batch, first=False, final_g=None):
    final = final_g is not None
    if first:
        _, seq, d = x.shape
        rows = seq * batch
    else:
        rows, d = x.shape
        seq = rows // batch
    d_ff = wg.shape[1]
    tt = ROW_TILE // batch
    row_spec = pl.BlockSpec((ROW_TILE, d), lambda i: (i, 0))
    batch_spec = pl.BlockSpec((batch, tt, d), lambda i: (0, i, 0))
    in_specs = [batch_spec if first else row_spec, _resident((1, d)),
                _resident((d, d_ff)), _resident((d, d_ff)),
                _resident((d_ff, d))]
    args = [x, g, wg, wu, wd]
    if final:
        in_specs.append(_resident((1, d)))
        args.append(final_g)
    scratch = ([pltpu.VMEM((_slabs(d), ROW_TILE, LANES), F32)]
               if (first or final) else [])
    out_shape = (batch, seq, d) if final else (rows, d)
    return pl.pallas_call(
        functools.partial(_ffn_kernel, first=first, final=final, batch=batch),
        out_shape=jax.ShapeDtypeStruct(out_shape, F32),
        grid=(rows // ROW_TILE,),
        in_specs=in_specs,
        out_specs=batch_spec if final else row_spec,
        scratch_shapes=scratch,
        compiler_params=pltpu.CompilerParams(
            dimension_semantics=("parallel",), vmem_limit_bytes=VMEM_LIMIT),
        name="ffn_first" if first else ("ffn_final" if final else "ffn"),
    )(*args)


ATT_WIDTHS = (D_ATT, 4 * HEAD_DIM, IDX_HEADS * IDX_DIM, LANES, LANES)
ATT_DTYPES = (BF16, BF16, BF16, BF16, F32)


def _mixer_in_kernel(x_ref, g_ref, w_ref, cw_ref, cb_ref, wa_ref, ba_ref,
                     wx_ref, bx_ref, lam_ref,
                     q_ref, kv_ref, qi_ref, ki_ref, wi_ref, ga_ref, gl_ref,
                     rnn_ref, slab_scr, xs_scr, a_scr, u_scr, h_scr, hc_scr,
                     *, batch):
    tm = x_ref.shape[0]
    c = cw_ref.shape[1]
    d_model = ga_ref.shape[1]
    halo = (CONV_W - 1) * batch
    h = _rmsnorm(x_ref[...], g_ref[...]).astype(BF16)
    col = 0

    def seg(width):
        nonlocal col
        z = _dot(h, w_ref[:, col:col + width])
        col += width
        return z

    @pl.when(pl.program_id(0) == 0)
    def _():
        xs_scr[0:halo, :] = jnp.zeros((halo, c), F32)
        hc_scr[...] = jnp.zeros(hc_scr.shape, F32)

    for n in range(c // MXU_DIM):
        cs = slice(n * MXU_DIM, (n + 1) * MXU_DIM)
        xr = _dot(h, w_ref[:, cs])
        xs_scr[halo:halo + tm, cs] = xr
        xc = cb_ref[:, cs]
        for j in range(CONV_W):
            xc = xc + cw_ref[j:j + 1, cs] * xs_scr[j * batch:j * batch + tm, cs]
        xs_scr[0:halo, cs] = xr[tm - halo:tm, :]

        xb = xc.astype(BF16)
        r = jax.nn.sigmoid(_dot(xb, wa_ref[n]) + ba_ref[:, cs])
        gate_i = jax.nn.sigmoid(_dot(xb, wx_ref[n]) + bx_ref[:, cs])
        neg_lam = -lam_ref[:, cs]
        softplus = (jnp.maximum(neg_lam, 0.0)
                    + jnp.log1p(jnp.exp(-jnp.abs(neg_lam))))
        log_a = -LRU_C * r * softplus
        a_scr[:, cs] = jnp.exp(log_a)
        th = jnp.tanh(log_a)
        u_scr[:, cs] = jnp.sqrt(-2.0 * th / (1.0 - th)) * gate_i * xc

        hcur = hc_scr[:, cs]
        for t in range(tm // batch):
            rs = slice(t * batch, (t + 1) * batch)
            hcur = a_scr[rs, cs] * hcur + u_scr[rs, cs]
            h_scr[rs, cs] = hcur
        hc_scr[:, cs] = hcur
        gr = _dot(h, w_ref[:, c + n * MXU_DIM:c + (n + 1) * MXU_DIM])
        rnn_ref[:, cs] = (h_scr[:, cs] * jax.nn.gelu(gr)).astype(BF16)
    col = 2 * c

    scales = (HEAD_DIM ** -0.5, None, None, None, (IDX_HEADS * IDX_DIM) ** -0.5)
    slab0 = 0
    for o_ref, width, scale in zip((q_ref, kv_ref, qi_ref, ki_ref, wi_ref),
                                   ATT_WIDTHS, scales):
        z = seg(width)
        if scale is not None:
            z = z * scale
        _rows_to_seq(o_ref, z, slab_scr, slab0, batch)
        slab0 += _slabs(width)

    ga_ref[...] = seg(d_model)
    gl_ref[...] = seg(d_model)


def _mixer_in(x, g, w, cw, cb, wa, ba, wx, bx, lam, batch):
    rows, d = x.shape
    c = cw.shape[1]
    tm = ROW_TILE
    tt = tm // batch
    seq = rows // batch
    halo = (CONV_W - 1) * batch
    assert w.shape[1] == 2 * c + sum(ATT_WIDTHS) + 2 * d

    def row_spec(n):
        return pl.BlockSpec((tm, n), lambda i: (i, 0))

    def seq_spec(n):
        return pl.BlockSpec((tt, batch * n), lambda i: (i, 0))

    out_shape = [jax.ShapeDtypeStruct((seq, batch * n), t)
                 for n, t in zip(ATT_WIDTHS, ATT_DTYPES)]
    out_shape += [jax.ShapeDtypeStruct((rows, d), F32)] * 2
    out_shape += [jax.ShapeDtypeStruct((rows, c), BF16)]
    out_specs = [seq_spec(n) for n in ATT_WIDTHS]
    out_specs += [row_spec(d), row_spec(d), row_spec(c)]
    return pl.pallas_call(
        functools.partial(_mixer_in_kernel, batch=batch),
        out_shape=out_shape,
        grid=(rows // tm,),
        in_specs=[row_spec(d), _resident((1, d)), _resident(w.shape),
                  _resident(cw.shape), _resident((1, c)), _resident(wa.shape),
                  _resident((1, c)), _resident(wx.shape), _resident((1, c)),
                  _resident((1, c))],
        out_specs=out_specs,
        scratch_shapes=[
            pltpu.VMEM((_slabs(sum(ATT_WIDTHS)), tm, LANES), F32),
            pltpu.VMEM((halo + tm, c), F32),
            pltpu.VMEM((tm, c), F32), pltpu.VMEM((tm, c), F32),
            pltpu.VMEM((tm, c), F32), pltpu.VMEM((batch, c), F32)],
        compiler_params=pltpu.CompilerParams(
            dimension_semantics=("arbitrary",), vmem_limit_bytes=VMEM_LIMIT),
        name="mixer_in",
    )(x, g, w, cw, cb, wa, ba, wx, bx, lam)


VT_ROWS = HEAD_DIM + 16


def _key_to_float(key):
    bits = jnp.where(key < 0, INT_MIN - key, key)
    return pltpu.bitcast(bits, F32)


def _attn_kernel(q_ref, qi_ref, wi_ref, kv_ref, ki_ref, o_ref,
                 qm_scr, qim_scr, wit_scr, vt_scr, sc_scr, sall_scr, m_scr,
                 acc_scr, mpart_scr, *, topk):
    tq, ck = Q_TILE, K_CHUNK
    n_kc = sc_scr.shape[0]
    seq = n_kc * ck
    qt = pl.program_id(1)
    n_chunks = (qt * tq) // ck + tq // ck
    lane = lax.broadcasted_iota(I32, (tq, LANES), 1)
    key_iota = lax.broadcasted_iota(I32, (ck, tq), 0)
    qry_iota = lax.broadcasted_iota(I32, (ck, tq), 1)
    kf = float(topk)

    for h in range(N_HEADS):
        qp = q_ref[:, (h // 2) * LANES:(h // 2 + 1) * LANES].astype(F32)
        lo = (h % 2) * HEAD_DIM
        qm_scr[h] = jnp.where((lane >= lo) & (lane < lo + HEAD_DIM), qp,
                              0.0).astype(BF16)
    for h in range(IDX_HEADS):
        qp = qi_ref[:, (h // 4) * LANES:(h // 4 + 1) * LANES].astype(F32)
        lo = (h % 4) * IDX_DIM
        qim_scr[h] = jnp.where((lane >= lo) & (lane < lo + IDX_DIM), qp,
                               0.0).astype(BF16)
    wit_scr[...] = jnp.transpose(wi_ref[...])

    @pl.when(qt == 0)
    def _():
        lane_c = lax.broadcasted_iota(I32, (ck, LANES), 1)
        for c in range(n_kc):
            v1 = jnp.where(lane_c < HEAD_DIM,
                           kv_ref[c * ck:(c + 1) * ck, LANES:2 * LANES].astype(F32),
                           1.0)
            vt_scr[c] = jnp.transpose(v1)[0:VT_ROWS, :].astype(BF16)

    def score_chunk(kc, carry):
        k0 = pl.multiple_of(kc * ck, ck)
        kic = ki_ref[pl.ds(k0, ck), :]
        acc = jnp.zeros((ck, tq), F32)
        for h in range(IDX_HEADS):
            acc = acc + (jnp.maximum(_dot_nt(kic, qim_scr[h]), 0.0)
                         * wit_scr[h:h + 1, :])
        causal = key_iota - qry_iota <= qt * tq - kc * ck
        sc_scr[kc] = jnp.where(causal, acc, -jnp.inf)
        return carry

    lax.fori_loop(0, n_chunks, score_chunk, 0)

    def count(pred):
        acc_rows = 2 * SUBLANES

        def chunk(kc, c):
            hit = jnp.where(pred(sc_scr[kc], kc), 1.0, 0.0)
            return c + jnp.sum(hit.reshape(ck // acc_rows, acc_rows, tq), axis=0)

        c = lax.fori_loop(0, n_chunks, chunk, jnp.zeros((acc_rows, tq), F32))
        return jnp.sum(c, axis=0, keepdims=True)

    def bit_step(i, thr):
        cand = thr + lax.shift_left(jnp.int32(1), 31 - i)
        cand_f = _key_to_float(cand)
        tot = count(lambda s, kc: s >= cand_f)
        return jnp.where(tot >= kf, cand, thr)

    thr = lax.fori_loop(0, 32, bit_step, jnp.full((1, tq), INT_MIN, I32))

    thr_f = _key_to_float(thr)
    short = jnp.logical_not(thr_f > -jnp.inf)
    thr_f = jnp.where(short, -jnp.inf, thr_f)
    n_gt = count(lambda s, kc: s > thr_f)
    need = jnp.where(short, 0.0, kf - n_gt)
    lower_tri = jnp.where(
        lax.broadcasted_iota(I32, (ck, ck), 0)
        >= lax.broadcasted_iota(I32, (ck, ck), 1), 1.0, 0.0).astype(BF16)

    mpart_scr[...] = jnp.full(mpart_scr.shape, MASK_BIAS, F32)
    acc_scr[...] = jnp.zeros(acc_scr.shape, F32)

    def scores_chunk(kc, ties_before):
        k0 = pl.multiple_of(kc * ck, ck)
        s_idx = sc_scr[kc]
        tie = s_idx == thr_f
        rank = ties_before + _dot(lower_tri,
                                  jnp.where(tie, 1.0, 0.0).astype(BF16))
        keep = (s_idx > thr_f) | (tie & (rank <= need))
        bias = jnp.where(keep, 0.0, MASK_BIAS)
        for h in range(N_HEADS):
            c0 = 0 if h % 2 == 0 else LANES
            k_mat = kv_ref[pl.ds(k0, ck), c0:c0 + LANES]
            s = _dot_nt(k_mat, qm_scr[h]) + bias
            sall_scr[kc, h] = s
            mpart_scr[h] = jnp.maximum(
                mpart_scr[h],
                jnp.max(s.reshape(ck // SUBLANES, SUBLANES, tq), axis=0))
        return rank[ck - 1:ck, :]

    lax.fori_loop(0, n_chunks, scores_chunk, jnp.zeros((1, tq), F32))
    for h in range(N_HEADS):
        m_scr[h:h + 1, :] = jnp.max(mpart_scr[h], axis=0, keepdims=True)

    def value_chunk(kc, carry):
        vt_c = vt_scr[kc]
        for h in range(N_HEADS):
            p = jnp.exp(sall_scr[kc, h] - m_scr[h:h + 1, :]).astype(BF16)
            acc_scr[h] = acc_scr[h] + _dot(vt_c, p)
        return carry

    lax.fori_loop(0, n_chunks, value_chunk, 0)

    outs = []
    for h in range(N_HEADS):
        a = acc_scr[h]
        outs.append(a[0:HEAD_DIM, :] / a[HEAD_DIM:HEAD_DIM + 1, :])
    o_ref[...] = jnp.transpose(jnp.concatenate(outs, axis=0)).astype(BF16)


def _attention(q, qi, wi, kv, ki, batch):
    tq, ck = Q_TILE, K_CHUNK
    seq = q.shape[0]
    topk = min(TOPK_MAX, seq // 4)
    nq = seq // tq

    def q_spec(c):
        return pl.BlockSpec((tq, c), lambda b, i: (i, b))

    def kv_spec(c):
        return pl.BlockSpec((seq, c), lambda b, i: (0, b))

    return pl.pallas_call(
        functools.partial(_attn_kernel, topk=topk),
        out_shape=jax.ShapeDtypeStruct((seq, batch * D_ATT), BF16),
        grid=(batch, nq),
        in_specs=[q_spec(D_ATT), q_spec(IDX_HEADS * IDX_DIM), q_spec(LANES),
                  kv_spec(4 * HEAD_DIM), kv_spec(LANES)],
        out_specs=q_spec(D_ATT),
        scratch_shapes=[
            pltpu.VMEM((N_HEADS, tq, LANES), BF16),
            pltpu.VMEM((IDX_HEADS, tq, LANES), BF16),
            pltpu.VMEM((LANES, tq), F32),
            pltpu.VMEM((seq // ck, VT_ROWS, ck), BF16),
            pltpu.VMEM((seq // ck, ck, tq), F32),
            pltpu.VMEM((seq // ck, N_HEADS, ck, tq), F32),
            pltpu.VMEM((N_HEADS, tq), F32),
            pltpu.VMEM((N_HEADS, VT_ROWS, tq), F32),
            pltpu.VMEM((N_HEADS, SUBLANES, tq), F32),
        ],
        compiler_params=pltpu.CompilerParams(
            dimension_semantics=("parallel", "arbitrary"),
            vmem_limit_bytes=VMEM_LIMIT),
        name="dsa_attention",
    )(q, qi, wi, kv, ki)


def _merge_kernel(att_ref, rnn_ref, ga_ref, gl_ref, x_ref, wa_ref, wr_ref,
                  wo_ref, o_ref, slab_scr, *, batch):
    att = _seq_to_rows(att_ref, slab_scr, batch).astype(BF16)
    pa = _dot(att, wa_ref[...])
    pr = _dot(rnn_ref[...], wr_ref[...])
    merged = (jax.nn.sigmoid(ga_ref[...]) * pa
              + jax.nn.sigmoid(gl_ref[...]) * pr)
    o_ref[...] = x_ref[...] + _dot(merged.astype(BF16), wo_ref[...])


def _merge(att, rnn, ga, gl, x, wa, wr, wo, batch):
    rows, d = x.shape
    tt = ROW_TILE // batch

    def row_spec(c):
        return pl.BlockSpec((ROW_TILE, c), lambda i: (i, 0))

    return pl.pallas_call(
        functools.partial(_merge_kernel, batch=batch),
        out_shape=jax.ShapeDtypeStruct((rows, d), F32),
        grid=(rows // ROW_TILE,),
        in_specs=[pl.BlockSpec((tt, att.shape[1]), lambda i: (i, 0)),
                  row_spec(rnn.shape[1]), row_spec(d), row_spec(d),
                  row_spec(d), _resident(wa.shape), _resident(wr.shape),
                  _resident(wo.shape)],
        out_specs=row_spec(d),
        scratch_shapes=[pltpu.VMEM((_slabs(D_ATT), ROW_TILE, LANES), F32)],
        compiler_params=pltpu.CompilerParams(
            dimension_semantics=("parallel",), vmem_limit_bytes=VMEM_LIMIT),
        name="merge",
    )(att, rnn, ga, gl, x, wa, wr, wo)


def _inproj_weight(w_in, d_rnn, d_model):
    splits = (D_ATT, HEAD_DIM, HEAD_DIM, IDX_HEADS * IDX_DIM, IDX_DIM,
              IDX_HEADS, d_rnn, d_rnn, d_model, d_model)
    parts, c = [], 0
    for n in splits:
        parts.append(w_in[:, c:c + n])
        c += n
    assert c == w_in.shape[1]
    wq, wk, wv, wqi, wki, wwi, wxr, wgr, wga, wgl = parts
    wi_pad = jnp.zeros((w_in.shape[0], LANES - IDX_HEADS), w_in.dtype)
    cols = [wxr, wgr, wq, wk, wv, wv, wk, wqi] + [wki] * (LANES // IDX_DIM)
    cols += [wwi, wi_pad, wga, wgl]
    return jnp.concatenate(cols, axis=1).astype(BF16)


def _block_diag_tiles(w):
    n_blocks, bw, _ = w.shape
    per = MXU_DIM // bw
    w = w.reshape(n_blocks // per, per, bw, bw)
    eye = jnp.eye(per, dtype=w.dtype)
    t = w[:, :, :, None, :] * eye[None, :, None, :, None]
    return t.reshape(n_blocks // per, MXU_DIM, MXU_DIM).astype(BF16)


def kernel(x, ffn1_norm, ffn1_wg, ffn1_wu, ffn1_wd, mix_norm, w_in, conv_w,
           conv_b, rg_wa, rg_ba, rg_wx, rg_bx, rg_lam, w_att_proj, w_rnn_proj,
           w_out, ffn2_norm, ffn2_wg, ffn2_wu, ffn2_wd, final_norm):
    batch, seq, d = x.shape
    depth = ffn1_norm.shape[0]
    d_rnn = conv_w.shape[-1]
    assert batch == SUBLANES and seq % Q_TILE == 0
    assert (batch * seq) % ROW_TILE == 0 and ROW_TILE % batch == 0

    def row(v):
        return v.reshape(1, -1)

    h = x
    for l in range(depth):
        h = _ffn(h, row(ffn1_norm[l]), ffn1_wg[l].astype(BF16),
                 ffn1_wu[l].astype(BF16), ffn1_wd[l].astype(BF16), batch,
                 first=(l == 0))
        q, kv, qi, ki, wi, ga, gl, rnn = _mixer_in(
            h, row(mix_norm[l]), _inproj_weight(w_in[l], d_rnn, d),
            conv_w[l], row(conv_b[l]), _block_diag_tiles(rg_wa[l]),
            row(rg_ba[l]), _block_diag_tiles(rg_wx[l]), row(rg_bx[l]),
            row(rg_lam[l]), batch)
        att = _attention(q, qi, wi, kv, ki, batch)
        h = _merge(att, rnn, ga, gl, h, w_att_proj[l].astype(BF16),
                   w_rnn_proj[l].astype(BF16), w_out[l].astype(BF16), batch)
        last = l == depth - 1
        h = _ffn(h, row(ffn2_norm[l]), ffn2_wg[l].astype(BF16),
                 ffn2_wu[l].astype(BF16), ffn2_wd[l].astype(BF16), batch,
                 final_g=row(final_norm) if last else None)
    return h
```

```python
import functools

import jax
import jax.numpy as jnp
from jax import lax
from jax.experimental import pallas as pl
from jax.experimental.pallas import tpu as pltpu

F32 = jnp.float32
BF16 = jnp.bfloat16
I32 = jnp.int32

N_HEADS = 8
HEAD_DIM = 64
D_ATT = N_HEADS * HEAD_DIM
IDX_HEADS = 8
IDX_DIM = 32
TOPK_MAX = 256
RNN_BLOCKS = 16
CONV_W = 4
LRU_C = 8.0
EPS = 1e-6

LANES = 128
SUBLANES = 8
MXU_DIM = 256
VMEM_LIMIT = 56 * 1024 * 1024

INT_MIN = -2 ** 31
MASK_BIAS = -1e30

ROW_TILE = 512
Q_TILE = 256
K_CHUNK = 256


def _resident(shape, layer=None):
    zeros = (0,) * len(shape)
    if layer is None:
        return pl.BlockSpec(shape, lambda *_: zeros,
                            pipeline_mode=pl.Buffered(1))
    return pl.BlockSpec((None,) + tuple(shape), lambda *_: (layer,) + zeros,
                        pipeline_mode=pl.Buffered(1))


def _rmsnorm(x, g):
    ms = jnp.mean(x * x, axis=-1, keepdims=True)
    return x * lax.rsqrt(ms + EPS) * g


def _dot(a, b):
    return jnp.dot(a, b, preferred_element_type=F32)


def _dot_nt(a, b):
    return lax.dot_general(a, b, (((1,), (1,)), ((), ())),
                           preferred_element_type=F32)


def _ff_chunks(d_ff):
    chunks, c0 = [], 0
    while c0 < d_ff:
        c1 = min(c0 + 4 * MXU_DIM, d_ff)
        chunks.append((c0, c1))
        c0 = c1
    return chunks


def _slabs(width):
    return width // LANES


def _batch_to_rows(x_ref, slab_scr, batch):
    tt, d = x_ref.shape[1:]
    for b in range(batch):
        for j in range(_slabs(d)):
            slab_scr[j, pl.ds(b, tt, stride=batch), :] = (
                x_ref[b, :, j * LANES:(j + 1) * LANES])
    return jnp.concatenate([slab_scr[j] for j in range(_slabs(d))], axis=1)


def _rows_to_batch(o_ref, y, slab_scr, batch):
    tt, d = o_ref.shape[1:]
    for j in range(_slabs(d)):
        slab_scr[j] = y[:, j * LANES:(j + 1) * LANES]
    for b in range(batch):
        for j in range(_slabs(d)):
            o_ref[b, :, j * LANES:(j + 1) * LANES] = (
                slab_scr[j, pl.ds(b, tt, stride=batch), :])


def _rows_to_seq(o_ref, z, slab_scr, slab0, batch):
    tt = o_ref.shape[0]
    c = z.shape[1]
    for j in range(_slabs(c)):
        slab_scr[slab0 + j] = z[:, j * LANES:(j + 1) * LANES]
    for b in range(batch):
        for j in range(_slabs(c)):
            o_ref[:, b * c + j * LANES:b * c + (j + 1) * LANES] = (
                slab_scr[slab0 + j, pl.ds(b, tt, stride=batch), :]
                .astype(o_ref.dtype))


def _seq_to_rows(x_ref, slab_scr, batch):
    tt = x_ref.shape[0]
    c = x_ref.shape[1] // batch
    for b in range(batch):
        for j in range(_slabs(c)):
            slab_scr[j, pl.ds(b, tt, stride=batch), :] = (
                x_ref[:, b * c + j * LANES:b * c + (j + 1) * LANES].astype(F32))
    return jnp.concatenate([slab_scr[j] for j in range(_slabs(c))], axis=1)


def _ffn_kernel(x_ref, g_ref, wg_ref, wu_ref, wd_ref, *rest, first, final,
                batch):
    rest = list(rest)
    fg_ref = rest.pop(0) if final else None
    o_ref = rest.pop(0)
    slab_scr = rest.pop(0) if (first or final) else None
    x = _batch_to_rows(x_ref, slab_scr, batch) if first else x_ref[...]
    h = _rmsnorm(x, g_ref[...]).astype(BF16)
    acc = None
    for c0, c1 in _ff_chunks(wg_ref.shape[1]):
        g = _dot(h, wg_ref[:, c0:c1])
        u = _dot(h, wu_ref[:, c0:c1])
        a = (g * jax.nn.sigmoid(g) * u).astype(BF16)
        d = _dot(a, wd_ref[c0:c1, :])
        acc = d if acc is None else acc + d
    y = x + 0.5 * acc
    if final:
        _rows_to_batch(o_ref, _rmsnorm(y, fg_ref[...]), slab_scr, batch)
    else:
        o_ref[...] = y


def _ffn(x, g, wg, wu, wd, layer, batch, first=False, final_g=None):
    final = final_g is not None
    if first:
        _, seq, d = x.shape
        rows = seq * batch
    else:
        rows, d = x.shape
        seq = rows // batch
    d_ff = wg.shape[2]
    tt = ROW_TILE // batch
    row_spec = pl.BlockSpec((ROW_TILE, d), lambda i: (i, 0))
    batch_spec = pl.BlockSpec((batch, tt, d), lambda i: (0, i, 0))
    in_specs = [batch_spec if first else row_spec, _resident((1, d)),
                _resident((d, d_ff), layer), _resident((d, d_ff), layer),
                _resident((d_ff, d), layer)]
    args = [x, g, wg, wu, wd]
    if final:
        in_specs.append(_resident((1, d)))
        args.append(final_g)
    scratch = ([pltpu.VMEM((_slabs(d), ROW_TILE, LANES), F32)]
               if (first or final) else [])
    out_shape = (batch, seq, d) if final else (rows, d)
    return pl.pallas_call(
        functools.partial(_ffn_kernel, first=first, final=final, batch=batch),
        out_shape=jax.ShapeDtypeStruct(out_shape, F32),
        grid=(rows // ROW_TILE,),
        in_specs=in_specs,
        out_specs=batch_spec if final else row_spec,
        scratch_shapes=scratch,
        compiler_params=pltpu.CompilerParams(
            dimension_semantics=("parallel",), vmem_limit_bytes=VMEM_LIMIT),
        name="ffn_first" if first else ("ffn_final" if final else "ffn"),
    )(*args)


ATT_WIDTHS = (D_ATT, 4 * HEAD_DIM, IDX_HEADS * IDX_DIM, LANES, LANES)
ATT_DTYPES = (BF16, BF16, BF16, BF16, F32)


def _mixer_in_kernel(x_ref, g_ref, w_ref, cw_ref, cb_ref, wa_ref, ba_ref,
                     wx_ref, bx_ref, lam_ref,
                     q_ref, kv_ref, qi_ref, ki_ref, wi_ref, ga_ref, gl_ref,
                     rnn_ref, slab_scr, xs_scr, a_scr, u_scr, h_scr, hc_scr,
                     *, batch):
    tm = x_ref.shape[0]
    c = cw_ref.shape[1]
    d_model = ga_ref.shape[1]
    halo = (CONV_W - 1) * batch
    h = _rmsnorm(x_ref[...], g_ref[...]).astype(BF16)

    @pl.when(pl.program_id(0) == 0)
    def _():
        xs_scr[0:halo, :] = jnp.zeros((halo, c), F32)
        hc_scr[...] = jnp.zeros(hc_scr.shape, F32)

    scales = (HEAD_DIM ** -0.5, None, None, None, (IDX_HEADS * IDX_DIM) ** -0.5)
    att_refs = (q_ref, kv_ref, qi_ref, ki_ref, wi_ref)
    att_col0 = 2 * c
    gate_col0 = att_col0 + sum(ATT_WIDTHS)

    def att_operand(i):
        col = att_col0 + sum(ATT_WIDTHS[:i])
        z = _dot(h, w_ref[:, col:col + ATT_WIDTHS[i]])
        if scales[i] is not None:
            z = z * scales[i]
        _rows_to_seq(att_refs[i], z, slab_scr, _slabs(sum(ATT_WIDTHS[:i])),
                     batch)

    def gate_logits(o_ref):
        col = gate_col0 if o_ref is ga_ref else gate_col0 + d_model
        o_ref[...] = _dot(h, w_ref[:, col:col + d_model])

    fillers = (lambda: gate_logits(ga_ref),
               lambda: gate_logits(gl_ref),
               lambda: (att_operand(0), att_operand(1)),
               lambda: (att_operand(2), att_operand(3), att_operand(4)))

    n_tiles = c // MXU_DIM
    assert n_tiles == len(fillers)
    for n in range(n_tiles):
        cs = slice(n * MXU_DIM, (n + 1) * MXU_DIM)
        xr = _dot(h, w_ref[:, cs])
        gr = _dot(h, w_ref[:, c + n * MXU_DIM:c + (n + 1) * MXU_DIM])
        xs_scr[halo:halo + tm, cs] = xr
        xc = cb_ref[:, cs]
        for j in range(CONV_W):
            xc = xc + cw_ref[j:j + 1, cs] * xs_scr[j * batch:j * batch + tm, cs]
        xs_scr[0:halo, cs] = xr[tm - halo:tm, :]

        xb = xc.astype(BF16)
        r = jax.nn.sigmoid(_dot(xb, wa_ref[n]) + ba_ref[:, cs])
        gate_i = jax.nn.sigmoid(_dot(xb, wx_ref[n]) + bx_ref[:, cs])
        fillers[n]()
        neg_lam = -lam_ref[:, cs]
        softplus = (jnp.maximum(neg_lam, 0.0)
                    + jnp.log1p(jnp.exp(-jnp.abs(neg_lam))))
        log_a = -LRU_C * r * softplus
        a_scr[:, cs] = jnp.exp(log_a)
        th = jnp.tanh(log_a)
        u_scr[:, cs] = jnp.sqrt(-2.0 * th / (1.0 - th)) * gate_i * xc

        hcur = hc_scr[:, cs]
        for t in range(tm // batch):
            rs = slice(t * batch, (t + 1) * batch)
            hcur = a_scr[rs, cs] * hcur + u_scr[rs, cs]
            h_scr[rs, cs] = hcur
        hc_scr[:, cs] = hcur
        rnn_ref[:, cs] = (h_scr[:, cs] * jax.nn.gelu(gr)).astype(BF16)


def _mixer_in(x, g, w, cw, cb, wa, ba, wx, bx, lam, layer, batch):
    rows, d = x.shape
    c = cw.shape[1]
    tm = ROW_TILE
    tt = tm // batch
    seq = rows // batch
    halo = (CONV_W - 1) * batch
    assert w.shape[2] == 2 * c + sum(ATT_WIDTHS) + 2 * d

    def row_spec(n):
        return pl.BlockSpec((tm, n), lambda i: (i, 0))

    def seq_spec(n):
        return pl.BlockSpec((tt, batch * n), lambda i: (i, 0))

    out_shape = [jax.ShapeDtypeStruct((seq, batch * n), t)
                 for n, t in zip(ATT_WIDTHS, ATT_DTYPES)]
    out_shape += [jax.ShapeDtypeStruct((rows, d), F32)] * 2
    out_shape += [jax.ShapeDtypeStruct((rows, c), BF16)]
    out_specs = [seq_spec(n) for n in ATT_WIDTHS]
    out_specs += [row_spec(d), row_spec(d), row_spec(c)]
    return pl.pallas_call(
        functools.partial(_mixer_in_kernel, batch=batch),
        out_shape=out_shape,
        grid=(rows // tm,),
        in_specs=[row_spec(d), _resident((1, d)),
                  _resident(w.shape[1:], layer), _resident(cw.shape),
                  _resident((1, c)), _resident(wa.shape[1:], layer),
                  _resident((1, c)), _resident(wx.shape[1:], layer),
                  _resident((1, c)), _resident((1, c))],
        out_specs=out_specs,
        scratch_shapes=[
            pltpu.VMEM((_slabs(sum(ATT_WIDTHS)), tm, LANES), F32),
            pltpu.VMEM((halo + tm, c), F32),
            pltpu.VMEM((tm, c), F32), pltpu.VMEM((tm, c), F32),
            pltpu.VMEM((tm, c), F32), pltpu.VMEM((batch, c), F32)],
        compiler_params=pltpu.CompilerParams(
            dimension_semantics=("arbitrary",), vmem_limit_bytes=VMEM_LIMIT),
        name="mixer_in",
    )(x, g, w, cw, cb, wa, ba, wx, bx, lam)


VT_ROWS = HEAD_DIM + 16


def _key_to_float(key):
    bits = jnp.where(key < 0, INT_MIN - key, key)
    return pltpu.bitcast(bits, F32)


def _attn_kernel(q_ref, qi_ref, wi_ref, kv_ref, ki_ref, o_ref,
                 qm_scr, qim_scr, wit_scr, vt_scr, sc_scr, sall_scr, m_scr,
                 acc_scr, mpart_scr, *, topk):
    tq, ck = Q_TILE, K_CHUNK
    n_kc = sc_scr.shape[0]
    seq = n_kc * ck
    qt = pl.program_id(1)
    n_chunks = (qt * tq) // ck + tq // ck
    lane = lax.broadcasted_iota(I32, (tq, LANES), 1)
    key_iota = lax.broadcasted_iota(I32, (ck, tq), 0)
    qry_iota = lax.broadcasted_iota(I32, (ck, tq), 1)
    kf = float(topk)

    for h in range(N_HEADS):
        qp = q_ref[:, (h // 2) * LANES:(h // 2 + 1) * LANES].astype(F32)
        lo = (h % 2) * HEAD_DIM
        qm_scr[h] = jnp.where((lane >= lo) & (lane < lo + HEAD_DIM), qp,
                              0.0).astype(BF16)
    for h in range(IDX_HEADS):
        qp = qi_ref[:, (h // 4) * LANES:(h // 4 + 1) * LANES].astype(F32)
        lo = (h % 4) * IDX_DIM
        qim_scr[h] = jnp.where((lane >= lo) & (lane < lo + IDX_DIM), qp,
                               0.0).astype(BF16)
    wit_scr[...] = jnp.transpose(wi_ref[...])

    @pl.when(qt == 0)
    def _():
        lane_c = lax.broadcasted_iota(I32, (ck, LANES), 1)
        for c in range(n_kc):
            v1 = jnp.where(lane_c < HEAD_DIM,
                           kv_ref[c * ck:(c + 1) * ck, LANES:2 * LANES].astype(F32),
                           1.0)
            vt_scr[c] = jnp.transpose(v1)[0:VT_ROWS, :].astype(BF16)

    def score_chunk(kc, carry):
        k0 = pl.multiple_of(kc * ck, ck)
        kic = ki_ref[pl.ds(k0, ck), :]
        acc = jnp.zeros((ck, tq), F32)
        for h in range(IDX_HEADS):
            acc = acc + (jnp.maximum(_dot_nt(kic, qim_scr[h]), 0.0)
                         * wit_scr[h:h + 1, :])
        causal = key_iota - qry_iota <= qt * tq - kc * ck
        sc_scr[kc] = jnp.where(causal, acc, -jnp.inf)
        return carry

    lax.fori_loop(0, n_chunks, score_chunk, 0)

    def count(pred):
        acc_rows = 2 * SUBLANES

        def chunk(kc, c):
            hit = jnp.where(pred(sc_scr[kc], kc), 1.0, 0.0)
            return c + jnp.sum(hit.reshape(ck // acc_rows, acc_rows, tq), axis=0)

        c = lax.fori_loop(0, n_chunks, chunk, jnp.zeros((acc_rows, tq), F32))
        return jnp.sum(c, axis=0, keepdims=True)

    def bit_step(i, thr):
        cand = thr + lax.shift_left(jnp.int32(1), 31 - i)
        cand_f = _key_to_float(cand)
        tot = count(lambda s, kc: s >= cand_f)
        return jnp.where(tot >= kf, cand, thr)

    thr = lax.fori_loop(0, 32, bit_step, jnp.full((1, tq), INT_MIN, I32))

    thr_f = _key_to_float(thr)
    short = jnp.logical_not(thr_f > -jnp.inf)
    thr_f = jnp.where(short, -jnp.inf, thr_f)
    n_gt = count(lambda s, kc: s > thr_f)
    need = jnp.where(short, 0.0, kf - n_gt)
    lower_tri = jnp.where(
        lax.broadcasted_iota(I32, (ck, ck), 0)
        >= lax.broadcasted_iota(I32, (ck, ck), 1), 1.0, 0.0).astype(BF16)

    mpart_scr[...] = jnp.full(mpart_scr.shape, MASK_BIAS, F32)
    acc_scr[...] = jnp.zeros(acc_scr.shape, F32)

    def scores_chunk(kc, ties_before):
        k0 = pl.multiple_of(kc * ck, ck)
        s_idx = sc_scr[kc]
        tie = s_idx == thr_f
        rank = ties_before + _dot(lower_tri,
                                  jnp.where(tie, 1.0, 0.0).astype(BF16))
        keep = (s_idx > thr_f) | (tie & (rank <= need))
        bias = jnp.where(keep, 0.0, MASK_BIAS)
        for h in range(N_HEADS):
            c0 = 0 if h % 2 == 0 else LANES
            k_mat = kv_ref[pl.ds(k0, ck), c0:c0 + LANES]
            s = _dot_nt(k_mat, qm_scr[h]) + bias
            sall_scr[kc, h] = s
            mpart_scr[h] = jnp.maximum(
                mpart_scr[h],
                jnp.max(s.reshape(ck // SUBLANES, SUBLANES, tq), axis=0))
        return rank[ck - 1:ck, :]

    lax.fori_loop(0, n_chunks, scores_chunk, jnp.zeros((1, tq), F32))
    for h in range(N_HEADS):
        m_scr[h:h + 1, :] = jnp.max(mpart_scr[h], axis=0, keepdims=True)

    def value_chunk(kc, carry):
        vt_c = vt_scr[kc]
        for h in range(N_HEADS):
            p = jnp.exp(sall_scr[kc, h] - m_scr[h:h + 1, :]).astype(BF16)
            acc_scr[h] = acc_scr[h] + _dot(vt_c, p)
        return carry

    lax.fori_loop(0, n_chunks, value_chunk, 0)

    outs = []
    for h in range(N_HEADS):
        a = acc_scr[h]
        outs.append(a[0:HEAD_DIM, :] / a[HEAD_DIM:HEAD_DIM + 1, :])
    o_ref[...] = jnp.transpose(jnp.concatenate(outs, axis=0)).astype(BF16)


def _attention(q, qi, wi, kv, ki, batch):
    tq, ck = Q_TILE, K_CHUNK
    seq = q.shape[0]
    topk = min(TOPK_MAX, seq // 4)
    nq = seq // tq

    def q_spec(c):
        return pl.BlockSpec((tq, c), lambda b, i: (i, b))

    def kv_spec(c):
        return pl.BlockSpec((seq, c), lambda b, i: (0, b))

    return pl.pallas_call(
        functools.partial(_attn_kernel, topk=topk),
        out_shape=jax.ShapeDtypeStruct((seq, batch * D_ATT), BF16),
        grid=(batch, nq),
        in_specs=[q_spec(D_ATT), q_spec(IDX_HEADS * IDX_DIM), q_spec(LANES),
                  kv_spec(4 * HEAD_DIM), kv_spec(LANES)],
        out_specs=q_spec(D_ATT),
        scratch_shapes=[
            pltpu.VMEM((N_HEADS, tq, LANES), BF16),
            pltpu.VMEM((IDX_HEADS, tq, LANES), BF16),
            pltpu.VMEM((LANES, tq), F32),
            pltpu.VMEM((seq // ck, VT_ROWS, ck), BF16),
            pltpu.VMEM((seq // ck, ck, tq), F32),
            pltpu.VMEM((seq // ck, N_HEADS, ck, tq), F32),
            pltpu.VMEM((N_HEADS, tq), F32),
            pltpu.VMEM((N_HEADS, VT_ROWS, tq), F32),
            pltpu.VMEM((N_HEADS, SUBLANES, tq), F32),
        ],
        compiler_params=pltpu.CompilerParams(
            dimension_semantics=("parallel", "arbitrary"),
            vmem_limit_bytes=VMEM_LIMIT),
        name="dsa_attention",
    )(q, qi, wi, kv, ki)


def _merge_kernel(att_ref, rnn_ref, ga_ref, gl_ref, x_ref, wa_ref, wr_ref,
                  wo_ref, o_ref, slab_scr, *, batch):
    att = _seq_to_rows(att_ref, slab_scr, batch).astype(BF16)
    pa = _dot(att, wa_ref[...])
    pr = _dot(rnn_ref[...], wr_ref[...])
    merged = (jax.nn.sigmoid(ga_ref[...]) * pa
              + jax.nn.sigmoid(gl_ref[...]) * pr)
    o_ref[...] = x_ref[...] + _dot(merged.astype(BF16), wo_ref[...])


def _merge(att, rnn, ga, gl, x, wa, wr, wo, layer, batch):
    rows, d = x.shape
    tt = ROW_TILE // batch

    def row_spec(c):
        return pl.BlockSpec((ROW_TILE, c), lambda i: (i, 0))

    return pl.pallas_call(
        functools.partial(_merge_kernel, batch=batch),
        out_shape=jax.ShapeDtypeStruct((rows, d), F32),
        grid=(rows // ROW_TILE,),
        in_specs=[pl.BlockSpec((tt, att.shape[1]), lambda i: (i, 0)),
                  row_spec(rnn.shape[1]), row_spec(d), row_spec(d),
                  row_spec(d), _resident(wa.shape[1:], layer),
                  _resident(wr.shape[1:], layer),
                  _resident(wo.shape[1:], layer)],
        out_specs=row_spec(d),
        scratch_shapes=[pltpu.VMEM((_slabs(D_ATT), ROW_TILE, LANES), F32)],
        compiler_params=pltpu.CompilerParams(
            dimension_semantics=("parallel",), vmem_limit_bytes=VMEM_LIMIT),
        name="merge",
    )(att, rnn, ga, gl, x, wa, wr, wo)


def _inproj_weight(w_in, d_rnn, d_model):
    splits = (D_ATT, HEAD_DIM, HEAD_DIM, IDX_HEADS * IDX_DIM, IDX_DIM,
              IDX_HEADS, d_rnn, d_rnn, d_model, d_model)
    parts, c = [], 0
    for n in splits:
        parts.append(w_in[..., c:c + n])
        c += n
    assert c == w_in.shape[-1]
    wq, wk, wv, wqi, wki, wwi, wxr, wgr, wga, wgl = parts
    wi_pad = jnp.zeros(w_in.shape[:-1] + (LANES - IDX_HEADS,), w_in.dtype)
    cols = [wxr, wgr, wq, wk, wv, wv, wk, wqi] + [wki] * (LANES // IDX_DIM)
    cols += [wwi, wi_pad, wga, wgl]
    return jnp.concatenate(cols, axis=-1).astype(BF16)


def _block_diag_tiles(w):
    depth, n_blocks, bw, _ = w.shape
    per = MXU_DIM // bw
    w = w.reshape(depth, n_blocks // per, per, bw, bw)
    eye = jnp.eye(per, dtype=w.dtype)
    t = w[:, :, :, :, None, :] * eye[None, None, :, None, :, None]
    return t.reshape(depth, n_blocks // per, MXU_DIM, MXU_DIM).astype(BF16)


def kernel(x, ffn1_norm, ffn1_wg, ffn1_wu, ffn1_wd, mix_norm, w_in, conv_w,
           conv_b, rg_wa, rg_ba, rg_wx, rg_bx, rg_lam, w_att_proj, w_rnn_proj,
           w_out, ffn2_norm, ffn2_wg, ffn2_wu, ffn2_wd, final_norm):
    batch, seq, d = x.shape
    depth = ffn1_norm.shape[0]
    d_rnn = conv_w.shape[-1]
    assert batch == SUBLANES and seq % Q_TILE == 0
    assert (batch * seq) % ROW_TILE == 0 and ROW_TILE % batch == 0

    def row(v):
        return v.reshape(1, -1)

    wg1, wu1, wd1 = (w.astype(BF16) for w in (ffn1_wg, ffn1_wu, ffn1_wd))
    wg2, wu2, wd2 = (w.astype(BF16) for w in (ffn2_wg, ffn2_wu, ffn2_wd))
    w_mix = _inproj_weight(w_in, d_rnn, d)
    wa_t, wx_t = _block_diag_tiles(rg_wa), _block_diag_tiles(rg_wx)
    w_ap, w_rp, w_o = (w.astype(BF16) for w in (w_att_proj, w_rnn_proj, w_out))

    h = x
    for l in range(depth):
        h = _ffn(h, row(ffn1_norm[l]), wg1, wu1, wd1, l, batch, first=(l == 0))
        q, kv, qi, ki, wi, ga, gl, rnn = _mixer_in(
            h, row(mix_norm[l]), w_mix, conv_w[l], row(conv_b[l]), wa_t,
            row(rg_ba[l]), wx_t, row(rg_bx[l]), row(rg_lam[l]), l, batch)
        att = _attention(q, qi, wi, kv, ki, batch)
        h = _merge(att, rnn, ga, gl, h, w_ap, w_rp, w_o, l, batch)
        last = l == depth - 1
        h = _ffn(h, row(ffn2_norm[l]), wg2, wu2, wd2, l, batch,
                 final_g=row(final_norm) if last else None)
    return h
```

```python
import functools

import jax
import jax.numpy as jnp
from jax import lax
from jax.experimental import pallas as pl
from jax.experimental.pallas import tpu as pltpu

F32 = jnp.float32
BF16 = jnp.bfloat16
I32 = jnp.int32

N_HEADS = 8
HEAD_DIM = 64
D_ATT = N_HEADS * HEAD_DIM
IDX_HEADS = 8
IDX_DIM = 32
TOPK_MAX = 256
RNN_BLOCKS = 16
CONV_W = 4
LRU_C = 8.0
EPS = 1e-6

LANES = 128
SUBLANES = 8
MXU_DIM = 256
VMEM_LIMIT = 56 * 1024 * 1024

INT_MIN = -2 ** 31
MASK_BIAS = -1e30

ROW_TILE = 512
Q_TILE = 256
K_CHUNK = 256


def _resident(shape, layer=None):
    zeros = (0,) * len(shape)
    if layer is None:
        return pl.BlockSpec(shape, lambda *_: zeros,
                            pipeline_mode=pl.Buffered(1))
    return pl.BlockSpec((None,) + tuple(shape), lambda *_: (layer,) + zeros,
                        pipeline_mode=pl.Buffered(1))


def _rmsnorm(x, g):
    ms = jnp.mean(x * x, axis=-1, keepdims=True)
    return x * lax.rsqrt(ms + EPS) * g


def _dot(a, b):
    return jnp.dot(a, b, preferred_element_type=F32)


def _dot_nt(a, b):
    return lax.dot_general(a, b, (((1,), (1,)), ((), ())),
                           preferred_element_type=F32)


def _ff_chunks(d_ff):
    chunks, c0 = [], 0
    while c0 < d_ff:
        c1 = min(c0 + 4 * MXU_DIM, d_ff)
        chunks.append((c0, c1))
        c0 = c1
    return chunks


def _slabs(width):
    return width // LANES


def _batch_to_rows(x_ref, slab_scr, batch):
    tt, d = x_ref.shape[1:]
    for b in range(batch):
        for j in range(_slabs(d)):
            slab_scr[j, pl.ds(b, tt, stride=batch), :] = (
                x_ref[b, :, j * LANES:(j + 1) * LANES])
    return jnp.concatenate([slab_scr[j] for j in range(_slabs(d))], axis=1)


def _rows_to_batch(o_ref, y, slab_scr, batch):
    tt, d = o_ref.shape[1:]
    for j in range(_slabs(d)):
        slab_scr[j] = y[:, j * LANES:(j + 1) * LANES]
    for b in range(batch):
        for j in range(_slabs(d)):
            o_ref[b, :, j * LANES:(j + 1) * LANES] = (
                slab_scr[j, pl.ds(b, tt, stride=batch), :])


def _rows_to_seq(o_ref, z, slab_scr, slab0, batch):
    tt = o_ref.shape[0]
    c = z.shape[1]
    for j in range(_slabs(c)):
        slab_scr[slab0 + j] = z[:, j * LANES:(j + 1) * LANES]
    for b in range(batch):
        for j in range(_slabs(c)):
            o_ref[:, b * c + j * LANES:b * c + (j + 1) * LANES] = (
                slab_scr[slab0 + j, pl.ds(b, tt, stride=batch), :]
                .astype(o_ref.dtype))


def _seq_to_rows(x_ref, slab_scr, batch):
    tt = x_ref.shape[0]
    c = x_ref.shape[1] // batch
    for b in range(batch):
        for j in range(_slabs(c)):
            slab_scr[j, pl.ds(b, tt, stride=batch), :] = (
                x_ref[:, b * c + j * LANES:b * c + (j + 1) * LANES].astype(F32))
    return jnp.concatenate([slab_scr[j] for j in range(_slabs(c))], axis=1)


def _ffn_kernel(x_ref, g_ref, wg_ref, wu_ref, wd_ref, *rest, first, final,
                batch):
    rest = list(rest)
    fg_ref = rest.pop(0) if final else None
    o_ref = rest.pop(0)
    slab_scr = rest.pop(0) if (first or final) else None
    x = _batch_to_rows(x_ref, slab_scr, batch) if first else x_ref[...]
    h = _rmsnorm(x, g_ref[...]).astype(BF16)
    acc = None
    for c0, c1 in _ff_chunks(wg_ref.shape[1]):
        g = _dot(h, wg_ref[:, c0:c1])
        u = _dot(h, wu_ref[:, c0:c1])
        a = (g * jax.nn.sigmoid(g) * u).astype(BF16)
        d = _dot(a, wd_ref[c0:c1, :])
        acc = d if acc is None else acc + d
    y = x + 0.5 * acc
    if final:
        _rows_to_batch(o_ref, _rmsnorm(y, fg_ref[...]), slab_scr, batch)
    else:
        o_ref[...] = y


def _ffn(x, g, wg, wu, wd, layer, batch, first=False, final_g=None):
    final = final_g is not None
    if first:
        _, seq, d = x.shape
        rows = seq * batch
    else:
        rows, d = x.shape
        seq = rows // batch
    d_ff = wg.shape[2]
    tt = ROW_TILE // batch
    row_spec = pl.BlockSpec((ROW_TILE, d), lambda i: (i, 0))
    batch_spec = pl.BlockSpec((batch, tt, d), lambda i: (0, i, 0))
    in_specs = [batch_spec if first else row_spec, _resident((1, d)),
                _resident((d, d_ff), layer), _resident((d, d_ff), layer),
                _resident((d_ff, d), layer)]
    args = [x, g, wg, wu, wd]
    if final:
        in_specs.append(_resident((1, d)))
        args.append(final_g)
    scratch = ([pltpu.VMEM((_slabs(d), ROW_TILE, LANES), F32)]
               if (first or final) else [])
    out_shape = (batch, seq, d) if final else (rows, d)
    return pl.pallas_call(
        functools.partial(_ffn_kernel, first=first, final=final, batch=batch),
        out_shape=jax.ShapeDtypeStruct(out_shape, F32),
        grid=(rows // ROW_TILE,),
        in_specs=in_specs,
        out_specs=batch_spec if final else row_spec,
        scratch_shapes=scratch,
        compiler_params=pltpu.CompilerParams(
            dimension_semantics=("parallel",), vmem_limit_bytes=VMEM_LIMIT),
        name="ffn_first" if first else ("ffn_final" if final else "ffn"),
    )(*args)


ATT_WIDTHS = (D_ATT, 4 * HEAD_DIM, IDX_HEADS * IDX_DIM, LANES, LANES)
ATT_DTYPES = (BF16, BF16, BF16, BF16, F32)


def _mixer_in_kernel(x_ref, g_ref, w_ref, cw_ref, cb_ref, wa_ref, ba_ref,
                     wx_ref, bx_ref, lam_ref,
                     q_ref, kv_ref, qi_ref, ki_ref, wi_ref, ga_ref, gl_ref,
                     rnn_ref, slab_scr, xs_scr, a_scr, u_scr, h_scr, hc_scr,
                     *, batch):
    tm = x_ref.shape[0]
    c = cw_ref.shape[1]
    d_model = ga_ref.shape[1]
    halo = (CONV_W - 1) * batch
    h = _rmsnorm(x_ref[...], g_ref[...]).astype(BF16)

    @pl.when(pl.program_id(0) == 0)
    def _():
        xs_scr[0:halo, :] = jnp.zeros((halo, c), F32)
        hc_scr[...] = jnp.zeros(hc_scr.shape, F32)

    scales = (HEAD_DIM ** -0.5, None, None, None, (IDX_HEADS * IDX_DIM) ** -0.5)
    att_refs = (q_ref, kv_ref, qi_ref, ki_ref, wi_ref)
    att_col0 = 2 * c
    gate_col0 = att_col0 + sum(ATT_WIDTHS)

    def att_operand(i):
        col = att_col0 + sum(ATT_WIDTHS[:i])
        z = _dot(h, w_ref[:, col:col + ATT_WIDTHS[i]])
        if scales[i] is not None:
            z = z * scales[i]
        _rows_to_seq(att_refs[i], z, slab_scr, _slabs(sum(ATT_WIDTHS[:i])),
                     batch)

    def gate_logits(o_ref):
        col = gate_col0 if o_ref is ga_ref else gate_col0 + d_model
        o_ref[...] = _dot(h, w_ref[:, col:col + d_model])

    fillers = (lambda: gate_logits(ga_ref),
               lambda: gate_logits(gl_ref),
               lambda: (att_operand(0), att_operand(1)),
               lambda: (att_operand(2), att_operand(3), att_operand(4)))

    n_tiles = c // MXU_DIM
    assert n_tiles == len(fillers)
    for n in range(n_tiles):
        cs = slice(n * MXU_DIM, (n + 1) * MXU_DIM)
        xr = _dot(h, w_ref[:, cs])
        gr = _dot(h, w_ref[:, c + n * MXU_DIM:c + (n + 1) * MXU_DIM])
        xs_scr[halo:halo + tm, cs] = xr
        xc = cb_ref[:, cs]
        for j in range(CONV_W):
            xc = xc + cw_ref[j:j + 1, cs] * xs_scr[j * batch:j * batch + tm, cs]
        xs_scr[0:halo, cs] = xr[tm - halo:tm, :]

        xb = xc.astype(BF16)
        r = jax.nn.sigmoid(_dot(xb, wa_ref[n]) + ba_ref[:, cs])
        gate_i = jax.nn.sigmoid(_dot(xb, wx_ref[n]) + bx_ref[:, cs])
        fillers[n]()
        neg_lam = -lam_ref[:, cs]
        softplus = (jnp.maximum(neg_lam, 0.0)
                    + jnp.log1p(jnp.exp(-jnp.abs(neg_lam))))
        log_a = -LRU_C * r * softplus
        a_scr[:, cs] = jnp.exp(log_a)
        th = jnp.tanh(log_a)
        u_scr[:, cs] = jnp.sqrt(-2.0 * th / (1.0 - th)) * gate_i * xc

        hcur = hc_scr[:, cs]
        for t in range(tm // batch):
            rs = slice(t * batch, (t + 1) * batch)
            hcur = a_scr[rs, cs] * hcur + u_scr[rs, cs]
            h_scr[rs, cs] = hcur
        hc_scr[:, cs] = hcur
        rnn_ref[:, cs] = (h_scr[:, cs] * jax.nn.gelu(gr)).astype(BF16)


def _mixer_in(x, g, w, cw, cb, wa, ba, wx, bx, lam, layer, batch):
    rows, d = x.shape
    c = cw.shape[1]
    tm = ROW_TILE
    tt = tm // batch
    seq = rows // batch
    halo = (CONV_W - 1) * batch
    assert w.shape[2] == 2 * c + sum(ATT_WIDTHS) + 2 * d

    def row_spec(n):
        return pl.BlockSpec((tm, n), lambda i: (i, 0))

    def seq_spec(n):
        return pl.BlockSpec((tt, batch * n), lambda i: (i, 0))

    out_shape = [jax.ShapeDtypeStruct((seq, batch * n), t)
                 for n, t in zip(ATT_WIDTHS, ATT_DTYPES)]
    out_shape += [jax.ShapeDtypeStruct((rows, d), F32)] * 2
    out_shape += [jax.ShapeDtypeStruct((rows, c), BF16)]
    out_specs = [seq_spec(n) for n in ATT_WIDTHS]
    out_specs += [row_spec(d), row_spec(d), row_spec(c)]
    return pl.pallas_call(
        functools.partial(_mixer_in_kernel, batch=batch),
        out_shape=out_shape,
        grid=(rows // tm,),
        in_specs=[row_spec(d), _resident((1, d)),
                  _resident(w.shape[1:], layer), _resident(cw.shape),
                  _resident((1, c)), _resident(wa.shape[1:], layer),
                  _resident((1, c)), _resident(wx.shape[1:], layer),
                  _resident((1, c)), _resident((1, c))],
        out_specs=out_specs,
        scratch_shapes=[
            pltpu.VMEM((_slabs(sum(ATT_WIDTHS)), tm, LANES), F32),
            pltpu.VMEM((halo + tm, c), F32),
            pltpu.VMEM((tm, c), F32), pltpu.VMEM((tm, c), F32),
            pltpu.VMEM((tm, c), F32), pltpu.VMEM((batch, c), F32)],
        compiler_params=pltpu.CompilerParams(
            dimension_semantics=("arbitrary",), vmem_limit_bytes=VMEM_LIMIT),
        name="mixer_in",
    )(x, g, w, cw, cb, wa, ba, wx, bx, lam)


VT_ROWS = HEAD_DIM + 16


def _key_to_float(key):
    bits = jnp.where(key < 0, INT_MIN - key, key)
    return pltpu.bitcast(bits, F32)


def _for_chunks(n, body):
    def pair(i, carry):
        body(2 * i)
        body(2 * i + 1)
        return carry

    lax.fori_loop(0, n // 2, pair, 0)

    @pl.when(n % 2 == 1)
    def _():
        body(n - 1)


def _attn_kernel(q_ref, qi_ref, wi_ref, kv_ref, ki_ref, o_ref,
                 qm_scr, qim_scr, wit_scr, vt_scr, sc_scr, sall_scr, m_scr,
                 acc_scr, mpart_scr, ties_scr, *, topk):
    tq, ck = Q_TILE, K_CHUNK
    n_kc = sc_scr.shape[0]
    seq = n_kc * ck
    qt = pl.program_id(1)
    n_chunks = (qt * tq) // ck + tq // ck
    lane = lax.broadcasted_iota(I32, (tq, LANES), 1)
    key_iota = lax.broadcasted_iota(I32, (ck, tq), 0)
    qry_iota = lax.broadcasted_iota(I32, (ck, tq), 1)
    kf = float(topk)

    for h in range(N_HEADS):
        qp = q_ref[:, (h // 2) * LANES:(h // 2 + 1) * LANES].astype(F32)
        lo = (h % 2) * HEAD_DIM
        qm_scr[h] = jnp.where((lane >= lo) & (lane < lo + HEAD_DIM), qp,
                              0.0).astype(BF16)
    for h in range(IDX_HEADS):
        qp = qi_ref[:, (h // 4) * LANES:(h // 4 + 1) * LANES].astype(F32)
        lo = (h % 4) * IDX_DIM
        qim_scr[h] = jnp.where((lane >= lo) & (lane < lo + IDX_DIM), qp,
                               0.0).astype(BF16)
    wit_scr[...] = jnp.transpose(wi_ref[...])

    @pl.when(qt == 0)
    def _():
        lane_c = lax.broadcasted_iota(I32, (ck, LANES), 1)
        for c in range(n_kc):
            v1 = jnp.where(lane_c < HEAD_DIM,
                           kv_ref[c * ck:(c + 1) * ck, LANES:2 * LANES].astype(F32),
                           1.0)
            vt_scr[c] = jnp.transpose(v1)[0:VT_ROWS, :].astype(BF16)

    def score_chunk(kc):
        k0 = pl.multiple_of(kc * ck, ck)
        kic = ki_ref[pl.ds(k0, ck), :]
        acc = jnp.zeros((ck, tq), F32)
        for h in range(IDX_HEADS):
            acc = acc + (jnp.maximum(_dot_nt(kic, qim_scr[h]), 0.0)
                         * wit_scr[h:h + 1, :])
        causal = key_iota - qry_iota <= qt * tq - kc * ck
        sc_scr[kc] = jnp.where(causal, acc, -jnp.inf)

    _for_chunks(n_chunks, score_chunk)

    def count(pred):
        acc_rows = 2 * SUBLANES

        def chunk(kc, c):
            hit = jnp.where(pred(sc_scr[kc], kc), 1.0, 0.0)
            return c + jnp.sum(hit.reshape(ck // acc_rows, acc_rows, tq), axis=0)

        c = lax.fori_loop(0, n_chunks, chunk, jnp.zeros((acc_rows, tq), F32))
        return jnp.sum(c, axis=0, keepdims=True)

    def bit_step(i, thr):
        cand = thr + lax.shift_left(jnp.int32(1), 31 - i)
        cand_f = _key_to_float(cand)
        tot = count(lambda s, kc: s >= cand_f)
        return jnp.where(tot >= kf, cand, thr)

    thr = lax.fori_loop(0, 32, bit_step, jnp.full((1, tq), INT_MIN, I32))

    thr_f = _key_to_float(thr)
    short = jnp.logical_not(thr_f > -jnp.inf)
    thr_f = jnp.where(short, -jnp.inf, thr_f)
    n_gt = count(lambda s, kc: s > thr_f)
    need = jnp.where(short, 0.0, kf - n_gt)
    lower_tri = jnp.where(
        lax.broadcasted_iota(I32, (ck, ck), 0)
        >= lax.broadcasted_iota(I32, (ck, ck), 1), 1.0, 0.0).astype(BF16)

    mpart_scr[...] = jnp.full(mpart_scr.shape, MASK_BIAS, F32)
    acc_scr[...] = jnp.zeros(acc_scr.shape, F32)

    ties_scr[...] = jnp.zeros(ties_scr.shape, F32)

    def scores_chunk(kc):
        k0 = pl.multiple_of(kc * ck, ck)
        s_idx = sc_scr[kc]
        tie = s_idx == thr_f
        tie_f = jnp.where(tie, 1.0, 0.0)
        ties_before = ties_scr[...]
        ties_scr[...] = ties_before + jnp.sum(tie_f, axis=0, keepdims=True)
        rank = ties_before + _dot(lower_tri, tie_f.astype(BF16))
        keep = (s_idx > thr_f) | (tie & (rank <= need))
        bias = jnp.where(keep, 0.0, MASK_BIAS)
        for h in range(N_HEADS):
            c0 = 0 if h % 2 == 0 else LANES
            k_mat = kv_ref[pl.ds(k0, ck), c0:c0 + LANES]
            s = _dot_nt(k_mat, qm_scr[h]) + bias
            sall_scr[kc, h] = s
            mpart_scr[h] = jnp.maximum(
                mpart_scr[h],
                jnp.max(s.reshape(ck // SUBLANES, SUBLANES, tq), axis=0))

    _for_chunks(n_chunks, scores_chunk)
    for h in range(N_HEADS):
        m_scr[h:h + 1, :] = jnp.max(mpart_scr[h], axis=0, keepdims=True)

    def value_chunk(kc):
        vt_c = vt_scr[kc]
        for h in range(N_HEADS):
            p = jnp.exp(sall_scr[kc, h] - m_scr[h:h + 1, :]).astype(BF16)
            acc_scr[h] = acc_scr[h] + _dot(vt_c, p)

    _for_chunks(n_chunks, value_chunk)

    outs = []
    for h in range(N_HEADS):
        a = acc_scr[h]
        outs.append(a[0:HEAD_DIM, :] / a[HEAD_DIM:HEAD_DIM + 1, :])
    o_ref[...] = jnp.transpose(jnp.concatenate(outs, axis=0)).astype(BF16)


def _attention(q, qi, wi, kv, ki, batch):
    tq, ck = Q_TILE, K_CHUNK
    seq = q.shape[0]
    topk = min(TOPK_MAX, seq // 4)
    nq = seq // tq

    def q_spec(c):
        return pl.BlockSpec((tq, c), lambda b, i: (i, b))

    def kv_spec(c):
        return pl.BlockSpec((seq, c), lambda b, i: (0, b))

    return pl.pallas_call(
        functools.partial(_attn_kernel, topk=topk),
        out_shape=jax.ShapeDtypeStruct((seq, batch * D_ATT), BF16),
        grid=(batch, nq),
        in_specs=[q_spec(D_ATT), q_spec(IDX_HEADS * IDX_DIM), q_spec(LANES),
                  kv_spec(4 * HEAD_DIM), kv_spec(LANES)],
        out_specs=q_spec(D_ATT),
        scratch_shapes=[
            pltpu.VMEM((N_HEADS, tq, LANES), BF16),
            pltpu.VMEM((IDX_HEADS, tq, LANES), BF16),
            pltpu.VMEM((LANES, tq), F32),
            pltpu.VMEM((seq // ck, VT_ROWS, ck), BF16),
            pltpu.VMEM((seq // ck, ck, tq), F32),
            pltpu.VMEM((seq // ck, N_HEADS, ck, tq), F32),
            pltpu.VMEM((N_HEADS, tq), F32),
            pltpu.VMEM((N_HEADS, VT_ROWS, tq), F32),
            pltpu.VMEM((N_HEADS, SUBLANES, tq), F32),
            pltpu.VMEM((1, tq), F32),
        ],
        compiler_params=pltpu.CompilerParams(
            dimension_semantics=("parallel", "arbitrary"),
            vmem_limit_bytes=VMEM_LIMIT),
        name="dsa_attention",
    )(q, qi, wi, kv, ki)


def _merge_kernel(att_ref, rnn_ref, ga_ref, gl_ref, x_ref, wa_ref, wr_ref,
                  wo_ref, o_ref, slab_scr, *, batch):
    att = _seq_to_rows(att_ref, slab_scr, batch).astype(BF16)
    pa = _dot(att, wa_ref[...])
    pr = _dot(rnn_ref[...], wr_ref[...])
    merged = (jax.nn.sigmoid(ga_ref[...]) * pa
              + jax.nn.sigmoid(gl_ref[...]) * pr)
    o_ref[...] = x_ref[...] + _dot(merged.astype(BF16), wo_ref[...])


def _merge(att, rnn, ga, gl, x, wa, wr, wo, layer, batch):
    rows, d = x.shape
    tt = ROW_TILE // batch

    def row_spec(c):
        return pl.BlockSpec((ROW_TILE, c), lambda i: (i, 0))

    return pl.pallas_call(
        functools.partial(_merge_kernel, batch=batch),
        out_shape=jax.ShapeDtypeStruct((rows, d), F32),
        grid=(rows // ROW_TILE,),
        in_specs=[pl.BlockSpec((tt, att.shape[1]), lambda i: (i, 0)),
                  row_spec(rnn.shape[1]), row_spec(d), row_spec(d),
                  row_spec(d), _resident(wa.shape[1:], layer),
                  _resident(wr.shape[1:], layer),
                  _resident(wo.shape[1:], layer)],
        out_specs=row_spec(d),
        scratch_shapes=[pltpu.VMEM((_slabs(D_ATT), ROW_TILE, LANES), F32)],
        compiler_params=pltpu.CompilerParams(
            dimension_semantics=("parallel",), vmem_limit_bytes=VMEM_LIMIT),
        name="merge",
    )(att, rnn, ga, gl, x, wa, wr, wo)


def _inproj_weight(w_in, d_rnn, d_model):
    splits = (D_ATT, HEAD_DIM, HEAD_DIM, IDX_HEADS * IDX_DIM, IDX_DIM,
              IDX_HEADS, d_rnn, d_rnn, d_model, d_model)
    parts, c = [], 0
    for n in splits:
        parts.append(w_in[..., c:c + n])
        c += n
    assert c == w_in.shape[-1]
    wq, wk, wv, wqi, wki, wwi, wxr, wgr, wga, wgl = parts
    wi_pad = jnp.zeros(w_in.shape[:-1] + (LANES - IDX_HEADS,), w_in.dtype)
    cols = [wxr, wgr, wq, wk, wv, wv, wk, wqi] + [wki] * (LANES // IDX_DIM)
    cols += [wwi, wi_pad, wga, wgl]
    return jnp.concatenate(cols, axis=-1).astype(BF16)


def _block_diag_tiles(w):
    depth, n_blocks, bw, _ = w.shape
    per = MXU_DIM // bw
    w = w.reshape(depth, n_blocks // per, per, bw, bw)
    eye = jnp.eye(per, dtype=w.dtype)
    t = w[:, :, :, :, None, :] * eye[None, None, :, None, :, None]
    return t.reshape(depth, n_blocks // per, MXU_DIM, MXU_DIM).astype(BF16)


def kernel(x, ffn1_norm, ffn1_wg, ffn1_wu, ffn1_wd, mix_norm, w_in, conv_w,
           conv_b, rg_wa, rg_ba, rg_wx, rg_bx, rg_lam, w_att_proj, w_rnn_proj,
           w_out, ffn2_norm, ffn2_wg, ffn2_wu, ffn2_wd, final_norm):
    batch, seq, d = x.shape
    depth = ffn1_norm.shape[0]
    d_rnn = conv_w.shape[-1]
    assert batch == SUBLANES and seq % Q_TILE == 0
    assert (batch * seq) % ROW_TILE == 0 and ROW_TILE % batch == 0

    def row(v):
        return v.reshape(1, -1)

    wg1, wu1, wd1 = (w.astype(BF16) for w in (ffn1_wg, ffn1_wu, ffn1_wd))
    wg2, wu2, wd2 = (w.astype(BF16) for w in (ffn2_wg, ffn2_wu, ffn2_wd))
    w_mix = _inproj_weight(w_in, d_rnn, d)
    wa_t, wx_t = _block_diag_tiles(rg_wa), _block_diag_tiles(rg_wx)
    w_ap, w_rp, w_o = (w.astype(BF16) for w in (w_att_proj, w_rnn_proj, w_out))

    h = x
    for l in range(depth):
        h = _ffn(h, row(ffn1_norm[l]), wg1, wu1, wd1, l, batch, first=(l == 0))
        q, kv, qi, ki, wi, ga, gl, rnn = _mixer_in(
            h, row(mix_norm[l]), w_mix, conv_w[l], row(conv_b[l]), wa_t,
            row(rg_ba[l]), wx_t, row(rg_bx[l]), row(rg_lam[l]), l, batch)
        att = _attention(q, qi, wi, kv, ki, batch)
        h = _merge(att, rnn, ga, gl, h, w_ap, w_rp, w_o, l, batch)
        last = l == depth - 1
        h = _ffn(h, row(ffn2_norm[l]), wg2, wu2, wd2, l, batch,
                 final_g=row(final_norm) if last else None)
    return h
```

```python
import functools

import jax
import jax.numpy as jnp
from jax import lax
from jax.experimental import pallas as pl
from jax.experimental.pallas import tpu as pltpu

F32 = jnp.float32
BF16 = jnp.bfloat16
I32 = jnp.int32

N_HEADS = 8
HEAD_DIM = 64
D_ATT = N_HEADS * HEAD_DIM
IDX_HEADS = 8
IDX_DIM = 32
TOPK_MAX = 256
RNN_BLOCKS = 16
CONV_W = 4
LRU_C = 8.0
EPS = 1e-6

LANES = 128
SUBLANES = 8
BF16_SUBLANES = 16
MXU_DIM = 256
VMEM_LIMIT = 56 * 1024 * 1024

INT_MIN = -2 ** 31
MASK_BIAS = -1e30

ROW_TILE = 512
Q_TILE = 256
K_CHUNK = 256


def _resident(shape, layer=None):
    zeros = (0,) * len(shape)
    if layer is None:
        return pl.BlockSpec(shape, lambda *_: zeros,
                            pipeline_mode=pl.Buffered(1))
    return pl.BlockSpec((None,) + tuple(shape), lambda *_: (layer,) + zeros,
                        pipeline_mode=pl.Buffered(1))


def _rmsnorm(x, g):
    ms = jnp.mean(x * x, axis=-1, keepdims=True)
    return x * lax.rsqrt(ms + EPS) * g


def _dot(a, b):
    return jnp.dot(a, b, preferred_element_type=F32)


def _dot_nt(a, b):
    return lax.dot_general(a, b, (((1,), (1,)), ((), ())),
                           preferred_element_type=F32)


def _ff_chunks(d_ff):
    chunks, c0 = [], 0
    while c0 < d_ff:
        c1 = min(c0 + 4 * MXU_DIM, d_ff)
        chunks.append((c0, c1))
        c0 = c1
    return chunks


def _slabs(width):
    return width // LANES


def _batch_to_rows(x_ref, slab_scr, batch):
    tt, d = x_ref.shape[1:]
    for b in range(batch):
        for j in range(_slabs(d)):
            slab_scr[j, pl.ds(b, tt, stride=batch), :] = (
                x_ref[b, :, j * LANES:(j + 1) * LANES])
    return jnp.concatenate([slab_scr[j] for j in range(_slabs(d))], axis=1)


def _rows_to_batch(o_ref, y, slab_scr, batch):
    tt, d = o_ref.shape[1:]
    for j in range(_slabs(d)):
        slab_scr[j] = y[:, j * LANES:(j + 1) * LANES]
    for b in range(batch):
        for j in range(_slabs(d)):
            o_ref[b, :, j * LANES:(j + 1) * LANES] = (
                slab_scr[j, pl.ds(b, tt, stride=batch), :])


def _rows_to_seq(o_ref, z, slab_scr, slab0, batch):
    tt = o_ref.shape[0]
    c = z.shape[1]
    for j in range(_slabs(c)):
        slab_scr[slab0 + j] = z[:, j * LANES:(j + 1) * LANES]
    for b in range(batch):
        for j in range(_slabs(c)):
            o_ref[:, b * c + j * LANES:b * c + (j + 1) * LANES] = (
                slab_scr[slab0 + j, pl.ds(b, tt, stride=batch), :]
                .astype(o_ref.dtype))


def _seq_to_rows(x_ref, slab_scr, batch):
    tt = x_ref.shape[0]
    c = x_ref.shape[1] // batch
    for b in range(batch):
        for j in range(_slabs(c)):
            slab_scr[j, pl.ds(b, tt, stride=batch), :] = (
                x_ref[:, b * c + j * LANES:b * c + (j + 1) * LANES].astype(F32))
    return jnp.concatenate([slab_scr[j] for j in range(_slabs(c))], axis=1)


def _ffn_kernel(x_ref, g_ref, wg_ref, wu_ref, wd_ref, *rest, first, final,
                batch):
    rest = list(rest)
    fg_ref = rest.pop(0) if final else None
    o_ref = rest.pop(0)
    slab_scr = rest.pop(0) if (first or final) else None
    x = _batch_to_rows(x_ref, slab_scr, batch) if first else x_ref[...]
    h = _rmsnorm(x, g_ref[...]).astype(BF16)
    acc = None
    for c0, c1 in _ff_chunks(wg_ref.shape[1]):
        g = _dot(h, wg_ref[:, c0:c1])
        u = _dot(h, wu_ref[:, c0:c1])
        a = (g * jax.nn.sigmoid(g) * u).astype(BF16)
        d = _dot(a, wd_ref[c0:c1, :])
        acc = d if acc is None else acc + d
    y = x + 0.5 * acc
    if final:
        _rows_to_batch(o_ref, _rmsnorm(y, fg_ref[...]), slab_scr, batch)
    else:
        o_ref[...] = y


def _ffn(x, g, wg, wu, wd, layer, batch, first=False, final_g=None):
    final = final_g is not None
    if first:
        _, seq, d = x.shape
        rows = seq * batch
    else:
        rows, d = x.shape
        seq = rows // batch
    d_ff = wg.shape[2]
    tt = ROW_TILE // batch
    row_spec = pl.BlockSpec((ROW_TILE, d), lambda i: (i, 0))
    batch_spec = pl.BlockSpec((batch, tt, d), lambda i: (0, i, 0))
    in_specs = [batch_spec if first else row_spec, _resident((1, d)),
                _resident((d, d_ff), layer), _resident((d, d_ff), layer),
                _resident((d_ff, d), layer)]
    args = [x, g, wg, wu, wd]
    if final:
        in_specs.append(_resident((1, d)))
        args.append(final_g)
    scratch = ([pltpu.VMEM((_slabs(d), ROW_TILE, LANES), F32)]
               if (first or final) else [])
    out_shape = (batch, seq, d) if final else (rows, d)
    return pl.pallas_call(
        functools.partial(_ffn_kernel, first=first, final=final, batch=batch),
        out_shape=jax.ShapeDtypeStruct(out_shape, F32),
        grid=(rows // ROW_TILE,),
        in_specs=in_specs,
        out_specs=batch_spec if final else row_spec,
        scratch_shapes=scratch,
        compiler_params=pltpu.CompilerParams(
            dimension_semantics=("parallel",), vmem_limit_bytes=VMEM_LIMIT),
        name="ffn_first" if first else ("ffn_final" if final else "ffn"),
    )(*args)


ATT_WIDTHS = (D_ATT, 4 * HEAD_DIM, IDX_HEADS * IDX_DIM, LANES, LANES)
ATT_DTYPES = (BF16, BF16, BF16, BF16, F32)


def _mixer_in_kernel(x_ref, g_ref, w_ref, cw_ref, cb_ref, wa_ref, ba_ref,
                     wx_ref, bx_ref, lam_ref,
                     q_ref, kv_ref, qi_ref, ki_ref, wi_ref, ga_ref, gl_ref,
                     rnn_ref, slab_scr, xs_scr, a_scr, u_scr, h_scr, hc_scr,
                     *, batch):
    tm = x_ref.shape[0]
    c = cw_ref.shape[1]
    d_model = ga_ref.shape[1]
    halo = (CONV_W - 1) * batch
    h = _rmsnorm(x_ref[...], g_ref[...]).astype(BF16)

    @pl.when(pl.program_id(0) == 0)
    def _():
        xs_scr[0:halo, :] = jnp.zeros((halo, c), F32)
        hc_scr[...] = jnp.zeros(hc_scr.shape, F32)

    scales = (HEAD_DIM ** -0.5, None, None, None, (IDX_HEADS * IDX_DIM) ** -0.5)
    att_refs = (q_ref, kv_ref, qi_ref, ki_ref, wi_ref)
    att_col0 = 2 * c
    gate_col0 = att_col0 + sum(ATT_WIDTHS)

    def att_operand(i):
        col = att_col0 + sum(ATT_WIDTHS[:i])
        z = _dot(h, w_ref[:, col:col + ATT_WIDTHS[i]])
        if scales[i] is not None:
            z = z * scales[i]
        _rows_to_seq(att_refs[i], z, slab_scr, _slabs(sum(ATT_WIDTHS[:i])),
                     batch)

    def gate_logits(o_ref):
        col = gate_col0 if o_ref is ga_ref else gate_col0 + d_model
        o_ref[...] = _dot(h, w_ref[:, col:col + d_model])

    fillers = (lambda: gate_logits(ga_ref),
               lambda: gate_logits(gl_ref),
               lambda: (att_operand(0), att_operand(1)),
               lambda: (att_operand(2), att_operand(3), att_operand(4)))

    n_tiles = c // MXU_DIM
    assert n_tiles == len(fillers)
    for n in range(n_tiles):
        cs = slice(n * MXU_DIM, (n + 1) * MXU_DIM)
        xr = _dot(h, w_ref[:, cs])
        gr = _dot(h, w_ref[:, c + n * MXU_DIM:c + (n + 1) * MXU_DIM])
        xs_scr[halo:halo + tm, cs] = xr
        xc = cb_ref[:, cs]
        for j in range(CONV_W):
            xc = xc + cw_ref[j:j + 1, cs] * xs_scr[j * batch:j * batch + tm, cs]
        xs_scr[0:halo, cs] = xr[tm - halo:tm, :]

        xb = xc.astype(BF16)
        r = jax.nn.sigmoid(_dot(xb, wa_ref[n]) + ba_ref[:, cs])
        gate_i = jax.nn.sigmoid(_dot(xb, wx_ref[n]) + bx_ref[:, cs])
        fillers[n]()
        neg_lam = -lam_ref[:, cs]
        softplus = (jnp.maximum(neg_lam, 0.0)
                    + jnp.log1p(jnp.exp(-jnp.abs(neg_lam))))
        log_a = -LRU_C * r * softplus
        a_scr[:, cs] = jnp.exp(log_a)
        th = jnp.tanh(log_a)
        u_scr[:, cs] = jnp.sqrt(-2.0 * th / (1.0 - th)) * gate_i * xc

        hcur = hc_scr[:, cs]
        for t in range(tm // batch):
            rs = slice(t * batch, (t + 1) * batch)
            hcur = a_scr[rs, cs] * hcur + u_scr[rs, cs]
            h_scr[rs, cs] = hcur
        hc_scr[:, cs] = hcur
        rnn_ref[:, cs] = (h_scr[:, cs] * jax.nn.gelu(gr)).astype(BF16)


def _mixer_in(x, g, w, cw, cb, wa, ba, wx, bx, lam, layer, batch):
    rows, d = x.shape
    c = cw.shape[1]
    tm = ROW_TILE
    tt = tm // batch
    seq = rows // batch
    halo = (CONV_W - 1) * batch
    assert w.shape[2] == 2 * c + sum(ATT_WIDTHS) + 2 * d

    def row_spec(n):
        return pl.BlockSpec((tm, n), lambda i: (i, 0))

    def seq_spec(n):
        return pl.BlockSpec((tt, batch * n), lambda i: (i, 0))

    out_shape = [jax.ShapeDtypeStruct((seq, batch * n), t)
                 for n, t in zip(ATT_WIDTHS, ATT_DTYPES)]
    out_shape += [jax.ShapeDtypeStruct((rows, d), F32)] * 2
    out_shape += [jax.ShapeDtypeStruct((rows, c), BF16)]
    out_specs = [seq_spec(n) for n in ATT_WIDTHS]
    out_specs += [row_spec(d), row_spec(d), row_spec(c)]
    return pl.pallas_call(
        functools.partial(_mixer_in_kernel, batch=batch),
        out_shape=out_shape,
        grid=(rows // tm,),
        in_specs=[row_spec(d), _resident((1, d)),
                  _resident(w.shape[1:], layer), _resident(cw.shape),
                  _resident((1, c)), _resident(wa.shape[1:], layer),
                  _resident((1, c)), _resident(wx.shape[1:], layer),
                  _resident((1, c)), _resident((1, c))],
        out_specs=out_specs,
        scratch_shapes=[
            pltpu.VMEM((_slabs(sum(ATT_WIDTHS)), tm, LANES), F32),
            pltpu.VMEM((halo + tm, c), F32),
            pltpu.VMEM((tm, c), F32), pltpu.VMEM((tm, c), F32),
            pltpu.VMEM((tm, c), F32), pltpu.VMEM((batch, c), F32)],
        compiler_params=pltpu.CompilerParams(
            dimension_semantics=("arbitrary",), vmem_limit_bytes=VMEM_LIMIT),
        name="mixer_in",
    )(x, g, w, cw, cb, wa, ba, wx, bx, lam)


VT_ROWS = HEAD_DIM + 16


def _key_to_float(key):
    bits = jnp.where(key < 0, INT_MIN - key, key)
    return pltpu.bitcast(bits, F32)


def _for_chunks(n, body):
    def pair(i, carry):
        body(2 * i)
        body(2 * i + 1)
        return carry

    lax.fori_loop(0, n // 2, pair, 0)

    @pl.when(n % 2 == 1)
    def _():
        body(n - 1)


def _attn_kernel(q_ref, qi_ref, wi_ref, kv_ref, ki_ref, o_ref,
                 qm_scr, qim_scr, wit_scr, vt_scr, sc_scr, scb_scr, sall_scr, m_scr,
                 acc_scr, mpart_scr, ties_scr, *, topk):
    tq, ck = Q_TILE, K_CHUNK
    n_kc = sc_scr.shape[0]
    seq = n_kc * ck
    qt = pl.program_id(1)
    n_chunks = (qt * tq) // ck + tq // ck
    lane = lax.broadcasted_iota(I32, (tq, LANES), 1)
    key_iota = lax.broadcasted_iota(I32, (ck, tq), 0)
    qry_iota = lax.broadcasted_iota(I32, (ck, tq), 1)
    kf = float(topk)

    for h in range(N_HEADS):
        qp = q_ref[:, (h // 2) * LANES:(h // 2 + 1) * LANES].astype(F32)
        lo = (h % 2) * HEAD_DIM
        qm_scr[h] = jnp.where((lane >= lo) & (lane < lo + HEAD_DIM), qp,
                              0.0).astype(BF16)
    for h in range(IDX_HEADS):
        qp = qi_ref[:, (h // 4) * LANES:(h // 4 + 1) * LANES].astype(F32)
        lo = (h % 4) * IDX_DIM
        qim_scr[h] = jnp.where((lane >= lo) & (lane < lo + IDX_DIM), qp,
                               0.0).astype(BF16)
    wit_scr[...] = jnp.transpose(wi_ref[...])

    @pl.when(qt == 0)
    def _():
        lane_c = lax.broadcasted_iota(I32, (ck, LANES), 1)
        for c in range(n_kc):
            v1 = jnp.where(lane_c < HEAD_DIM,
                           kv_ref[c * ck:(c + 1) * ck, LANES:2 * LANES].astype(F32),
                           1.0)
            vt_scr[c] = jnp.transpose(v1)[0:VT_ROWS, :].astype(BF16)

    def score_chunk(kc):
        k0 = pl.multiple_of(kc * ck, ck)
        kic = ki_ref[pl.ds(k0, ck), :]
        acc = jnp.zeros((ck, tq), F32)
        for h in range(IDX_HEADS):
            acc = acc + (jnp.maximum(_dot_nt(kic, qim_scr[h]), 0.0)
                         * wit_scr[h:h + 1, :])
        causal = key_iota - qry_iota <= qt * tq - kc * ck
        sc = jnp.where(causal, acc, -jnp.inf)
        sc_scr[kc] = sc
        scb_scr[kc] = sc.astype(BF16)

    _for_chunks(n_chunks, score_chunk)

    def count(pred):
        acc_rows = 2 * SUBLANES

        def chunk(kc, c):
            hit = jnp.where(pred(sc_scr[kc], kc), 1.0, 0.0)
            return c + jnp.sum(hit.reshape(ck // acc_rows, acc_rows, tq), axis=0)

        c = lax.fori_loop(0, n_chunks, chunk, jnp.zeros((acc_rows, tq), F32))
        return jnp.sum(c, axis=0, keepdims=True)

    def count_bf16(cand):
        acc_rows = 2 * BF16_SUBLANES
        one, zero = jnp.ones((), BF16), jnp.zeros((), BF16)

        def chunk(kc, c):
            hit = jnp.where(scb_scr[kc] >= cand, one, zero)
            part = hit[0:acc_rows]
            for r0 in range(acc_rows, ck, acc_rows):
                part = part + hit[r0:r0 + acc_rows]
            return c + part.astype(F32)

        c = lax.fori_loop(0, n_chunks, chunk, jnp.zeros((acc_rows, tq), F32))
        return jnp.sum(c, axis=0, keepdims=True)

    def coarse_step(i, thr16):
        cand16 = thr16 + lax.shift_left(jnp.int32(1), 15 - i)
        cand_f = _key_to_float(cand16 * 65536)
        tot = count_bf16(cand_f.astype(BF16))
        return jnp.where(tot >= kf, cand16, thr16)

    thr16 = lax.fori_loop(0, 16, coarse_step,
                          jnp.full((1, tq), -2 ** 15, I32))

    def fine_step(i, thr):
        cand = thr + lax.shift_left(jnp.int32(1), 16 - i)
        cand_f = _key_to_float(cand)
        tot = count(lambda s, kc: s >= cand_f)
        return jnp.where(tot >= kf, cand, thr)

    thr = lax.fori_loop(0, 17, fine_step, (thr16 - 1) * 65536)

    thr_f = _key_to_float(thr)
    short = jnp.logical_not(thr_f > -jnp.inf)
    thr_f = jnp.where(short, -jnp.inf, thr_f)
    n_gt = count(lambda s, kc: s > thr_f)
    need = jnp.where(short, 0.0, kf - n_gt)
    lower_tri = jnp.where(
        lax.broadcasted_iota(I32, (ck, ck), 0)
        >= lax.broadcasted_iota(I32, (ck, ck), 1), 1.0, 0.0).astype(BF16)

    mpart_scr[...] = jnp.full(mpart_scr.shape, MASK_BIAS, F32)
    acc_scr[...] = jnp.zeros(acc_scr.shape, F32)

    ties_scr[...] = jnp.zeros(ties_scr.shape, F32)

    def scores_chunk(kc):
        k0 = pl.multiple_of(kc * ck, ck)
        s_idx = sc_scr[kc]
        tie = s_idx == thr_f
        tie_f = jnp.where(tie, 1.0, 0.0)
        ties_before = ties_scr[...]
        ties_scr[...] = ties_before + jnp.sum(tie_f, axis=0, keepdims=True)
        rank = ties_before + _dot(lower_tri, tie_f.astype(BF16))
        keep = (s_idx > thr_f) | (tie & (rank <= need))
        bias = jnp.where(keep, 0.0, MASK_BIAS)
        for h in range(N_HEADS):
            c0 = 0 if h % 2 == 0 else LANES
            k_mat = kv_ref[pl.ds(k0, ck), c0:c0 + LANES]
            s = _dot_nt(k_mat, qm_scr[h]) + bias
            sall_scr[kc, h] = s
            mpart_scr[h] = jnp.maximum(
                mpart_scr[h],
                jnp.max(s.reshape(ck // SUBLANES, SUBLANES, tq), axis=0))

    _for_chunks(n_chunks, scores_chunk)
    for h in range(N_HEADS):
        m_scr[h:h + 1, :] = jnp.max(mpart_scr[h], axis=0, keepdims=True)

    def value_chunk(kc):
        vt_c = vt_scr[kc]
        for h in range(N_HEADS):
            p = jnp.exp(sall_scr[kc, h] - m_scr[h:h + 1, :]).astype(BF16)
            acc_scr[h] = acc_scr[h] + _dot(vt_c, p)

    _for_chunks(n_chunks, value_chunk)

    outs = []
    for h in range(N_HEADS):
        a = acc_scr[h]
        outs.append(a[0:HEAD_DIM, :] / a[HEAD_DIM:HEAD_DIM + 1, :])
    o_ref[...] = jnp.transpose(jnp.concatenate(outs, axis=0)).astype(BF16)


def _attention(q, qi, wi, kv, ki, batch):
    tq, ck = Q_TILE, K_CHUNK
    seq = q.shape[0]
    topk = min(TOPK_MAX, seq // 4)
    nq = seq // tq

    def q_spec(c):
        return pl.BlockSpec((tq, c), lambda b, i: (i, b))

    def kv_spec(c):
        return pl.BlockSpec((seq, c), lambda b, i: (0, b))

    return pl.pallas_call(
        functools.partial(_attn_kernel, topk=topk),
        out_shape=jax.ShapeDtypeStruct((seq, batch * D_ATT), BF16),
        grid=(batch, nq),
        in_specs=[q_spec(D_ATT), q_spec(IDX_HEADS * IDX_DIM), q_spec(LANES),
                  kv_spec(4 * HEAD_DIM), kv_spec(LANES)],
        out_specs=q_spec(D_ATT),
        scratch_shapes=[
            pltpu.VMEM((N_HEADS, tq, LANES), BF16),
            pltpu.VMEM((IDX_HEADS, tq, LANES), BF16),
            pltpu.VMEM((LANES, tq), F32),
            pltpu.VMEM((seq // ck, VT_ROWS, ck), BF16),
            pltpu.VMEM((seq // ck, ck, tq), F32),
            pltpu.VMEM((seq // ck, ck, tq), BF16),
            pltpu.VMEM((seq // ck, N_HEADS, ck, tq), F32),
            pltpu.VMEM((N_HEADS, tq), F32),
            pltpu.VMEM((N_HEADS, VT_ROWS, tq), F32),
            pltpu.VMEM((N_HEADS, SUBLANES, tq), F32),
            pltpu.VMEM((1, tq), F32),
        ],
        compiler_params=pltpu.CompilerParams(
            dimension_semantics=("parallel", "arbitrary"),
            vmem_limit_bytes=VMEM_LIMIT),
        name="dsa_attention",
    )(q, qi, wi, kv, ki)


def _merge_kernel(att_ref, rnn_ref, ga_ref, gl_ref, x_ref, wa_ref, wr_ref,
                  wo_ref, o_ref, slab_scr, *, batch):
    att = _seq_to_rows(att_ref, slab_scr, batch).astype(BF16)
    pa = _dot(att, wa_ref[...])
    pr = _dot(rnn_ref[...], wr_ref[...])
    merged = (jax.nn.sigmoid(ga_ref[...]) * pa
              + jax.nn.sigmoid(gl_ref[...]) * pr)
    o_ref[...] = x_ref[...] + _dot(merged.astype(BF16), wo_ref[...])


def _merge(att, rnn, ga, gl, x, wa, wr, wo, layer, batch):
    rows, d = x.shape
    tt = ROW_TILE // batch

    def row_spec(c):
        return pl.BlockSpec((ROW_TILE, c), lambda i: (i, 0))

    return pl.pallas_call(
        functools.partial(_merge_kernel, batch=batch),
        out_shape=jax.ShapeDtypeStruct((rows, d), F32),
        grid=(rows // ROW_TILE,),
        in_specs=[pl.BlockSpec((tt, att.shape[1]), lambda i: (i, 0)),
                  row_spec(rnn.shape[1]), row_spec(d), row_spec(d),
                  row_spec(d), _resident(wa.shape[1:], layer),
                  _resident(wr.shape[1:], layer),
                  _resident(wo.shape[1:], layer)],
        out_specs=row_spec(d),
        scratch_shapes=[pltpu.VMEM((_slabs(D_ATT), ROW_TILE, LANES), F32)],
        compiler_params=pltpu.CompilerParams(
            dimension_semantics=("parallel",), vmem_limit_bytes=VMEM_LIMIT),
        name="merge",
    )(att, rnn, ga, gl, x, wa, wr, wo)


def _inproj_weight(w_in, d_rnn, d_model):
    splits = (D_ATT, HEAD_DIM, HEAD_DIM, IDX_HEADS * IDX_DIM, IDX_DIM,
              IDX_HEADS, d_rnn, d_rnn, d_model, d_model)
    parts, c = [], 0
    for n in splits:
        parts.append(w_in[..., c:c + n])
        c += n
    assert c == w_in.shape[-1]
    wq, wk, wv, wqi, wki, wwi, wxr, wgr, wga, wgl = parts
    wi_pad = jnp.zeros(w_in.shape[:-1] + (LANES - IDX_HEADS,), w_in.dtype)
    cols = [wxr, wgr, wq, wk, wv, wv, wk, wqi] + [wki] * (LANES // IDX_DIM)
    cols += [wwi, wi_pad, wga, wgl]
    return jnp.concatenate(cols, axis=-1).astype(BF16)


def _block_diag_tiles(w):
    depth, n_blocks, bw, _ = w.shape
    per = MXU_DIM // bw
    w = w.reshape(depth, n_blocks // per, per, bw, bw)
    eye = jnp.eye(per, dtype=w.dtype)
    t = w[:, :, :, :, None, :] * eye[None, None, :, None, :, None]
    return t.reshape(depth, n_blocks // per, MXU_DIM, MXU_DIM).astype(BF16)


def kernel(x, ffn1_norm, ffn1_wg, ffn1_wu, ffn1_wd, mix_norm, w_in, conv_w,
           conv_b, rg_wa, rg_ba, rg_wx, rg_bx, rg_lam, w_att_proj, w_rnn_proj,
           w_out, ffn2_norm, ffn2_wg, ffn2_wu, ffn2_wd, final_norm):
    batch, seq, d = x.shape
    depth = ffn1_norm.shape[0]
    d_rnn = conv_w.shape[-1]
    assert batch == SUBLANES and seq % Q_TILE == 0
    assert (batch * seq) % ROW_TILE == 0 and ROW_TILE % batch == 0

    def row(v):
        return v.reshape(1, -1)

    wg1, wu1, wd1 = (w.astype(BF16) for w in (ffn1_wg, ffn1_wu, ffn1_wd))
    wg2, wu2, wd2 = (w.astype(BF16) for w in (ffn2_wg, ffn2_wu, ffn2_wd))
    w_mix = _inproj_weight(w_in, d_rnn, d)
    wa_t, wx_t = _block_diag_tiles(rg_wa), _block_diag_tiles(rg_wx)
    w_ap, w_rp, w_o = (w.astype(BF16) for w in (w_att_proj, w_rnn_proj, w_out))

    h = x
    for l in range(depth):
        h = _ffn(h, row(ffn1_norm[l]), wg1, wu1, wd1, l, batch, first=(l == 0))
        q, kv, qi, ki, wi, ga, gl, rnn = _mixer_in(
            h, row(mix_norm[l]), w_mix, conv_w[l], row(conv_b[l]), wa_t,
            row(rg_ba[l]), wx_t, row(rg_bx[l]), row(rg_lam[l]), l, batch)
        att = _attention(q, qi, wi, kv, ki, batch)
        h = _merge(att, rnn, ga, gl, h, w_ap, w_rp, w_o, l, batch)
        last = l == depth - 1
        h = _ffn(h, row(ffn2_norm[l]), wg2, wu2, wd2, l, batch,
                 final_g=row(final_norm) if last else None)
    return h
```

```python
import functools

import jax
import jax.numpy as jnp
from jax import lax
from jax.experimental import pallas as pl
from jax.experimental.pallas import tpu as pltpu

F32 = jnp.float32
BF16 = jnp.bfloat16
I32 = jnp.int32

N_HEADS = 8
HEAD_DIM = 64
D_ATT = N_HEADS * HEAD_DIM
IDX_HEADS = 8
IDX_DIM = 32
TOPK_MAX = 256
RNN_BLOCKS = 16
CONV_W = 4
LRU_C = 8.0
EPS = 1e-6

LANES = 128
SUBLANES = 8
BF16_SUBLANES = 16
MXU_DIM = 256
VMEM_LIMIT = 56 * 1024 * 1024

INT_MIN = -2 ** 31
MASK_BIAS = -1e30

ROW_TILE = 512
FFN_ROW_TILE = 1024
Q_TILE = 256
K_CHUNK = 256


def _resident(shape, layer=None):
    zeros = (0,) * len(shape)
    if layer is None:
        return pl.BlockSpec(shape, lambda *_: zeros,
                            pipeline_mode=pl.Buffered(1))
    return pl.BlockSpec((None,) + tuple(shape), lambda *_: (layer,) + zeros,
                        pipeline_mode=pl.Buffered(1))


def _rmsnorm(x, g):
    ms = jnp.mean(x * x, axis=-1, keepdims=True)
    return x * lax.rsqrt(ms + EPS) * g


def _dot(a, b):
    return jnp.dot(a, b, preferred_element_type=F32)


def _dot_nt(a, b):
    return lax.dot_general(a, b, (((1,), (1,)), ((), ())),
                           preferred_element_type=F32)


def _ff_chunks(d_ff):
    chunks, c0 = [], 0
    while c0 < d_ff:
        c1 = min(c0 + 2 * MXU_DIM, d_ff)
        chunks.append((c0, c1))
        c0 = c1
    return chunks


def _slabs(width):
    return width // LANES


def _batch_to_rows(x_ref, slab_scr, batch):
    tt, d = x_ref.shape[1:]
    for b in range(batch):
        for j in range(_slabs(d)):
            slab_scr[j, pl.ds(b, tt, stride=batch), :] = (
                x_ref[b, :, j * LANES:(j + 1) * LANES])
    return jnp.concatenate([slab_scr[j] for j in range(_slabs(d))], axis=1)


def _rows_to_batch(o_ref, y, slab_scr, batch):
    tt, d = o_ref.shape[1:]
    for j in range(_slabs(d)):
        slab_scr[j] = y[:, j * LANES:(j + 1) * LANES]
    for b in range(batch):
        for j in range(_slabs(d)):
            o_ref[b, :, j * LANES:(j + 1) * LANES] = (
                slab_scr[j, pl.ds(b, tt, stride=batch), :])


def _rows_to_seq(o_ref, z, slab_scr, slab0, batch):
    tt = o_ref.shape[0]
    c = z.shape[1]
    for j in range(_slabs(c)):
        slab_scr[slab0 + j] = z[:, j * LANES:(j + 1) * LANES]
    for b in range(batch):
        for j in range(_slabs(c)):
            o_ref[:, b * c + j * LANES:b * c + (j + 1) * LANES] = (
                slab_scr[slab0 + j, pl.ds(b, tt, stride=batch), :]
                .astype(o_ref.dtype))


def _seq_to_rows(x_ref, slab_scr, batch):
    tt = x_ref.shape[0]
    c = x_ref.shape[1] // batch
    for b in range(batch):
        for j in range(_slabs(c)):
            slab_scr[j, pl.ds(b, tt, stride=batch), :] = (
                x_ref[:, b * c + j * LANES:b * c + (j + 1) * LANES].astype(F32))
    return jnp.concatenate([slab_scr[j] for j in range(_slabs(c))], axis=1)


def _ffn_kernel(x_ref, g_ref, wg_ref, wu_ref, wd_ref, *rest, first, final,
                batch):
    rest = list(rest)
    fg_ref = rest.pop(0) if final else None
    o_ref = rest.pop(0)
    slab_scr = rest.pop(0) if (first or final) else None
    x = _batch_to_rows(x_ref, slab_scr, batch) if first else x_ref[...]
    h = _rmsnorm(x, g_ref[...]).astype(BF16)
    acc = None
    for c0, c1 in _ff_chunks(wg_ref.shape[1]):
        g = _dot(h, wg_ref[:, c0:c1])
        u = _dot(h, wu_ref[:, c0:c1])
        a = (g * jax.nn.sigmoid(g) * u).astype(BF16)
        d = _dot(a, wd_ref[c0:c1, :])
        acc = d if acc is None else acc + d
    y = x + 0.5 * acc
    if final:
        _rows_to_batch(o_ref, _rmsnorm(y, fg_ref[...]), slab_scr, batch)
    else:
        o_ref[...] = y


def _ffn(x, g, wg, wu, wd, layer, batch, first=False, final_g=None):
    final = final_g is not None
    if first:
        _, seq, d = x.shape
        rows = seq * batch
    else:
        rows, d = x.shape
        seq = rows // batch
    d_ff = wg.shape[2]
    tt = FFN_ROW_TILE // batch
    row_spec = pl.BlockSpec((FFN_ROW_TILE, d), lambda i: (i, 0))
    batch_spec = pl.BlockSpec((batch, tt, d), lambda i: (0, i, 0))
    in_specs = [batch_spec if first else row_spec, _resident((1, d)),
                _resident((d, d_ff), layer), _resident((d, d_ff), layer),
                _resident((d_ff, d), layer)]
    args = [x, g, wg, wu, wd]
    if final:
        in_specs.append(_resident((1, d)))
        args.append(final_g)
    scratch = ([pltpu.VMEM((_slabs(d), FFN_ROW_TILE, LANES), F32)]
               if (first or final) else [])
    out_shape = (batch, seq, d) if final else (rows, d)
    return pl.pallas_call(
        functools.partial(_ffn_kernel, first=first, final=final, batch=batch),
        out_shape=jax.ShapeDtypeStruct(out_shape, F32),
        grid=(rows // FFN_ROW_TILE,),
        in_specs=in_specs,
        out_specs=batch_spec if final else row_spec,
        scratch_shapes=scratch,
        compiler_params=pltpu.CompilerParams(
            dimension_semantics=("parallel",), vmem_limit_bytes=VMEM_LIMIT),
        name="ffn_first" if first else ("ffn_final" if final else "ffn"),
    )(*args)


ATT_WIDTHS = (D_ATT, 4 * HEAD_DIM, IDX_HEADS * IDX_DIM, LANES, LANES)
ATT_DTYPES = (BF16, BF16, BF16, BF16, F32)


def _mixer_in_kernel(x_ref, g_ref, w_att_ref, w_ref, cw_ref, cb_ref, wa_ref,
                     ba_ref,
                     wx_ref, bx_ref, lam_ref,
                     q_ref, kv_ref, qi_ref, ki_ref, wi_ref, ga_ref, gl_ref,
                     rnn_ref, slab_scr, xs_scr, a_scr, u_scr, h_scr, hc_scr,
                     *, batch):
    tm = x_ref.shape[0]
    c = cw_ref.shape[1]
    d_model = ga_ref.shape[1]
    halo = (CONV_W - 1) * batch
    h = _rmsnorm(x_ref[...], g_ref[...]).astype(BF16)

    @pl.when(pl.program_id(0) == 0)
    def _():
        xs_scr[0:halo, :] = jnp.zeros((halo, c), F32)
        hc_scr[...] = jnp.zeros(hc_scr.shape, F32)

    scales = (HEAD_DIM ** -0.5, None, None, None, (IDX_HEADS * IDX_DIM) ** -0.5)
    att_refs = (q_ref, kv_ref, qi_ref, ki_ref, wi_ref)
    gate_col0 = 2 * c

    def att_operand(i):
        col = sum(ATT_WIDTHS[:i])
        z = _dot(h, w_att_ref[:, col:col + ATT_WIDTHS[i]])
        if scales[i] is not None:
            z = z * scales[i]
        _rows_to_seq(att_refs[i], z, slab_scr, _slabs(sum(ATT_WIDTHS[:i])),
                     batch)

    def gate_logits(o_ref):
        col = gate_col0 if o_ref is ga_ref else gate_col0 + d_model
        o_ref[...] = _dot(h, w_ref[:, col:col + d_model])

    fillers = (lambda: gate_logits(ga_ref),
               lambda: gate_logits(gl_ref),
               lambda: (att_operand(0), att_operand(1)),
               lambda: (att_operand(2), att_operand(3), att_operand(4)))

    n_tiles = c // MXU_DIM
    assert n_tiles == len(fillers)
    for n in range(n_tiles):
        cs = slice(n * MXU_DIM, (n + 1) * MXU_DIM)
        xr = _dot(h, w_ref[:, cs])
        gr = _dot(h, w_ref[:, c + n * MXU_DIM:c + (n + 1) * MXU_DIM])
        xs_scr[halo:halo + tm, cs] = xr
        xc = cb_ref[:, cs]
        for j in range(CONV_W):
            xc = xc + cw_ref[j:j + 1, cs] * xs_scr[j * batch:j * batch + tm, cs]
        xs_scr[0:halo, cs] = xr[tm - halo:tm, :]

        xb = xc.astype(BF16)
        r = jax.nn.sigmoid(_dot(xb, wa_ref[n]) + ba_ref[:, cs])
        gate_i = jax.nn.sigmoid(_dot(xb, wx_ref[n]) + bx_ref[:, cs])
        fillers[n]()
        neg_lam = -lam_ref[:, cs]
        softplus = (jnp.maximum(neg_lam, 0.0)
                    + jnp.log1p(jnp.exp(-jnp.abs(neg_lam))))
        log_a = -LRU_C * r * softplus
        a_scr[:, cs] = jnp.exp(log_a)
        th = jnp.tanh(log_a)
        u_scr[:, cs] = jnp.sqrt(-2.0 * th / (1.0 - th)) * gate_i * xc

        hcur = hc_scr[:, cs]
        for t in range(tm // batch):
            rs = slice(t * batch, (t + 1) * batch)
            hcur = a_scr[rs, cs] * hcur + u_scr[rs, cs]
            h_scr[rs, cs] = hcur
        hc_scr[:, cs] = hcur
        rnn_ref[:, cs] = (h_scr[:, cs] * jax.nn.gelu(gr)).astype(BF16)


def _mixer_in(x, g, w_att, w, cw, cb, wa, ba, wx, bx, lam, layer, batch):
    rows, d = x.shape
    c = cw.shape[1]
    tm = ROW_TILE
    tt = tm // batch
    seq = rows // batch
    halo = (CONV_W - 1) * batch
    assert w.shape[2] == 2 * c + 2 * d and w_att.shape[2] == sum(ATT_WIDTHS)

    def row_spec(n):
        return pl.BlockSpec((tm, n), lambda i: (i, 0))

    def seq_spec(n):
        return pl.BlockSpec((tt, batch * n), lambda i: (i, 0))

    out_shape = [jax.ShapeDtypeStruct((seq, batch * n), t)
                 for n, t in zip(ATT_WIDTHS, ATT_DTYPES)]
    out_shape += [jax.ShapeDtypeStruct((rows, d), F32)] * 2
    out_shape += [jax.ShapeDtypeStruct((rows, c), BF16)]
    out_specs = [seq_spec(n) for n in ATT_WIDTHS]
    out_specs += [row_spec(d), row_spec(d), row_spec(c)]
    return pl.pallas_call(
        functools.partial(_mixer_in_kernel, batch=batch),
        out_shape=out_shape,
        grid=(rows // tm,),
        in_specs=[row_spec(d), _resident((1, d)),
                  _resident(w_att.shape[1:], layer),
                  _resident(w.shape[1:], layer), _resident(cw.shape),
                  _resident((1, c)), _resident(wa.shape[1:], layer),
                  _resident((1, c)), _resident(wx.shape[1:], layer),
                  _resident((1, c)), _resident((1, c))],
        out_specs=out_specs,
        scratch_shapes=[
            pltpu.VMEM((_slabs(sum(ATT_WIDTHS)), tm, LANES), F32),
            pltpu.VMEM((halo + tm, c), F32),
            pltpu.VMEM((tm, c), F32), pltpu.VMEM((tm, c), F32),
            pltpu.VMEM((tm, c), F32), pltpu.VMEM((batch, c), F32)],
        compiler_params=pltpu.CompilerParams(
            dimension_semantics=("arbitrary",), vmem_limit_bytes=VMEM_LIMIT),
        name="mixer_in",
    )(x, g, w_att, w, cw, cb, wa, ba, wx, bx, lam)


VT_ROWS = HEAD_DIM + 16


def _key_to_float(key):
    bits = jnp.where(key < 0, INT_MIN - key, key)
    return pltpu.bitcast(bits, F32)


def _for_chunks(n, body):
    def pair(i, carry):
        body(2 * i)
        body(2 * i + 1)
        return carry

    lax.fori_loop(0, n // 2, pair, 0)

    @pl.when(n % 2 == 1)
    def _():
        body(n - 1)


def _attn_kernel(q_ref, qi_ref, wi_ref, kv_ref, ki_ref, o_ref,
                 qm_scr, qim_scr, wit_scr, vt_scr, sc_scr, scb_scr, sall_scr, m_scr,
                 acc_scr, mpart_scr, ties_scr, *, topk):
    tq, ck = Q_TILE, K_CHUNK
    n_kc = sc_scr.shape[0]
    seq = n_kc * ck
    qt = pl.program_id(1)
    n_chunks = (qt * tq) // ck + tq // ck
    lane = lax.broadcasted_iota(I32, (tq, LANES), 1)
    key_iota = lax.broadcasted_iota(I32, (ck, tq), 0)
    qry_iota = lax.broadcasted_iota(I32, (ck, tq), 1)
    kf = float(topk)

    for h in range(N_HEADS):
        qp = q_ref[:, (h // 2) * LANES:(h // 2 + 1) * LANES].astype(F32)
        lo = (h % 2) * HEAD_DIM
        qm_scr[h] = jnp.where((lane >= lo) & (lane < lo + HEAD_DIM), qp,
                              0.0).astype(BF16)
    for h in range(IDX_HEADS):
        qp = qi_ref[:, (h // 4) * LANES:(h // 4 + 1) * LANES].astype(F32)
        lo = (h % 4) * IDX_DIM
        qim_scr[h] = jnp.where((lane >= lo) & (lane < lo + IDX_DIM), qp,
                               0.0).astype(BF16)
    wit_scr[...] = jnp.transpose(wi_ref[...])

    @pl.when(qt == 0)
    def _():
        lane_c = lax.broadcasted_iota(I32, (ck, LANES), 1)
        for c in range(n_kc):
            v1 = jnp.where(lane_c < HEAD_DIM,
                           kv_ref[c * ck:(c + 1) * ck, LANES:2 * LANES].astype(F32),
                           1.0)
            vt_scr[c] = jnp.transpose(v1)[0:VT_ROWS, :].astype(BF16)

    def score_chunk(kc):
        k0 = pl.multiple_of(kc * ck, ck)
        kic = ki_ref[pl.ds(k0, ck), :]
        acc = jnp.zeros((ck, tq), F32)
        for h in range(IDX_HEADS):
            acc = acc + (jnp.maximum(_dot_nt(kic, qim_scr[h]), 0.0)
                         * wit_scr[h:h + 1, :])
        causal = key_iota - qry_iota <= qt * tq - kc * ck
        sc = jnp.where(causal, acc, -jnp.inf)
        sc_scr[kc] = sc
        scb_scr[kc] = sc.astype(BF16)

    _for_chunks(n_chunks, score_chunk)

    def count(pred):
        acc_rows = 2 * SUBLANES

        def chunk(kc, c):
            hit = jnp.where(pred(sc_scr[kc], kc), 1.0, 0.0)
            return c + jnp.sum(hit.reshape(ck // acc_rows, acc_rows, tq), axis=0)

        c = lax.fori_loop(0, n_chunks, chunk, jnp.zeros((acc_rows, tq), F32))
        return jnp.sum(c, axis=0, keepdims=True)

    def count_bf16(cand):
        acc_rows = 2 * BF16_SUBLANES
        one, zero = jnp.ones((), BF16), jnp.zeros((), BF16)

        def chunk(kc, c):
            hit = jnp.where(scb_scr[kc] >= cand, one, zero)
            part = hit[0:acc_rows]
            for r0 in range(acc_rows, ck, acc_rows):
                part = part + hit[r0:r0 + acc_rows]
            return c + part.astype(F32)

        c = lax.fori_loop(0, n_chunks, chunk, jnp.zeros((acc_rows, tq), F32))
        return jnp.sum(c, axis=0, keepdims=True)

    def coarse_step(i, thr16):
        cand16 = thr16 + lax.shift_left(jnp.int32(1), 15 - i)
        cand_f = _key_to_float(cand16 * 65536)
        tot = count_bf16(cand_f.astype(BF16))
        return jnp.where(tot >= kf, cand16, thr16)

    thr16 = lax.fori_loop(0, 16, coarse_step,
                          jnp.full((1, tq), -2 ** 15, I32))

    def fine_step(i, thr):
        cand = thr + lax.shift_left(jnp.int32(1), 16 - i)
        cand_f = _key_to_float(cand)
        tot = count(lambda s, kc: s >= cand_f)
        return jnp.where(tot >= kf, cand, thr)

    thr = lax.fori_loop(0, 17, fine_step, (thr16 - 1) * 65536)

    thr_f = _key_to_float(thr)
    short = jnp.logical_not(thr_f > -jnp.inf)
    thr_f = jnp.where(short, -jnp.inf, thr_f)
    n_gt = count(lambda s, kc: s > thr_f)
    need = jnp.where(short, 0.0, kf - n_gt)
    lower_tri = jnp.where(
        lax.broadcasted_iota(I32, (ck, ck), 0)
        >= lax.broadcasted_iota(I32, (ck, ck), 1), 1.0, 0.0).astype(BF16)

    mpart_scr[...] = jnp.full(mpart_scr.shape, MASK_BIAS, F32)
    acc_scr[...] = jnp.zeros(acc_scr.shape, F32)

    ties_scr[...] = jnp.zeros(ties_scr.shape, F32)

    def scores_chunk(kc):
        k0 = pl.multiple_of(kc * ck, ck)
        s_idx = sc_scr[kc]
        tie = s_idx == thr_f
        tie_f = jnp.where(tie, 1.0, 0.0)
        ties_before = ties_scr[...]
        ties_scr[...] = ties_before + jnp.sum(tie_f, axis=0, keepdims=True)
        rank = ties_before + _dot(lower_tri, tie_f.astype(BF16))
        keep = (s_idx > thr_f) | (tie & (rank <= need))
        bias = jnp.where(keep, 0.0, MASK_BIAS)
        for h in range(N_HEADS):
            c0 = 0 if h % 2 == 0 else LANES
            k_mat = kv_ref[pl.ds(k0, ck), c0:c0 + LANES]
            s = _dot_nt(k_mat, qm_scr[h]) + bias
            sall_scr[kc, h] = s
            mpart_scr[h] = jnp.maximum(
                mpart_scr[h],
                jnp.max(s.reshape(ck // SUBLANES, SUBLANES, tq), axis=0))

    _for_chunks(n_chunks, scores_chunk)
    for h in range(N_HEADS):
        m_scr[h:h + 1, :] = jnp.max(mpart_scr[h], axis=0, keepdims=True)

    def value_chunk(kc):
        vt_c = vt_scr[kc]
        for h in range(N_HEADS):
            p = jnp.exp(sall_scr[kc, h] - m_scr[h:h + 1, :]).astype(BF16)
            acc_scr[h] = acc_scr[h] + _dot(vt_c, p)

    _for_chunks(n_chunks, value_chunk)

    outs = []
    for h in range(N_HEADS):
        a = acc_scr[h]
        outs.append(a[0:HEAD_DIM, :] / a[HEAD_DIM:HEAD_DIM + 1, :])
    o_ref[...] = jnp.transpose(jnp.concatenate(outs, axis=0)).astype(BF16)


def _attention(q, qi, wi, kv, ki, batch):
    tq, ck = Q_TILE, K_CHUNK
    seq = q.shape[0]
    topk = min(TOPK_MAX, seq // 4)
    nq = seq // tq

    def q_spec(c):
        return pl.BlockSpec((tq, c), lambda b, i: (i, b))

    def kv_spec(c):
        return pl.BlockSpec((seq, c), lambda b, i: (0, b))

    return pl.pallas_call(
        functools.partial(_attn_kernel, topk=topk),
        out_shape=jax.ShapeDtypeStruct((seq, batch * D_ATT), BF16),
        grid=(batch, nq),
        in_specs=[q_spec(D_ATT), q_spec(IDX_HEADS * IDX_DIM), q_spec(LANES),
                  kv_spec(4 * HEAD_DIM), kv_spec(LANES)],
        out_specs=q_spec(D_ATT),
        scratch_shapes=[
            pltpu.VMEM((N_HEADS, tq, LANES), BF16),
            pltpu.VMEM((IDX_HEADS, tq, LANES), BF16),
            pltpu.VMEM((LANES, tq), F32),
            pltpu.VMEM((seq // ck, VT_ROWS, ck), BF16),
            pltpu.VMEM((seq // ck, ck, tq), F32),
            pltpu.VMEM((seq // ck, ck, tq), BF16),
            pltpu.VMEM((seq // ck, N_HEADS, ck, tq), F32),
            pltpu.VMEM((N_HEADS, tq), F32),
            pltpu.VMEM((N_HEADS, VT_ROWS, tq), F32),
            pltpu.VMEM((N_HEADS, SUBLANES, tq), F32),
            pltpu.VMEM((1, tq), F32),
        ],
        compiler_params=pltpu.CompilerParams(
            dimension_semantics=("parallel", "arbitrary"),
            vmem_limit_bytes=VMEM_LIMIT),
        name="dsa_attention",
    )(q, qi, wi, kv, ki)


def _merge_kernel(att_ref, rnn_ref, ga_ref, gl_ref, x_ref, wa_ref, wr_ref,
                  wo_ref, o_ref, slab_scr, *, batch):
    att = _seq_to_rows(att_ref, slab_scr, batch).astype(BF16)
    pa = _dot(att, wa_ref[...])
    pr = _dot(rnn_ref[...], wr_ref[...])
    merged = (jax.nn.sigmoid(ga_ref[...]) * pa
              + jax.nn.sigmoid(gl_ref[...]) * pr)
    o_ref[...] = x_ref[...] + _dot(merged.astype(BF16), wo_ref[...])


def _merge(att, rnn, ga, gl, x, wa, wr, wo, layer, batch):
    rows, d = x.shape
    tt = ROW_TILE // batch

    def row_spec(c):
        return pl.BlockSpec((ROW_TILE, c), lambda i: (i, 0))

    return pl.pallas_call(
        functools.partial(_merge_kernel, batch=batch),
        out_shape=jax.ShapeDtypeStruct((rows, d), F32),
        grid=(rows // ROW_TILE,),
        in_specs=[pl.BlockSpec((tt, att.shape[1]), lambda i: (i, 0)),
                  row_spec(rnn.shape[1]), row_spec(d), row_spec(d),
                  row_spec(d), _resident(wa.shape[1:], layer),
                  _resident(wr.shape[1:], layer),
                  _resident(wo.shape[1:], layer)],
        out_specs=row_spec(d),
        scratch_shapes=[pltpu.VMEM((_slabs(D_ATT), ROW_TILE, LANES), F32)],
        compiler_params=pltpu.CompilerParams(
            dimension_semantics=("parallel",), vmem_limit_bytes=VMEM_LIMIT),
        name="merge",
    )(att, rnn, ga, gl, x, wa, wr, wo)


def _inproj_weights(w_in, d_rnn, d_model):
    splits = (D_ATT, HEAD_DIM, HEAD_DIM, IDX_HEADS * IDX_DIM, IDX_DIM,
              IDX_HEADS)
    parts, c = [], 0
    for n in splits:
        parts.append(w_in[..., c:c + n])
        c += n
    assert c + 2 * d_rnn + 2 * d_model == w_in.shape[-1]
    wq, wk, wv, wqi, wki, wwi = parts
    wi_pad = jnp.zeros(w_in.shape[:-1] + (LANES - IDX_HEADS,), w_in.dtype)
    cols = [wq, wk, wv, wv, wk, wqi] + [wki] * (LANES // IDX_DIM)
    cols += [wwi, wi_pad]
    return (jnp.concatenate(cols, axis=-1).astype(BF16),
            w_in[..., c:].astype(BF16))


def _block_diag_tiles(w):
    depth, n_blocks, bw, _ = w.shape
    per = MXU_DIM // bw
    w = w.reshape(depth, n_blocks // per, per, bw, bw)
    eye = jnp.eye(per, dtype=w.dtype)
    t = w[:, :, :, :, None, :] * eye[None, None, :, None, :, None]
    return t.reshape(depth, n_blocks // per, MXU_DIM, MXU_DIM).astype(BF16)


def kernel(x, ffn1_norm, ffn1_wg, ffn1_wu, ffn1_wd, mix_norm, w_in, conv_w,
           conv_b, rg_wa, rg_ba, rg_wx, rg_bx, rg_lam, w_att_proj, w_rnn_proj,
           w_out, ffn2_norm, ffn2_wg, ffn2_wu, ffn2_wd, final_norm):
    batch, seq, d = x.shape
    depth = ffn1_norm.shape[0]
    d_rnn = conv_w.shape[-1]
    assert batch == SUBLANES and seq % Q_TILE == 0
    assert (batch * seq) % FFN_ROW_TILE == 0 and ROW_TILE % batch == 0

    def row(v):
        return v.reshape(1, -1)

    wg1, wu1, wd1 = (w.astype(BF16) for w in (ffn1_wg, ffn1_wu, ffn1_wd))
    wg2, wu2, wd2 = (w.astype(BF16) for w in (ffn2_wg, ffn2_wu, ffn2_wd))
    w_att, w_wide = _inproj_weights(w_in, d_rnn, d)
    wa_t, wx_t = _block_diag_tiles(rg_wa), _block_diag_tiles(rg_wx)
    w_ap, w_rp, w_o = (w.astype(BF16) for w in (w_att_proj, w_rnn_proj, w_out))

    h = x
    for l in range(depth):
        h = _ffn(h, row(ffn1_norm[l]), wg1, wu1, wd1, l, batch, first=(l == 0))
        q, kv, qi, ki, wi, ga, gl, rnn = _mixer_in(
            h, row(mix_norm[l]), w_att, w_wide, conv_w[l], row(conv_b[l]), wa_t,
            row(rg_ba[l]), wx_t, row(rg_bx[l]), row(rg_lam[l]), l, batch)
        att = _attention(q, qi, wi, kv, ki, batch)
        h = _merge(att, rnn, ga, gl, h, w_ap, w_rp, w_o, l, batch)
        last = l == depth - 1
        h = _ffn(h, row(ffn2_norm[l]), wg2, wu2, wd2, l, batch,
                 final_g=row(final_norm) if last else None)
    return h
```

```python
import functools

import jax
import jax.numpy as jnp
from jax import lax
from jax.experimental import pallas as pl
from jax.experimental.pallas import tpu as pltpu

F32 = jnp.float32
BF16 = jnp.bfloat16
I32 = jnp.int32

N_HEADS = 8
HEAD_DIM = 64
D_ATT = N_HEADS * HEAD_DIM
IDX_HEADS = 8
IDX_DIM = 32
TOPK_MAX = 256
RNN_BLOCKS = 16
CONV_W = 4
LRU_C = 8.0
EPS = 1e-6

LANES = 128
SUBLANES = 8
BF16_SUBLANES = 16
MXU_DIM = 256
VMEM_LIMIT = 56 * 1024 * 1024

INT_MIN = -2 ** 31
MASK_BIAS = -1e30

ROW_TILE = 512
FFN_ROW_TILE = 1024
Q_TILE = 256
K_CHUNK = 256


def _resident(shape, layer=None):
    zeros = (0,) * len(shape)
    if layer is None:
        return pl.BlockSpec(shape, lambda *_: zeros,
                            pipeline_mode=pl.Buffered(1))
    return pl.BlockSpec((None,) + tuple(shape), lambda *_: (layer,) + zeros,
                        pipeline_mode=pl.Buffered(1))


def _rmsnorm(x, g):
    ms = jnp.mean(x * x, axis=-1, keepdims=True)
    return x * lax.rsqrt(ms + EPS) * g


def _dot(a, b):
    return jnp.dot(a, b, preferred_element_type=F32)


def _dot_nt(a, b):
    return lax.dot_general(a, b, (((1,), (1,)), ((), ())),
                           preferred_element_type=F32)


def _ff_chunks(d_ff):
    chunks, c0 = [], 0
    while c0 < d_ff:
        c1 = min(c0 + 2 * MXU_DIM, d_ff)
        chunks.append((c0, c1))
        c0 = c1
    return chunks


def _slabs(width):
    return width // LANES


def _batch_to_rows(x_ref, slab_scr, batch):
    tt, d = x_ref.shape[1:]
    for b in range(batch):
        for j in range(_slabs(d)):
            slab_scr[j, pl.ds(b, tt, stride=batch), :] = (
                x_ref[b, :, j * LANES:(j + 1) * LANES])
    return jnp.concatenate([slab_scr[j] for j in range(_slabs(d))], axis=1)


def _rows_to_batch(o_ref, y, slab_scr, batch):
    tt, d = o_ref.shape[1:]
    for j in range(_slabs(d)):
        slab_scr[j] = y[:, j * LANES:(j + 1) * LANES]
    for b in range(batch):
        for j in range(_slabs(d)):
            o_ref[b, :, j * LANES:(j + 1) * LANES] = (
                slab_scr[j, pl.ds(b, tt, stride=batch), :])


def _rows_to_seq(o_ref, z, slab_scr, slab0, batch):
    tt = o_ref.shape[0]
    c = z.shape[1]
    for j in range(_slabs(c)):
        slab_scr[slab0 + j] = z[:, j * LANES:(j + 1) * LANES]
    for b in range(batch):
        for j in range(_slabs(c)):
            o_ref[:, b * c + j * LANES:b * c + (j + 1) * LANES] = (
                slab_scr[slab0 + j, pl.ds(b, tt, stride=batch), :]
                .astype(o_ref.dtype))


def _seq_to_rows(x_ref, slab_scr, batch):
    tt = x_ref.shape[0]
    c = x_ref.shape[1] // batch
    for b in range(batch):
        for j in range(_slabs(c)):
            slab_scr[j, pl.ds(b, tt, stride=batch), :] = (
                x_ref[:, b * c + j * LANES:b * c + (j + 1) * LANES].astype(F32))
    return jnp.concatenate([slab_scr[j] for j in range(_slabs(c))], axis=1)


def _ffn_kernel(x_ref, g_ref, wg_ref, wu_ref, wd_ref, *rest, first, final,
                batch):
    rest = list(rest)
    fg_ref = rest.pop(0) if final else None
    o_ref = rest.pop(0)
    slab_scr = rest.pop(0) if (first or final) else None
    x = _batch_to_rows(x_ref, slab_scr, batch) if first else x_ref[...]
    h = _rmsnorm(x, g_ref[...]).astype(BF16)
    acc = None
    for c0, c1 in _ff_chunks(wg_ref.shape[1]):
        g = _dot(h, wg_ref[:, c0:c1])
        u = _dot(h, wu_ref[:, c0:c1])
        a = (g * jax.nn.sigmoid(g) * u).astype(BF16)
        d = _dot(a, wd_ref[c0:c1, :])
        acc = d if acc is None else acc + d
    y = x + 0.5 * acc
    if final:
        _rows_to_batch(o_ref, _rmsnorm(y, fg_ref[...]), slab_scr, batch)
    else:
        o_ref[...] = y


def _ffn(x, g, wg, wu, wd, layer, batch, first=False, final_g=None):
    final = final_g is not None
    if first:
        _, seq, d = x.shape
        rows = seq * batch
    else:
        rows, d = x.shape
        seq = rows // batch
    d_ff = wg.shape[2]
    tt = FFN_ROW_TILE // batch
    row_spec = pl.BlockSpec((FFN_ROW_TILE, d), lambda i: (i, 0))
    batch_spec = pl.BlockSpec((batch, tt, d), lambda i: (0, i, 0))
    in_specs = [batch_spec if first else row_spec, _resident((1, d)),
                _resident((d, d_ff), layer), _resident((d, d_ff), layer),
                _resident((d_ff, d), layer)]
    args = [x, g, wg, wu, wd]
    if final:
        in_specs.append(_resident((1, d)))
        args.append(final_g)
    scratch = ([pltpu.VMEM((_slabs(d), FFN_ROW_TILE, LANES), F32)]
               if (first or final) else [])
    out_shape = (batch, seq, d) if final else (rows, d)
    return pl.pallas_call(
        functools.partial(_ffn_kernel, first=first, final=final, batch=batch),
        out_shape=jax.ShapeDtypeStruct(out_shape, F32),
        grid=(rows // FFN_ROW_TILE,),
        in_specs=in_specs,
        out_specs=batch_spec if final else row_spec,
        scratch_shapes=scratch,
        compiler_params=pltpu.CompilerParams(
            dimension_semantics=("parallel",), vmem_limit_bytes=VMEM_LIMIT),
        name="ffn_first" if first else ("ffn_final" if final else "ffn"),
    )(*args)


ATT_WIDTHS = (D_ATT, 4 * HEAD_DIM, IDX_HEADS * IDX_DIM, LANES, LANES)
ATT_DTYPES = (BF16, BF16, BF16, BF16, F32)


def _mixer_in_kernel(x_ref, g_ref, w_att_ref, w_ref, cw_ref, cb_ref, wa_ref,
                     ba_ref,
                     wx_ref, bx_ref, lam_ref,
                     q_ref, kv_ref, qi_ref, ki_ref, wi_ref, ga_ref, gl_ref,
                     rnn_ref, slab_scr, xs_scr, a_scr, u_scr, h_scr, hc_scr,
                     *, batch):
    tm = x_ref.shape[0]
    c = cw_ref.shape[1]
    d_model = ga_ref.shape[1]
    halo = (CONV_W - 1) * batch
    h = _rmsnorm(x_ref[...], g_ref[...]).astype(BF16)

    @pl.when(pl.program_id(0) == 0)
    def _():
        xs_scr[0:halo, :] = jnp.zeros((halo, c), F32)
        hc_scr[...] = jnp.zeros(hc_scr.shape, F32)

    scales = (HEAD_DIM ** -0.5, None, None, None, (IDX_HEADS * IDX_DIM) ** -0.5)
    att_refs = (q_ref, kv_ref, qi_ref, ki_ref, wi_ref)
    gate_col0 = 2 * c

    def att_operand(i):
        col = sum(ATT_WIDTHS[:i])
        z = _dot(h, w_att_ref[:, col:col + ATT_WIDTHS[i]])
        if scales[i] is not None:
            z = z * scales[i]
        _rows_to_seq(att_refs[i], z, slab_scr, _slabs(sum(ATT_WIDTHS[:i])),
                     batch)

    def gate_logits(o_ref):
        col = gate_col0 if o_ref is ga_ref else gate_col0 + d_model
        o_ref[...] = _dot(h, w_ref[:, col:col + d_model])

    fillers = (lambda: gate_logits(ga_ref),
               lambda: gate_logits(gl_ref),
               lambda: (att_operand(0), att_operand(1)),
               lambda: (att_operand(2), att_operand(3), att_operand(4)))

    n_tiles = c // MXU_DIM
    assert n_tiles == len(fillers)
    for n in range(n_tiles):
        cs = slice(n * MXU_DIM, (n + 1) * MXU_DIM)
        xr = _dot(h, w_ref[:, cs])
        gr = _dot(h, w_ref[:, c + n * MXU_DIM:c + (n + 1) * MXU_DIM])
        xs_scr[halo:halo + tm, cs] = xr
        xc = cb_ref[:, cs]
        for j in range(CONV_W):
            xc = xc + cw_ref[j:j + 1, cs] * xs_scr[j * batch:j * batch + tm, cs]
        xs_scr[0:halo, cs] = xr[tm - halo:tm, :]

        xb = xc.astype(BF16)
        r = jax.nn.sigmoid(_dot(xb, wa_ref[n]) + ba_ref[:, cs])
        gate_i = jax.nn.sigmoid(_dot(xb, wx_ref[n]) + bx_ref[:, cs])
        fillers[n]()
        neg_lam = -lam_ref[:, cs]
        softplus = (jnp.maximum(neg_lam, 0.0)
                    + jnp.log1p(jnp.exp(-jnp.abs(neg_lam))))
        log_a = -LRU_C * r * softplus
        a_scr[:, cs] = jnp.exp(log_a)
        th = jnp.tanh(log_a)
        u_scr[:, cs] = jnp.sqrt(-2.0 * th / (1.0 - th)) * gate_i * xc

        hcur = hc_scr[:, cs]
        for t in range(tm // batch):
            rs = slice(t * batch, (t + 1) * batch)
            hcur = a_scr[rs, cs] * hcur + u_scr[rs, cs]
            h_scr[rs, cs] = hcur
        hc_scr[:, cs] = hcur
        rnn_ref[:, cs] = (h_scr[:, cs] * jax.nn.gelu(gr)).astype(BF16)


def _mixer_in(x, g, w_att, w, cw, cb, wa, ba, wx, bx, lam, layer, batch):
    rows, d = x.shape
    c = cw.shape[1]
    tm = ROW_TILE
    tt = tm // batch
    seq = rows // batch
    halo = (CONV_W - 1) * batch
    assert w.shape[2] == 2 * c + 2 * d and w_att.shape[2] == sum(ATT_WIDTHS)

    def row_spec(n):
        return pl.BlockSpec((tm, n), lambda i: (i, 0))

    def seq_spec(n):
        return pl.BlockSpec((tt, batch * n), lambda i: (i, 0))

    out_shape = [jax.ShapeDtypeStruct((seq, batch * n), t)
                 for n, t in zip(ATT_WIDTHS, ATT_DTYPES)]
    out_shape += [jax.ShapeDtypeStruct((rows, d), F32)] * 2
    out_shape += [jax.ShapeDtypeStruct((rows, c), BF16)]
    out_specs = [seq_spec(n) for n in ATT_WIDTHS]
    out_specs += [row_spec(d), row_spec(d), row_spec(c)]
    return pl.pallas_call(
        functools.partial(_mixer_in_kernel, batch=batch),
        out_shape=out_shape,
        grid=(rows // tm,),
        in_specs=[row_spec(d), _resident((1, d)),
                  _resident(w_att.shape[1:], layer),
                  _resident(w.shape[1:], layer), _resident(cw.shape),
                  _resident((1, c)), _resident(wa.shape[1:], layer),
                  _resident((1, c)), _resident(wx.shape[1:], layer),
                  _resident((1, c)), _resident((1, c))],
        out_specs=out_specs,
        scratch_shapes=[
            pltpu.VMEM((_slabs(sum(ATT_WIDTHS)), tm, LANES), F32),
            pltpu.VMEM((halo + tm, c), F32),
            pltpu.VMEM((tm, c), F32), pltpu.VMEM((tm, c), F32),
            pltpu.VMEM((tm, c), F32), pltpu.VMEM((batch, c), F32)],
        compiler_params=pltpu.CompilerParams(
            dimension_semantics=("arbitrary",), vmem_limit_bytes=VMEM_LIMIT),
        name="mixer_in",
    )(x, g, w_att, w, cw, cb, wa, ba, wx, bx, lam)


VT_ROWS = HEAD_DIM + 16


def _key_to_float(key):
    bits = jnp.where(key < 0, INT_MIN - key, key)
    return pltpu.bitcast(bits, F32)


def _for_chunks(n, body):
    def quad(i, carry):
        for j in range(4):
            body(4 * i + j)
        return carry

    lax.fori_loop(0, n // 4, quad, 0)
    base = (n // 4) * 4

    @pl.when(n % 4 >= 2)
    def _():
        body(base)
        body(base + 1)

    @pl.when(n % 2 == 1)
    def _():
        body(n - 1)


def _attn_kernel(q_ref, qi_ref, wi_ref, kv_ref, ki_ref, o_ref,
                 qm_scr, qim_scr, wit_scr, vt_scr, sc_scr, scb_scr, sall_scr, m_scr,
                 acc_scr, mpart_scr, ties_scr, *, topk):
    tq, ck = Q_TILE, K_CHUNK
    n_kc = sc_scr.shape[0]
    seq = n_kc * ck
    qt = pl.program_id(1)
    n_chunks = (qt * tq) // ck + tq // ck
    lane = lax.broadcasted_iota(I32, (tq, LANES), 1)
    key_iota = lax.broadcasted_iota(I32, (ck, tq), 0)
    qry_iota = lax.broadcasted_iota(I32, (ck, tq), 1)
    kf = float(topk)

    for h in range(N_HEADS):
        qp = q_ref[:, (h // 2) * LANES:(h // 2 + 1) * LANES].astype(F32)
        lo = (h % 2) * HEAD_DIM
        qm_scr[h] = jnp.where((lane >= lo) & (lane < lo + HEAD_DIM), qp,
                              0.0).astype(BF16)
    for h in range(IDX_HEADS):
        qp = qi_ref[:, (h // 4) * LANES:(h // 4 + 1) * LANES].astype(F32)
        lo = (h % 4) * IDX_DIM
        qim_scr[h] = jnp.where((lane >= lo) & (lane < lo + IDX_DIM), qp,
                               0.0).astype(BF16)
    wit_scr[...] = jnp.transpose(wi_ref[...])

    @pl.when(qt == 0)
    def _():
        lane_c = lax.broadcasted_iota(I32, (ck, LANES), 1)
        for c in range(n_kc):
            v1 = jnp.where(lane_c < HEAD_DIM,
                           kv_ref[c * ck:(c + 1) * ck, LANES:2 * LANES].astype(F32),
                           1.0)
            vt_scr[c] = jnp.transpose(v1)[0:VT_ROWS, :].astype(BF16)

    def score_chunk(kc):
        k0 = pl.multiple_of(kc * ck, ck)
        kic = ki_ref[pl.ds(k0, ck), :]
        acc = jnp.zeros((ck, tq), F32)
        for h in range(IDX_HEADS):
            acc = acc + (jnp.maximum(_dot_nt(kic, qim_scr[h]), 0.0)
                         * wit_scr[h:h + 1, :])
        causal = key_iota - qry_iota <= qt * tq - kc * ck
        sc = jnp.where(causal, acc, -jnp.inf)
        sc_scr[kc] = sc
        scb_scr[kc] = sc.astype(BF16)

    _for_chunks(n_chunks, score_chunk)

    def count(pred):
        acc_rows = 2 * SUBLANES

        def chunk(kc, c):
            hit = jnp.where(pred(sc_scr[kc], kc), 1.0, 0.0)
            return c + jnp.sum(hit.reshape(ck // acc_rows, acc_rows, tq), axis=0)

        c = lax.fori_loop(0, n_chunks, chunk, jnp.zeros((acc_rows, tq), F32))
        return jnp.sum(c, axis=0, keepdims=True)

    def count_bf16(cand):
        acc_rows = 2 * BF16_SUBLANES
        one, zero = jnp.ones((), BF16), jnp.zeros((), BF16)

        def chunk(kc, c):
            hit = jnp.where(scb_scr[kc] >= cand, one, zero)
            part = hit[0:acc_rows]
            for r0 in range(acc_rows, ck, acc_rows):
                part = part + hit[r0:r0 + acc_rows]
            return c + part.astype(F32)

        c = lax.fori_loop(0, n_chunks, chunk, jnp.zeros((acc_rows, tq), F32))
        return jnp.sum(c, axis=0, keepdims=True)

    def coarse_step(i, thr16):
        cand16 = thr16 + lax.shift_left(jnp.int32(1), 15 - i)
        cand_f = _key_to_float(cand16 * 65536)
        tot = count_bf16(cand_f.astype(BF16))
        return jnp.where(tot >= kf, cand16, thr16)

    thr16 = lax.fori_loop(0, 16, coarse_step,
                          jnp.full((1, tq), -2 ** 15, I32))

    def fine_step(i, thr):
        cand = thr + lax.shift_left(jnp.int32(1), 16 - i)
        cand_f = _key_to_float(cand)
        tot = count(lambda s, kc: s >= cand_f)
        return jnp.where(tot >= kf, cand, thr)

    thr = lax.fori_loop(0, 17, fine_step, (thr16 - 1) * 65536)

    thr_f = _key_to_float(thr)
    short = jnp.logical_not(thr_f > -jnp.inf)
    thr_f = jnp.where(short, -jnp.inf, thr_f)
    n_gt = count(lambda s, kc: s > thr_f)
    need = jnp.where(short, 0.0, kf - n_gt)
    lower_tri = jnp.where(
        lax.broadcasted_iota(I32, (ck, ck), 0)
        >= lax.broadcasted_iota(I32, (ck, ck), 1), 1.0, 0.0).astype(BF16)

    mpart_scr[...] = jnp.full(mpart_scr.shape, MASK_BIAS, F32)
    acc_scr[...] = jnp.zeros(acc_scr.shape, F32)

    ties_scr[...] = jnp.zeros(ties_scr.shape, F32)

    def scores_chunk(kc):
        k0 = pl.multiple_of(kc * ck, ck)
        s_idx = sc_scr[kc]
        tie = s_idx == thr_f
        tie_f = jnp.where(tie, 1.0, 0.0)
        ties_before = ties_scr[...]
        ties_scr[...] = ties_before + jnp.sum(tie_f, axis=0, keepdims=True)
        rank = ties_before + _dot(lower_tri, tie_f.astype(BF16))
        keep = (s_idx > thr_f) | (tie & (rank <= need))
        bias = jnp.where(keep, 0.0, MASK_BIAS)
        for h in range(N_HEADS):
            c0 = 0 if h % 2 == 0 else LANES
            k_mat = kv_ref[pl.ds(k0, ck), c0:c0 + LANES]
            s = _dot_nt(k_mat, qm_scr[h]) + bias
            sall_scr[kc, h] = s
            mpart_scr[h] = jnp.maximum(
                mpart_scr[h],
                jnp.max(s.reshape(ck // SUBLANES, SUBLANES, tq), axis=0))

    _for_chunks(n_chunks, scores_chunk)
    for h in range(N_HEADS):
        m_scr[h:h + 1, :] = jnp.max(mpart_scr[h], axis=0, keepdims=True)

    def value_chunk(kc):
        vt_c = vt_scr[kc]
        for h in range(N_HEADS):
            p = jnp.exp(sall_scr[kc, h] - m_scr[h:h + 1, :]).astype(BF16)
            acc_scr[h] = acc_scr[h] + _dot(vt_c, p)

    _for_chunks(n_chunks, value_chunk)

    outs = []
    for h in range(N_HEADS):
        a = acc_scr[h]
        outs.append(a[0:HEAD_DIM, :] / a[HEAD_DIM:HEAD_DIM + 1, :])
    o_ref[...] = jnp.transpose(jnp.concatenate(outs, axis=0)).astype(BF16)


def _attention(q, qi, wi, kv, ki, batch):
    tq, ck = Q_TILE, K_CHUNK
    seq = q.shape[0]
    topk = min(TOPK_MAX, seq // 4)
    nq = seq // tq

    def q_spec(c):
        return pl.BlockSpec((tq, c), lambda b, i: (i, b))

    def kv_spec(c):
        return pl.BlockSpec((seq, c), lambda b, i: (0, b))

    return pl.pallas_call(
        functools.partial(_attn_kernel, topk=topk),
        out_shape=jax.ShapeDtypeStruct((seq, batch * D_ATT), BF16),
        grid=(batch, nq),
        in_specs=[q_spec(D_ATT), q_spec(IDX_HEADS * IDX_DIM), q_spec(LANES),
                  kv_spec(4 * HEAD_DIM), kv_spec(LANES)],
        out_specs=q_spec(D_ATT),
        scratch_shapes=[
            pltpu.VMEM((N_HEADS, tq, LANES), BF16),
            pltpu.VMEM((IDX_HEADS, tq, LANES), BF16),
            pltpu.VMEM((LANES, tq), F32),
            pltpu.VMEM((seq // ck, VT_ROWS, ck), BF16),
            pltpu.VMEM((seq // ck, ck, tq), F32),
            pltpu.VMEM((seq // ck, ck, tq), BF16),
            pltpu.VMEM((seq // ck, N_HEADS, ck, tq), F32),
            pltpu.VMEM((N_HEADS, tq), F32),
            pltpu.VMEM((N_HEADS, VT_ROWS, tq), F32),
            pltpu.VMEM((N_HEADS, SUBLANES, tq), F32),
            pltpu.VMEM((1, tq), F32),
        ],
        compiler_params=pltpu.CompilerParams(
            dimension_semantics=("parallel", "arbitrary"),
            vmem_limit_bytes=VMEM_LIMIT),
        name="dsa_attention",
    )(q, qi, wi, kv, ki)


def _merge_kernel(att_ref, rnn_ref, ga_ref, gl_ref, x_ref, wa_ref, wr_ref,
                  wo_ref, o_ref, slab_scr, *, batch):
    att = _seq_to_rows(att_ref, slab_scr, batch).astype(BF16)
    pa = _dot(att, wa_ref[...])
    pr = _dot(rnn_ref[...], wr_ref[...])
    merged = (jax.nn.sigmoid(ga_ref[...]) * pa
              + jax.nn.sigmoid(gl_ref[...]) * pr)
    o_ref[...] = x_ref[...] + _dot(merged.astype(BF16), wo_ref[...])


def _merge(att, rnn, ga, gl, x, wa, wr, wo, layer, batch):
    rows, d = x.shape
    tt = ROW_TILE // batch

    def row_spec(c):
        return pl.BlockSpec((ROW_TILE, c), lambda i: (i, 0))

    return pl.pallas_call(
        functools.partial(_merge_kernel, batch=batch),
        out_shape=jax.ShapeDtypeStruct((rows, d), F32),
        grid=(rows // ROW_TILE,),
        in_specs=[pl.BlockSpec((tt, att.shape[1]), lambda i: (i, 0)),
                  row_spec(rnn.shape[1]), row_spec(d), row_spec(d),
                  row_spec(d), _resident(wa.shape[1:], layer),
                  _resident(wr.shape[1:], layer),
                  _resident(wo.shape[1:], layer)],
        out_specs=row_spec(d),
        scratch_shapes=[pltpu.VMEM((_slabs(D_ATT), ROW_TILE, LANES), F32)],
        compiler_params=pltpu.CompilerParams(
            dimension_semantics=("parallel",), vmem_limit_bytes=VMEM_LIMIT),
        name="merge",
    )(att, rnn, ga, gl, x, wa, wr, wo)


def _inproj_weights(w_in, d_rnn, d_model):
    splits = (D_ATT, HEAD_DIM, HEAD_DIM, IDX_HEADS * IDX_DIM, IDX_DIM,
              IDX_HEADS)
    parts, c = [], 0
    for n in splits:
        parts.append(w_in[..., c:c + n])
        c += n
    assert c + 2 * d_rnn + 2 * d_model == w_in.shape[-1]
    wq, wk, wv, wqi, wki, wwi = parts
    wi_pad = jnp.zeros(w_in.shape[:-1] + (LANES - IDX_HEADS,), w_in.dtype)
    cols = [wq, wk, wv, wv, wk, wqi] + [wki] * (LANES // IDX_DIM)
    cols += [wwi, wi_pad]
    return (jnp.concatenate(cols, axis=-1).astype(BF16),
            w_in[..., c:].astype(BF16))


def _block_diag_tiles(w):
    depth, n_blocks, bw, _ = w.shape
    per = MXU_DIM // bw
    w = w.reshape(depth, n_blocks // per, per, bw, bw)
    eye = jnp.eye(per, dtype=w.dtype)
    t = w[:, :, :, :, None, :] * eye[None, None, :, None, :, None]
    return t.reshape(depth, n_blocks // per, MXU_DIM, MXU_DIM).astype(BF16)


def kernel(x, ffn1_norm, ffn1_wg, ffn1_wu, ffn1_wd, mix_norm, w_in, conv_w,
           conv_b, rg_wa, rg_ba, rg_wx, rg_bx, rg_lam, w_att_proj, w_rnn_proj,
           w_out, ffn2_norm, ffn2_wg, ffn2_wu, ffn2_wd, final_norm):
    batch, seq, d = x.shape
    depth = ffn1_norm.shape[0]
    d_rnn = conv_w.shape[-1]
    assert batch == SUBLANES and seq % Q_TILE == 0
    assert (batch * seq) % FFN_ROW_TILE == 0 and ROW_TILE % batch == 0

    def row(v):
        return v.reshape(1, -1)

    wg1, wu1, wd1 = (w.astype(BF16) for w in (ffn1_wg, ffn1_wu, ffn1_wd))
    wg2, wu2, wd2 = (w.astype(BF16) for w in (ffn2_wg, ffn2_wu, ffn2_wd))
    w_att, w_wide = _inproj_weights(w_in, d_rnn, d)
    wa_t, wx_t = _block_diag_tiles(rg_wa), _block_diag_tiles(rg_wx)
    w_ap, w_rp, w_o = (w.astype(BF16) for w in (w_att_proj, w_rnn_proj, w_out))

    h = x
    for l in range(depth):
        h = _ffn(h, row(ffn1_norm[l]), wg1, wu1, wd1, l, batch, first=(l == 0))
        q, kv, qi, ki, wi, ga, gl, rnn = _mixer_in(
            h, row(mix_norm[l]), w_att, w_wide, conv_w[l], row(conv_b[l]), wa_t,
            row(rg_ba[l]), wx_t, row(rg_bx[l]), row(rg_lam[l]), l, batch)
        att = _attention(q, qi, wi, kv, ki, batch)
        h = _merge(att, rnn, ga, gl, h, w_ap, w_rp, w_o, l, batch)
        last = l == depth - 1
        h = _ffn(h, row(ffn2_norm[l]), wg2, wu2, wd2, l, batch,
                 final_g=row(final_norm) if last else None)
    return h
```

```python
import functools

import jax
import jax.numpy as jnp
from jax import lax
from jax.experimental import pallas as pl
from jax.experimental.pallas import tpu as pltpu

F32 = jnp.float32
BF16 = jnp.bfloat16
I32 = jnp.int32

N_HEADS = 8
HEAD_DIM = 64
D_ATT = N_HEADS * HEAD_DIM
IDX_HEADS = 8
IDX_DIM = 32
TOPK_MAX = 256
RNN_BLOCKS = 16
CONV_W = 4
LRU_C = 8.0
EPS = 1e-6

LANES = 128
SUBLANES = 8
BF16_SUBLANES = 16
MXU_DIM = 256
VMEM_LIMIT = 56 * 1024 * 1024

INT_MIN = -2 ** 31
MASK_BIAS = -1e30

ROW_TILE = 512
FFN_ROW_TILE = 1024
Q_TILE = 256
K_CHUNK = 256


def _resident(shape, layer=None):
    zeros = (0,) * len(shape)
    if layer is None:
        return pl.BlockSpec(shape, lambda *_: zeros,
                            pipeline_mode=pl.Buffered(1))
    return pl.BlockSpec((None,) + tuple(shape), lambda *_: (layer,) + zeros,
                        pipeline_mode=pl.Buffered(1))


def _rmsnorm(x, g):
    ms = jnp.mean(x * x, axis=-1, keepdims=True)
    return x * lax.rsqrt(ms + EPS) * g


def _dot(a, b):
    return jnp.dot(a, b, preferred_element_type=F32)


def _dot_nt(a, b):
    return lax.dot_general(a, b, (((1,), (1,)), ((), ())),
                           preferred_element_type=F32)


def _ff_chunks(d_ff):
    chunks, c0 = [], 0
    while c0 < d_ff:
        c1 = min(c0 + 2 * MXU_DIM, d_ff)
        chunks.append((c0, c1))
        c0 = c1
    return chunks


def _slabs(width):
    return width // LANES


def _batch_to_rows(x_ref, slab_scr, batch):
    tt, d = x_ref.shape[1:]
    for b in range(batch):
        for j in range(_slabs(d)):
            slab_scr[j, pl.ds(b, tt, stride=batch), :] = (
                x_ref[b, :, j * LANES:(j + 1) * LANES])
    return jnp.concatenate([slab_scr[j] for j in range(_slabs(d))], axis=1)


def _rows_to_batch(o_ref, y, slab_scr, batch):
    tt, d = o_ref.shape[1:]
    for j in range(_slabs(d)):
        slab_scr[j] = y[:, j * LANES:(j + 1) * LANES]
    for b in range(batch):
        for j in range(_slabs(d)):
            o_ref[b, :, j * LANES:(j + 1) * LANES] = (
                slab_scr[j, pl.ds(b, tt, stride=batch), :])


def _rows_to_seq(o_ref, z, slab_scr, slab0, batch):
    tt = o_ref.shape[0]
    c = z.shape[1]
    for j in range(_slabs(c)):
        slab_scr[slab0 + j] = z[:, j * LANES:(j + 1) * LANES]
    for b in range(batch):
        for j in range(_slabs(c)):
            o_ref[:, b * c + j * LANES:b * c + (j + 1) * LANES] = (
                slab_scr[slab0 + j, pl.ds(b, tt, stride=batch), :]
                .astype(o_ref.dtype))


def _seq_to_rows(x_ref, slab_scr, batch):
    tt = x_ref.shape[0]
    c = x_ref.shape[1] // batch
    for b in range(batch):
        for j in range(_slabs(c)):
            slab_scr[j, pl.ds(b, tt, stride=batch), :] = (
                x_ref[:, b * c + j * LANES:b * c + (j + 1) * LANES].astype(F32))
    return jnp.concatenate([slab_scr[j] for j in range(_slabs(c))], axis=1)


def _ffn_kernel(x_ref, g_ref, wg_ref, wu_ref, wd_ref, *rest, first, final,
                batch):
    rest = list(rest)
    fg_ref = rest.pop(0) if final else None
    o_ref = rest.pop(0)
    slab_scr = rest.pop(0) if (first or final) else None
    x = _batch_to_rows(x_ref, slab_scr, batch) if first else x_ref[...]
    h = _rmsnorm(x, g_ref[...]).astype(BF16)
    acc = None
    for c0, c1 in _ff_chunks(wg_ref.shape[1]):
        g = _dot(h, wg_ref[:, c0:c1])
        u = _dot(h, wu_ref[:, c0:c1])
        a = (g * jax.nn.sigmoid(g) * u).astype(BF16)
        d = _dot(a, wd_ref[c0:c1, :])
        acc = d if acc is None else acc + d
    y = x + 0.5 * acc
    if final:
        _rows_to_batch(o_ref, _rmsnorm(y, fg_ref[...]), slab_scr, batch)
    else:
        o_ref[...] = y


def _ffn(x, g, wg, wu, wd, layer, batch, first=False, final_g=None):
    final = final_g is not None
    if first:
        _, seq, d = x.shape
        rows = seq * batch
    else:
        rows, d = x.shape
        seq = rows // batch
    d_ff = wg.shape[2]
    tt = FFN_ROW_TILE // batch
    row_spec = pl.BlockSpec((FFN_ROW_TILE, d), lambda i: (i, 0))
    batch_spec = pl.BlockSpec((batch, tt, d), lambda i: (0, i, 0))
    in_specs = [batch_spec if first else row_spec, _resident((1, d)),
                _resident((d, d_ff), layer), _resident((d, d_ff), layer),
                _resident((d_ff, d), layer)]
    args = [x, g, wg, wu, wd]
    if final:
        in_specs.append(_resident((1, d)))
        args.append(final_g)
    scratch = ([pltpu.VMEM((_slabs(d), FFN_ROW_TILE, LANES), F32)]
               if (first or final) else [])
    out_shape = (batch, seq, d) if final else (rows, d)
    return pl.pallas_call(
        functools.partial(_ffn_kernel, first=first, final=final, batch=batch),
        out_shape=jax.ShapeDtypeStruct(out_shape, F32),
        grid=(rows // FFN_ROW_TILE,),
        in_specs=in_specs,
        out_specs=batch_spec if final else row_spec,
        scratch_shapes=scratch,
        compiler_params=pltpu.CompilerParams(
            dimension_semantics=("parallel",), vmem_limit_bytes=VMEM_LIMIT),
        name="ffn_first" if first else ("ffn_final" if final else "ffn"),
    )(*args)


ATT_WIDTHS = (D_ATT, 4 * HEAD_DIM, IDX_HEADS * IDX_DIM, LANES, LANES)
ATT_DTYPES = (BF16, BF16, BF16, BF16, F32)


def _mixer_in_kernel(x_ref, g_ref, w_att_ref, w_ref, cw_ref, cb_ref, wa_ref,
                     ba_ref,
                     wx_ref, bx_ref, lam_ref,
                     q_ref, kv_ref, qi_ref, ki_ref, wi_ref, ga_ref, gl_ref,
                     rnn_ref, slab_scr, xs_scr, a_scr, u_scr, h_scr, hc_scr,
                     *, batch):
    tm = x_ref.shape[0]
    c = cw_ref.shape[1]
    d_model = ga_ref.shape[1]
    halo = (CONV_W - 1) * batch
    h = _rmsnorm(x_ref[...], g_ref[...]).astype(BF16)

    @pl.when(pl.program_id(0) == 0)
    def _():
        xs_scr[0:halo, :] = jnp.zeros((halo, c), F32)
        hc_scr[...] = jnp.zeros(hc_scr.shape, F32)

    scales = (HEAD_DIM ** -0.5, None, None, None, (IDX_HEADS * IDX_DIM) ** -0.5)
    att_refs = (q_ref, kv_ref, qi_ref, ki_ref, wi_ref)
    gate_col0 = 2 * c

    def att_operand(i):
        col = sum(ATT_WIDTHS[:i])
        z = _dot(h, w_att_ref[:, col:col + ATT_WIDTHS[i]])
        if scales[i] is not None:
            z = z * scales[i]
        _rows_to_seq(att_refs[i], z, slab_scr, _slabs(sum(ATT_WIDTHS[:i])),
                     batch)

    def gate_logits(o_ref):
        col = gate_col0 if o_ref is ga_ref else gate_col0 + d_model
        o_ref[...] = _dot(h, w_ref[:, col:col + d_model])

    fillers = (lambda: gate_logits(ga_ref),
               lambda: gate_logits(gl_ref),
               lambda: (att_operand(0), att_operand(1)),
               lambda: (att_operand(2), att_operand(3), att_operand(4)))

    n_tiles = c // MXU_DIM
    assert n_tiles == len(fillers)
    for n in range(n_tiles):
        cs = slice(n * MXU_DIM, (n + 1) * MXU_DIM)
        xr = _dot(h, w_ref[:, cs])
        gr = _dot(h, w_ref[:, c + n * MXU_DIM:c + (n + 1) * MXU_DIM])
        xs_scr[halo:halo + tm, cs] = xr
        xc = cb_ref[:, cs]
        for j in range(CONV_W):
            xc = xc + cw_ref[j:j + 1, cs] * xs_scr[j * batch:j * batch + tm, cs]
        xs_scr[0:halo, cs] = xr[tm - halo:tm, :]

        xb = xc.astype(BF16)
        r = jax.nn.sigmoid(_dot(xb, wa_ref[n]) + ba_ref[:, cs])
        gate_i = jax.nn.sigmoid(_dot(xb, wx_ref[n]) + bx_ref[:, cs])
        fillers[n]()
        neg_lam = -lam_ref[:, cs]
        softplus = (jnp.maximum(neg_lam, 0.0)
                    + jnp.log1p(jnp.exp(-jnp.abs(neg_lam))))
        log_a = -LRU_C * r * softplus
        a_scr[:, cs] = jnp.exp(log_a)
        th = jnp.tanh(log_a)
        u_scr[:, cs] = jnp.sqrt(-2.0 * th / (1.0 - th)) * gate_i * xc

        hcur = hc_scr[:, cs]
        for t in range(tm // batch):
            rs = slice(t * batch, (t + 1) * batch)
            hcur = a_scr[rs, cs] * hcur + u_scr[rs, cs]
            h_scr[rs, cs] = hcur
        hc_scr[:, cs] = hcur
        rnn_ref[:, cs] = (h_scr[:, cs] * jax.nn.gelu(gr)).astype(BF16)


def _mixer_in(x, g, w_att, w, cw, cb, wa, ba, wx, bx, lam, layer, batch):
    rows, d = x.shape
    c = cw.shape[1]
    tm = ROW_TILE
    tt = tm // batch
    seq = rows // batch
    halo = (CONV_W - 1) * batch
    assert w.shape[2] == 2 * c + 2 * d and w_att.shape[2] == sum(ATT_WIDTHS)

    def row_spec(n):
        return pl.BlockSpec((tm, n), lambda i: (i, 0))

    def seq_spec(n):
        return pl.BlockSpec((tt, batch * n), lambda i: (i, 0))

    out_shape = [jax.ShapeDtypeStruct((seq, batch * n), t)
                 for n, t in zip(ATT_WIDTHS, ATT_DTYPES)]
    out_shape += [jax.ShapeDtypeStruct((rows, d), F32)] * 2
    out_shape += [jax.ShapeDtypeStruct((rows, c), BF16)]
    out_specs = [seq_spec(n) for n in ATT_WIDTHS]
    out_specs += [row_spec(d), row_spec(d), row_spec(c)]
    return pl.pallas_call(
        functools.partial(_mixer_in_kernel, batch=batch),
        out_shape=out_shape,
        grid=(rows // tm,),
        in_specs=[row_spec(d), _resident((1, d)),
                  _resident(w_att.shape[1:], layer),
                  _resident(w.shape[1:], layer), _resident(cw.shape),
                  _resident((1, c)), _resident(wa.shape[1:], layer),
                  _resident((1, c)), _resident(wx.shape[1:], layer),
                  _resident((1, c)), _resident((1, c))],
        out_specs=out_specs,
        scratch_shapes=[
            pltpu.VMEM((_slabs(sum(ATT_WIDTHS)), tm, LANES), F32),
            pltpu.VMEM((halo + tm, c), F32),
            pltpu.VMEM((tm, c), F32), pltpu.VMEM((tm, c), F32),
            pltpu.VMEM((tm, c), F32), pltpu.VMEM((batch, c), F32)],
        compiler_params=pltpu.CompilerParams(
            dimension_semantics=("arbitrary",), vmem_limit_bytes=VMEM_LIMIT),
        name="mixer_in",
    )(x, g, w_att, w, cw, cb, wa, ba, wx, bx, lam)


VT_ROWS = HEAD_DIM + 16


def _key_to_float(key):
    bits = jnp.where(key < 0, INT_MIN - key, key)
    return pltpu.bitcast(bits, F32)


def _for_chunks(n, body):
    def quad(i, carry):
        for j in range(4):
            body(4 * i + j)
        return carry

    lax.fori_loop(0, n // 4, quad, 0)
    base = (n // 4) * 4

    @pl.when(n % 4 >= 2)
    def _():
        body(base)
        body(base + 1)

    @pl.when(n % 2 == 1)
    def _():
        body(n - 1)


def _fold_chunks(n, chunk, init):
    def pair(i, c):
        return chunk(2 * i + 1, chunk(2 * i, c))

    c = lax.fori_loop(0, n // 2, pair, init)
    return lax.cond(n % 2 == 1, lambda c: chunk(n - 1, c), lambda c: c, c)


def _attn_kernel(q_ref, qi_ref, wi_ref, kv_ref, ki_ref, o_ref,
                 qm_scr, qim_scr, wit_scr, vt_scr, sc_scr, scb_scr, sall_scr, m_scr,
                 acc_scr, mpart_scr, ties_scr, *, topk):
    tq, ck = Q_TILE, K_CHUNK
    n_kc = sc_scr.shape[0]
    seq = n_kc * ck
    qt = pl.program_id(1)
    n_chunks = (qt * tq) // ck + tq // ck
    lane = lax.broadcasted_iota(I32, (tq, LANES), 1)
    key_iota = lax.broadcasted_iota(I32, (ck, tq), 0)
    qry_iota = lax.broadcasted_iota(I32, (ck, tq), 1)
    kf = float(topk)

    for h in range(N_HEADS):
        qp = q_ref[:, (h // 2) * LANES:(h // 2 + 1) * LANES].astype(F32)
        lo = (h % 2) * HEAD_DIM
        qm_scr[h] = jnp.where((lane >= lo) & (lane < lo + HEAD_DIM), qp,
                              0.0).astype(BF16)
    for h in range(IDX_HEADS):
        qp = qi_ref[:, (h // 4) * LANES:(h // 4 + 1) * LANES].astype(F32)
        lo = (h % 4) * IDX_DIM
        qim_scr[h] = jnp.where((lane >= lo) & (lane < lo + IDX_DIM), qp,
                               0.0).astype(BF16)
    wit_scr[...] = jnp.transpose(wi_ref[...])

    @pl.when(qt == 0)
    def _():
        lane_c = lax.broadcasted_iota(I32, (ck, LANES), 1)
        for c in range(n_kc):
            v1 = jnp.where(lane_c < HEAD_DIM,
                           kv_ref[c * ck:(c + 1) * ck, LANES:2 * LANES].astype(F32),
                           1.0)
            vt_scr[c] = jnp.transpose(v1)[0:VT_ROWS, :].astype(BF16)

    def score_chunk(kc):
        k0 = pl.multiple_of(kc * ck, ck)
        kic = ki_ref[pl.ds(k0, ck), :]
        acc = jnp.zeros((ck, tq), F32)
        for h in range(IDX_HEADS):
            acc = acc + (jnp.maximum(_dot_nt(kic, qim_scr[h]), 0.0)
                         * wit_scr[h:h + 1, :])
        causal = key_iota - qry_iota <= qt * tq - kc * ck
        sc = jnp.where(causal, acc, -jnp.inf)
        sc_scr[kc] = sc
        scb_scr[kc] = sc.astype(BF16)

    _for_chunks(n_chunks, score_chunk)

    def count(pred):
        acc_rows = 2 * SUBLANES

        def chunk(kc, c):
            hit = jnp.where(pred(sc_scr[kc], kc), 1.0, 0.0)
            return c + jnp.sum(hit.reshape(ck // acc_rows, acc_rows, tq), axis=0)

        c = _fold_chunks(n_chunks, chunk, jnp.zeros((acc_rows, tq), F32))
        return jnp.sum(c, axis=0, keepdims=True)

    def count_bf16(cand):
        acc_rows = 2 * BF16_SUBLANES
        one, zero = jnp.ones((), BF16), jnp.zeros((), BF16)

        def chunk(kc, c):
            hit = jnp.where(scb_scr[kc] >= cand, one, zero)
            part = hit[0:acc_rows]
            for r0 in range(acc_rows, ck, acc_rows):
                part = part + hit[r0:r0 + acc_rows]
            return c + part.astype(F32)

        c = _fold_chunks(n_chunks, chunk, jnp.zeros((acc_rows, tq), F32))
        return jnp.sum(c, axis=0, keepdims=True)

    def coarse_step(i, thr16):
        cand16 = thr16 + lax.shift_left(jnp.int32(1), 15 - i)
        cand_f = _key_to_float(cand16 * 65536)
        tot = count_bf16(cand_f.astype(BF16))
        return jnp.where(tot >= kf, cand16, thr16)

    thr16 = lax.fori_loop(0, 16, coarse_step,
                          jnp.full((1, tq), -2 ** 15, I32))

    def fine_step(i, carry):
        thr, above = carry
        cand = thr + lax.shift_left(jnp.int32(1), 16 - i)
        cand_f = _key_to_float(cand)
        tot = count(lambda s, kc: s >= cand_f)
        ok = tot >= kf
        return jnp.where(ok, cand, thr), jnp.where(ok, above, tot)

    thr, above = lax.fori_loop(
        0, 17, fine_step,
        ((thr16 - 1) * 65536, jnp.full((1, tq), -1.0, F32)))

    thr_f = _key_to_float(thr)
    short = jnp.logical_not(thr_f > -jnp.inf)
    thr_f = jnp.where(short, -jnp.inf, thr_f)
    n_gt = lax.cond(jnp.min(above) < 0.0,
                    lambda: count(lambda s, kc: s > thr_f), lambda: above)
    need = jnp.where(short, 0.0, kf - n_gt)
    lower_tri = jnp.where(
        lax.broadcasted_iota(I32, (ck, ck), 0)
        >= lax.broadcasted_iota(I32, (ck, ck), 1), 1.0, 0.0).astype(BF16)

    mpart_scr[...] = jnp.full(mpart_scr.shape, MASK_BIAS, F32)
    acc_scr[...] = jnp.zeros(acc_scr.shape, F32)

    ties_scr[...] = jnp.zeros(ties_scr.shape, F32)

    def scores_chunk(kc):
        k0 = pl.multiple_of(kc * ck, ck)
        s_idx = sc_scr[kc]
        tie = s_idx == thr_f
        tie_f = jnp.where(tie, 1.0, 0.0)
        ties_before = ties_scr[...]
        ties_scr[...] = ties_before + jnp.sum(tie_f, axis=0, keepdims=True)
        rank = ties_before + _dot(lower_tri, tie_f.astype(BF16))
        keep = (s_idx > thr_f) | (tie & (rank <= need))
        bias = jnp.where(keep, 0.0, MASK_BIAS)
        for h in range(N_HEADS):
            c0 = 0 if h % 2 == 0 else LANES
            k_mat = kv_ref[pl.ds(k0, ck), c0:c0 + LANES]
            s = _dot_nt(k_mat, qm_scr[h]) + bias
            sall_scr[kc, h] = s
            mpart_scr[h] = jnp.maximum(
                mpart_scr[h],
                jnp.max(s.reshape(ck // SUBLANES, SUBLANES, tq), axis=0))

    _for_chunks(n_chunks, scores_chunk)
    for h in range(N_HEADS):
        m_scr[h:h + 1, :] = jnp.max(mpart_scr[h], axis=0, keepdims=True)

    def value_chunk(kc):
        vt_c = vt_scr[kc]
        for h in range(N_HEADS):
            p = jnp.exp(sall_scr[kc, h] - m_scr[h:h + 1, :]).astype(BF16)
            acc_scr[h] = acc_scr[h] + _dot(vt_c, p)

    _for_chunks(n_chunks, value_chunk)

    outs = []
    for h in range(N_HEADS):
        a = acc_scr[h]
        outs.append(a[0:HEAD_DIM, :] / a[HEAD_DIM:HEAD_DIM + 1, :])
    o_ref[...] = jnp.transpose(jnp.concatenate(outs, axis=0)).astype(BF16)


def _attention(q, qi, wi, kv, ki, batch):
    tq, ck = Q_TILE, K_CHUNK
    seq = q.shape[0]
    topk = min(TOPK_MAX, seq // 4)
    nq = seq // tq

    def q_spec(c):
        return pl.BlockSpec((tq, c), lambda b, i: (i, b))

    def kv_spec(c):
        return pl.BlockSpec((seq, c), lambda b, i: (0, b))

    return pl.pallas_call(
        functools.partial(_attn_kernel, topk=topk),
        out_shape=jax.ShapeDtypeStruct((seq, batch * D_ATT), BF16),
        grid=(batch, nq),
        in_specs=[q_spec(D_ATT), q_spec(IDX_HEADS * IDX_DIM), q_spec(LANES),
                  kv_spec(4 * HEAD_DIM), kv_spec(LANES)],
        out_specs=q_spec(D_ATT),
        scratch_shapes=[
            pltpu.VMEM((N_HEADS, tq, LANES), BF16),
            pltpu.VMEM((IDX_HEADS, tq, LANES), BF16),
            pltpu.VMEM((LANES, tq), F32),
            pltpu.VMEM((seq // ck, VT_ROWS, ck), BF16),
            pltpu.VMEM((seq // ck, ck, tq), F32),
            pltpu.VMEM((seq // ck, ck, tq), BF16),
            pltpu.VMEM((seq // ck, N_HEADS, ck, tq), F32),
            pltpu.VMEM((N_HEADS, tq), F32),
            pltpu.VMEM((N_HEADS, VT_ROWS, tq), F32),
            pltpu.VMEM((N_HEADS, SUBLANES, tq), F32),
            pltpu.VMEM((1, tq), F32),
        ],
        compiler_params=pltpu.CompilerParams(
            dimension_semantics=("parallel", "arbitrary"),
            vmem_limit_bytes=VMEM_LIMIT),
        name="dsa_attention",
    )(q, qi, wi, kv, ki)


def _merge_kernel(att_ref, rnn_ref, ga_ref, gl_ref, x_ref, wa_ref, wr_ref,
                  wo_ref, o_ref, slab_scr, *, batch):
    att = _seq_to_rows(att_ref, slab_scr, batch).astype(BF16)
    pa = _dot(att, wa_ref[...])
    pr = _dot(rnn_ref[...], wr_ref[...])
    merged = (jax.nn.sigmoid(ga_ref[...]) * pa
              + jax.nn.sigmoid(gl_ref[...]) * pr)
    o_ref[...] = x_ref[...] + _dot(merged.astype(BF16), wo_ref[...])


def _merge(att, rnn, ga, gl, x, wa, wr, wo, layer, batch):
    rows, d = x.shape
    tt = ROW_TILE // batch

    def row_spec(c):
        return pl.BlockSpec((ROW_TILE, c), lambda i: (i, 0))

    return pl.pallas_call(
        functools.partial(_merge_kernel, batch=batch),
        out_shape=jax.ShapeDtypeStruct((rows, d), F32),
        grid=(rows // ROW_TILE,),
        in_specs=[pl.BlockSpec((tt, att.shape[1]), lambda i: (i, 0)),
                  row_spec(rnn.shape[1]), row_spec(d), row_spec(d),
                  row_spec(d), _resident(wa.shape[1:], layer),
                  _resident(wr.shape[1:], layer),
                  _resident(wo.shape[1:], layer)],
        out_specs=row_spec(d),
        scratch_shapes=[pltpu.VMEM((_slabs(D_ATT), ROW_TILE, LANES), F32)],
        compiler_params=pltpu.CompilerParams(
            dimension_semantics=("parallel",), vmem_limit_bytes=VMEM_LIMIT),
        name="merge",
    )(att, rnn, ga, gl, x, wa, wr, wo)


def _inproj_weights(w_in, d_rnn, d_model):
    splits = (D_ATT, HEAD_DIM, HEAD_DIM, IDX_HEADS * IDX_DIM, IDX_DIM,
              IDX_HEADS)
    parts, c = [], 0
    for n in splits:
        parts.append(w_in[..., c:c + n])
        c += n
    assert c + 2 * d_rnn + 2 * d_model == w_in.shape[-1]
    wq, wk, wv, wqi, wki, wwi = parts
    wi_pad = jnp.zeros(w_in.shape[:-1] + (LANES - IDX_HEADS,), w_in.dtype)
    cols = [wq, wk, wv, wv, wk, wqi] + [wki] * (LANES // IDX_DIM)
    cols += [wwi, wi_pad]
    return (jnp.concatenate(cols, axis=-1).astype(BF16),
            w_in[..., c:].astype(BF16))


def _block_diag_tiles(w):
    depth, n_blocks, bw, _ = w.shape
    per = MXU_DIM // bw
    w = w.reshape(depth, n_blocks // per, per, bw, bw)
    eye = jnp.eye(per, dtype=w.dtype)
    t = w[:, :, :, :, None, :] * eye[None, None, :, None, :, None]
    return t.reshape(depth, n_blocks // per, MXU_DIM, MXU_DIM).astype(BF16)


def kernel(x, ffn1_norm, ffn1_wg, ffn1_wu, ffn1_wd, mix_norm, w_in, conv_w,
           conv_b, rg_wa, rg_ba, rg_wx, rg_bx, rg_lam, w_att_proj, w_rnn_proj,
           w_out, ffn2_norm, ffn2_wg, ffn2_wu, ffn2_wd, final_norm):
    batch, seq, d = x.shape
    depth = ffn1_norm.shape[0]
    d_rnn = conv_w.shape[-1]
    assert batch == SUBLANES and seq % Q_TILE == 0
    assert (batch * seq) % FFN_ROW_TILE == 0 and ROW_TILE % batch == 0

    def row(v):
        return v.reshape(1, -1)

    wg1, wu1, wd1 = (w.astype(BF16) for w in (ffn1_wg, ffn1_wu, ffn1_wd))
    wg2, wu2, wd2 = (w.astype(BF16) for w in (ffn2_wg, ffn2_wu, ffn2_wd))
    w_att, w_wide = _inproj_weights(w_in, d_rnn, d)
    wa_t, wx_t = _block_diag_tiles(rg_wa), _block_diag_tiles(rg_wx)
    w_ap, w_rp, w_o = (w.astype(BF16) for w in (w_att_proj, w_rnn_proj, w_out))

    h = x
    for l in range(depth):
        h = _ffn(h, row(ffn1_norm[l]), wg1, wu1, wd1, l, batch, first=(l == 0))
        q, kv, qi, ki, wi, ga, gl, rnn = _mixer_in(
            h, row(mix_norm[l]), w_att, w_wide, conv_w[l], row(conv_b[l]), wa_t,
            row(rg_ba[l]), wx_t, row(rg_bx[l]), row(rg_lam[l]), l, batch)
        att = _attention(q, qi, wi, kv, ki, batch)
        h = _merge(att, rnn, ga, gl, h, w_ap, w_rp, w_o, l, batch)
        last = l == depth - 1
        h = _ffn(h, row(ffn2_norm[l]), wg2, wu2, wd2, l, batch,
                 final_g=row(final_norm) if last else None)
    return h
```

```python
import functools

import jax
import jax.numpy as jnp
from jax import lax
from jax.experimental import pallas as pl
from jax.experimental.pallas import tpu as pltpu

F32 = jnp.float32
BF16 = jnp.bfloat16
I32 = jnp.int32

N_HEADS = 8
HEAD_DIM = 64
D_ATT = N_HEADS * HEAD_DIM
IDX_HEADS = 8
IDX_DIM = 32
TOPK_MAX = 256
RNN_BLOCKS = 16
CONV_W = 4
LRU_C = 8.0
EPS = 1e-6

LANES = 128
SUBLANES = 8
BF16_SUBLANES = 16
MXU_DIM = 256
VMEM_LIMIT = 56 * 1024 * 1024

INT_MIN = -2 ** 31
MASK_BIAS = -1e30

ROW_TILE = 512
FFN_ROW_TILE = 1024
Q_TILE = 256
K_CHUNK = 256


def _resident(shape, layer=None):
    zeros = (0,) * len(shape)
    if layer is None:
        return pl.BlockSpec(shape, lambda *_: zeros,
                            pipeline_mode=pl.Buffered(1))
    return pl.BlockSpec((None,) + tuple(shape), lambda *_: (layer,) + zeros,
                        pipeline_mode=pl.Buffered(1))


def _rmsnorm(x, g):
    ms = jnp.mean(x * x, axis=-1, keepdims=True)
    return x * lax.rsqrt(ms + EPS) * g


def _dot(a, b):
    return jnp.dot(a, b, preferred_element_type=F32)


def _dot_nt(a, b):
    return lax.dot_general(a, b, (((1,), (1,)), ((), ())),
                           preferred_element_type=F32)


def _ff_chunks(d_ff):
    chunks, c0 = [], 0
    while c0 < d_ff:
        c1 = min(c0 + 2 * MXU_DIM, d_ff)
        chunks.append((c0, c1))
        c0 = c1
    return chunks


def _slabs(width):
    return width // LANES


def _batch_to_rows(x_ref, slab_scr, batch):
    tt, d = x_ref.shape[1:]
    for b in range(batch):
        for j in range(_slabs(d)):
            slab_scr[j, pl.ds(b, tt, stride=batch), :] = (
                x_ref[b, :, j * LANES:(j + 1) * LANES])
    return jnp.concatenate([slab_scr[j] for j in range(_slabs(d))], axis=1)


def _rows_to_batch(o_ref, y, slab_scr, batch):
    tt, d = o_ref.shape[1:]
    for j in range(_slabs(d)):
        slab_scr[j] = y[:, j * LANES:(j + 1) * LANES]
    for b in range(batch):
        for j in range(_slabs(d)):
            o_ref[b, :, j * LANES:(j + 1) * LANES] = (
                slab_scr[j, pl.ds(b, tt, stride=batch), :])


def _rows_to_seq(o_ref, z, slab_scr, slab0, batch):
    tt = o_ref.shape[0]
    c = z.shape[1]
    for j in range(_slabs(c)):
        slab_scr[slab0 + j] = z[:, j * LANES:(j + 1) * LANES]
    for b in range(batch):
        for j in range(_slabs(c)):
            o_ref[:, b * c + j * LANES:b * c + (j + 1) * LANES] = (
                slab_scr[slab0 + j, pl.ds(b, tt, stride=batch), :]
                .astype(o_ref.dtype))


def _seq_to_rows(x_ref, slab_scr, batch):
    tt = x_ref.shape[0]
    c = x_ref.shape[1] // batch
    for b in range(batch):
        for j in range(_slabs(c)):
            slab_scr[j, pl.ds(b, tt, stride=batch), :] = (
                x_ref[:, b * c + j * LANES:b * c + (j + 1) * LANES].astype(F32))
    return jnp.concatenate([slab_scr[j] for j in range(_slabs(c))], axis=1)


def _ffn_kernel(x_ref, g_ref, wg_ref, wu_ref, wd_ref, *rest, first, final,
                batch):
    rest = list(rest)
    fg_ref = rest.pop(0) if final else None
    o_ref = rest.pop(0)
    slab_scr = rest.pop(0) if (first or final) else None
    x = _batch_to_rows(x_ref, slab_scr, batch) if first else x_ref[...]
    h = _rmsnorm(x, g_ref[...]).astype(BF16)
    acc = None
    for c0, c1 in _ff_chunks(wg_ref.shape[1]):
        g = _dot(h, wg_ref[:, c0:c1])
        u = _dot(h, wu_ref[:, c0:c1])
        a = (g * jax.nn.sigmoid(g) * u).astype(BF16)
        d = _dot(a, wd_ref[c0:c1, :])
        acc = d if acc is None else acc + d
    y = x + 0.5 * acc
    if final:
        _rows_to_batch(o_ref, _rmsnorm(y, fg_ref[...]), slab_scr, batch)
    else:
        o_ref[...] = y


def _ffn(x, g, wg, wu, wd, layer, batch, first=False, final_g=None):
    final = final_g is not None
    if first:
        _, seq, d = x.shape
        rows = seq * batch
    else:
        rows, d = x.shape
        seq = rows // batch
    d_ff = wg.shape[2]
    tt = FFN_ROW_TILE // batch
    row_spec = pl.BlockSpec((FFN_ROW_TILE, d), lambda i: (i, 0))
    batch_spec = pl.BlockSpec((batch, tt, d), lambda i: (0, i, 0))
    in_specs = [batch_spec if first else row_spec, _resident((1, d)),
                _resident((d, d_ff), layer), _resident((d, d_ff), layer),
                _resident((d_ff, d), layer)]
    args = [x, g, wg, wu, wd]
    if final:
        in_specs.append(_resident((1, d)))
        args.append(final_g)
    scratch = ([pltpu.VMEM((_slabs(d), FFN_ROW_TILE, LANES), F32)]
               if (first or final) else [])
    out_shape = (batch, seq, d) if final else (rows, d)
    return pl.pallas_call(
        functools.partial(_ffn_kernel, first=first, final=final, batch=batch),
        out_shape=jax.ShapeDtypeStruct(out_shape, F32),
        grid=(rows // FFN_ROW_TILE,),
        in_specs=in_specs,
        out_specs=batch_spec if final else row_spec,
        scratch_shapes=scratch,
        compiler_params=pltpu.CompilerParams(
            dimension_semantics=("parallel",), vmem_limit_bytes=VMEM_LIMIT),
        name="ffn_first" if first else ("ffn_final" if final else "ffn"),
    )(*args)


ATT_WIDTHS = (D_ATT, 4 * HEAD_DIM, IDX_HEADS * IDX_DIM, LANES, LANES)
ATT_DTYPES = (BF16, BF16, BF16, BF16, F32)


def _mixer_in_kernel(x_ref, g_ref, w_att_ref, w_ref, cw_ref, cb_ref, wa_ref,
                     ba_ref,
                     wx_ref, bx_ref, lam_ref,
                     q_ref, kv_ref, qi_ref, ki_ref, wi_ref, ga_ref, gl_ref,
                     rnn_ref, slab_scr, xs_scr, a_scr, u_scr, h_scr, hc_scr,
                     *, batch):
    tm = x_ref.shape[0]
    c = cw_ref.shape[1]
    d_model = ga_ref.shape[1]
    halo = (CONV_W - 1) * batch
    h = _rmsnorm(x_ref[...], g_ref[...]).astype(BF16)

    @pl.when(pl.program_id(0) == 0)
    def _():
        xs_scr[0:halo, :] = jnp.zeros((halo, c), F32)
        hc_scr[...] = jnp.zeros(hc_scr.shape, F32)

    scales = (HEAD_DIM ** -0.5, None, None, None, (IDX_HEADS * IDX_DIM) ** -0.5)
    att_refs = (q_ref, kv_ref, qi_ref, ki_ref, wi_ref)
    gate_col0 = 2 * c

    def att_operand(i):
        col = sum(ATT_WIDTHS[:i])
        z = _dot(h, w_att_ref[:, col:col + ATT_WIDTHS[i]])
        if scales[i] is not None:
            z = z * scales[i]
        _rows_to_seq(att_refs[i], z, slab_scr, _slabs(sum(ATT_WIDTHS[:i])),
                     batch)

    def gate_logits(o_ref):
        col = gate_col0 if o_ref is ga_ref else gate_col0 + d_model
        o_ref[...] = _dot(h, w_ref[:, col:col + d_model]).astype(o_ref.dtype)

    fillers = (lambda: gate_logits(ga_ref),
               lambda: gate_logits(gl_ref),
               lambda: (att_operand(0), att_operand(1)),
               lambda: (att_operand(2), att_operand(3), att_operand(4)))

    n_tiles = c // MXU_DIM
    assert n_tiles == len(fillers)
    for n in range(n_tiles):
        cs = slice(n * MXU_DIM, (n + 1) * MXU_DIM)
        xr = _dot(h, w_ref[:, cs])
        gr = _dot(h, w_ref[:, c + n * MXU_DIM:c + (n + 1) * MXU_DIM])
        xs_scr[halo:halo + tm, cs] = xr
        xc = cb_ref[:, cs]
        for j in range(CONV_W):
            xc = xc + cw_ref[j:j + 1, cs] * xs_scr[j * batch:j * batch + tm, cs]
        xs_scr[0:halo, cs] = xr[tm - halo:tm, :]

        xb = xc.astype(BF16)
        r = jax.nn.sigmoid(_dot(xb, wa_ref[n]) + ba_ref[:, cs])
        gate_i = jax.nn.sigmoid(_dot(xb, wx_ref[n]) + bx_ref[:, cs])
        fillers[n]()
        neg_lam = -lam_ref[:, cs]
        softplus = (jnp.maximum(neg_lam, 0.0)
                    + jnp.log1p(jnp.exp(-jnp.abs(neg_lam))))
        log_a = -LRU_C * r * softplus
        a_scr[:, cs] = jnp.exp(log_a)
        th = jnp.tanh(log_a)
        u_scr[:, cs] = jnp.sqrt(-2.0 * th / (1.0 - th)) * gate_i * xc

        hcur = hc_scr[:, cs]
        for t in range(tm // batch):
            rs = slice(t * batch, (t + 1) * batch)
            hcur = a_scr[rs, cs] * hcur + u_scr[rs, cs]
            h_scr[rs, cs] = hcur
        hc_scr[:, cs] = hcur
        rnn_ref[:, cs] = (h_scr[:, cs] * jax.nn.gelu(gr)).astype(BF16)


def _mixer_in(x, g, w_att, w, cw, cb, wa, ba, wx, bx, lam, layer, batch):
    rows, d = x.shape
    c = cw.shape[1]
    tm = ROW_TILE
    tt = tm // batch
    seq = rows // batch
    halo = (CONV_W - 1) * batch
    assert w.shape[2] == 2 * c + 2 * d and w_att.shape[2] == sum(ATT_WIDTHS)

    def row_spec(n):
        return pl.BlockSpec((tm, n), lambda i: (i, 0))

    def seq_spec(n):
        return pl.BlockSpec((tt, batch * n), lambda i: (i, 0))

    out_shape = [jax.ShapeDtypeStruct((seq, batch * n), t)
                 for n, t in zip(ATT_WIDTHS, ATT_DTYPES)]
    out_shape += [jax.ShapeDtypeStruct((rows, d), BF16)] * 2
    out_shape += [jax.ShapeDtypeStruct((rows, c), BF16)]
    out_specs = [seq_spec(n) for n in ATT_WIDTHS]
    out_specs += [row_spec(d), row_spec(d), row_spec(c)]
    return pl.pallas_call(
        functools.partial(_mixer_in_kernel, batch=batch),
        out_shape=out_shape,
        grid=(rows // tm,),
        in_specs=[row_spec(d), _resident((1, d)),
                  _resident(w_att.shape[1:], layer),
                  _resident(w.shape[1:], layer), _resident(cw.shape),
                  _resident((1, c)), _resident(wa.shape[1:], layer),
                  _resident((1, c)), _resident(wx.shape[1:], layer),
                  _resident((1, c)), _resident((1, c))],
        out_specs=out_specs,
        scratch_shapes=[
            pltpu.VMEM((_slabs(sum(ATT_WIDTHS)), tm, LANES), F32),
            pltpu.VMEM((halo + tm, c), F32),
            pltpu.VMEM((tm, c), F32), pltpu.VMEM((tm, c), F32),
            pltpu.VMEM((tm, c), F32), pltpu.VMEM((batch, c), F32)],
        compiler_params=pltpu.CompilerParams(
            dimension_semantics=("arbitrary",), vmem_limit_bytes=VMEM_LIMIT),
        name="mixer_in",
    )(x, g, w_att, w, cw, cb, wa, ba, wx, bx, lam)


VT_ROWS = HEAD_DIM + 16


def _key_to_float(key):
    bits = jnp.where(key < 0, INT_MIN - key, key)
    return pltpu.bitcast(bits, F32)


def _for_chunks(n, body):
    def quad(i, carry):
        for j in range(4):
            body(4 * i + j)
        return carry

    lax.fori_loop(0, n // 4, quad, 0)
    base = (n // 4) * 4

    @pl.when(n % 4 >= 2)
    def _():
        body(base)
        body(base + 1)

    @pl.when(n % 2 == 1)
    def _():
        body(n - 1)


def _fold_chunks(n, chunk, init):
    def pair(i, c):
        return chunk(2 * i + 1, chunk(2 * i, c))

    c = lax.fori_loop(0, n // 2, pair, init)
    return lax.cond(n % 2 == 1, lambda c: chunk(n - 1, c), lambda c: c, c)


def _attn_kernel(q_ref, qi_ref, wi_ref, kv_ref, ki_ref, o_ref,
                 qm_scr, qim_scr, wit_scr, vt_scr, sc_scr, scb_scr, sall_scr, m_scr,
                 acc_scr, mpart_scr, ties_scr, *, topk):
    tq, ck = Q_TILE, K_CHUNK
    n_kc = sc_scr.shape[0]
    seq = n_kc * ck
    qt = pl.program_id(1)
    n_chunks = (qt * tq) // ck + tq // ck
    lane = lax.broadcasted_iota(I32, (tq, LANES), 1)
    key_iota = lax.broadcasted_iota(I32, (ck, tq), 0)
    qry_iota = lax.broadcasted_iota(I32, (ck, tq), 1)
    kf = float(topk)

    for h in range(N_HEADS):
        qp = q_ref[:, (h // 2) * LANES:(h // 2 + 1) * LANES].astype(F32)
        lo = (h % 2) * HEAD_DIM
        qm_scr[h] = jnp.where((lane >= lo) & (lane < lo + HEAD_DIM), qp,
                              0.0).astype(BF16)
    for h in range(IDX_HEADS):
        qp = qi_ref[:, (h // 4) * LANES:(h // 4 + 1) * LANES].astype(F32)
        lo = (h % 4) * IDX_DIM
        qim_scr[h] = jnp.where((lane >= lo) & (lane < lo + IDX_DIM), qp,
                               0.0).astype(BF16)
    wit_scr[...] = jnp.transpose(wi_ref[...])

    @pl.when(qt == 0)
    def _():
        lane_c = lax.broadcasted_iota(I32, (ck, LANES), 1)
        for c in range(n_kc):
            v1 = jnp.where(lane_c < HEAD_DIM,
                           kv_ref[c * ck:(c + 1) * ck, LANES:2 * LANES].astype(F32),
                           1.0)
            vt_scr[c] = jnp.transpose(v1)[0:VT_ROWS, :].astype(BF16)

    def score_chunk(kc):
        k0 = pl.multiple_of(kc * ck, ck)
        kic = ki_ref[pl.ds(k0, ck), :]
        acc = jnp.zeros((ck, tq), F32)
        for h in range(IDX_HEADS):
            acc = acc + (jnp.maximum(_dot_nt(kic, qim_scr[h]), 0.0)
                         * wit_scr[h:h + 1, :])
        causal = key_iota - qry_iota <= qt * tq - kc * ck
        sc = jnp.where(causal, acc, -jnp.inf)
        sc_scr[kc] = sc
        scb_scr[kc] = sc.astype(BF16)

    _for_chunks(n_chunks, score_chunk)

    def count(pred):
        acc_rows = 2 * SUBLANES

        def chunk(kc, c):
            hit = jnp.where(pred(sc_scr[kc], kc), 1.0, 0.0)
            return c + jnp.sum(hit.reshape(ck // acc_rows, acc_rows, tq), axis=0)

        c = _fold_chunks(n_chunks, chunk, jnp.zeros((acc_rows, tq), F32))
        return jnp.sum(c, axis=0, keepdims=True)

    def count_bf16(cand):
        acc_rows = 2 * BF16_SUBLANES
        one, zero = jnp.ones((), BF16), jnp.zeros((), BF16)

        def chunk(kc, c):
            hit = jnp.where(scb_scr[kc] >= cand, one, zero)
            part = hit[0:acc_rows]
            for r0 in range(acc_rows, ck, acc_rows):
                part = part + hit[r0:r0 + acc_rows]
            return c + part.astype(F32)

        c = _fold_chunks(n_chunks, chunk, jnp.zeros((acc_rows, tq), F32))
        return jnp.sum(c, axis=0, keepdims=True)

    def coarse_step(i, thr16):
        cand16 = thr16 + lax.shift_left(jnp.int32(1), 15 - i)
        cand_f = _key_to_float(cand16 * 65536)
        tot = count_bf16(cand_f.astype(BF16))
        return jnp.where(tot >= kf, cand16, thr16)

    thr16 = lax.fori_loop(0, 16, coarse_step,
                          jnp.full((1, tq), -2 ** 15, I32))

    def fine_step(i, carry):
        thr, above = carry
        cand = thr + lax.shift_left(jnp.int32(1), 16 - i)
        cand_f = _key_to_float(cand)
        tot = count(lambda s, kc: s >= cand_f)
        ok = tot >= kf
        return jnp.where(ok, cand, thr), jnp.where(ok, above, tot)

    thr, above = lax.fori_loop(
        0, 17, fine_step,
        ((thr16 - 1) * 65536, jnp.full((1, tq), -1.0, F32)))

    thr_f = _key_to_float(thr)
    short = jnp.logical_not(thr_f > -jnp.inf)
    thr_f = jnp.where(short, -jnp.inf, thr_f)
    n_gt = lax.cond(jnp.min(above) < 0.0,
                    lambda: count(lambda s, kc: s > thr_f), lambda: above)
    need = jnp.where(short, 0.0, kf - n_gt)
    lower_tri = jnp.where(
        lax.broadcasted_iota(I32, (ck, ck), 0)
        >= lax.broadcasted_iota(I32, (ck, ck), 1), 1.0, 0.0).astype(BF16)

    mpart_scr[...] = jnp.full(mpart_scr.shape, MASK_BIAS, F32)
    acc_scr[...] = jnp.zeros(acc_scr.shape, F32)

    ties_scr[...] = jnp.zeros(ties_scr.shape, F32)

    def scores_chunk(kc):
        k0 = pl.multiple_of(kc * ck, ck)
        s_idx = sc_scr[kc]
        tie = s_idx == thr_f
        tie_f = jnp.where(tie, 1.0, 0.0)
        ties_before = ties_scr[...]
        ties_scr[...] = ties_before + jnp.sum(tie_f, axis=0, keepdims=True)
        rank = ties_before + _dot(lower_tri, tie_f.astype(BF16))
        keep = (s_idx > thr_f) | (tie & (rank <= need))
        bias = jnp.where(keep, 0.0, MASK_BIAS)
        for h in range(N_HEADS):
            c0 = 0 if h % 2 == 0 else LANES
            k_mat = kv_ref[pl.ds(k0, ck), c0:c0 + LANES]
            s = _dot_nt(k_mat, qm_scr[h]) + bias
            sall_scr[kc, h] = s
            mpart_scr[h] = jnp.maximum(
                mpart_scr[h],
                jnp.max(s.reshape(ck // SUBLANES, SUBLANES, tq), axis=0))

    _for_chunks(n_chunks, scores_chunk)
    for h in range(N_HEADS):
        m_scr[h:h + 1, :] = jnp.max(mpart_scr[h], axis=0, keepdims=True)

    def value_chunk(kc):
        vt_c = vt_scr[kc]
        for h in range(N_HEADS):
            p = jnp.exp(sall_scr[kc, h] - m_scr[h:h + 1, :]).astype(BF16)
            acc_scr[h] = acc_scr[h] + _dot(vt_c, p)

    _for_chunks(n_chunks, value_chunk)

    outs = []
    for h in range(N_HEADS):
        a = acc_scr[h]
        outs.append(a[0:HEAD_DIM, :] / a[HEAD_DIM:HEAD_DIM + 1, :])
    o_ref[...] = jnp.transpose(jnp.concatenate(outs, axis=0)).astype(BF16)


def _attention(q, qi, wi, kv, ki, batch):
    tq, ck = Q_TILE, K_CHUNK
    seq = q.shape[0]
    topk = min(TOPK_MAX, seq // 4)
    nq = seq // tq

    def q_spec(c):
        return pl.BlockSpec((tq, c), lambda b, i: (i, b))

    def kv_spec(c):
        return pl.BlockSpec((seq, c), lambda b, i: (0, b))

    return pl.pallas_call(
        functools.partial(_attn_kernel, topk=topk),
        out_shape=jax.ShapeDtypeStruct((seq, batch * D_ATT), BF16),
        grid=(batch, nq),
        in_specs=[q_spec(D_ATT), q_spec(IDX_HEADS * IDX_DIM), q_spec(LANES),
                  kv_spec(4 * HEAD_DIM), kv_spec(LANES)],
        out_specs=q_spec(D_ATT),
        scratch_shapes=[
            pltpu.VMEM((N_HEADS, tq, LANES), BF16),
            pltpu.VMEM((IDX_HEADS, tq, LANES), BF16),
            pltpu.VMEM((LANES, tq), F32),
            pltpu.VMEM((seq // ck, VT_ROWS, ck), BF16),
            pltpu.VMEM((seq // ck, ck, tq), F32),
            pltpu.VMEM((seq // ck, ck, tq), BF16),
            pltpu.VMEM((seq // ck, N_HEADS, ck, tq), F32),
            pltpu.VMEM((N_HEADS, tq), F32),
            pltpu.VMEM((N_HEADS, VT_ROWS, tq), F32),
            pltpu.VMEM((N_HEADS, SUBLANES, tq), F32),
            pltpu.VMEM((1, tq), F32),
        ],
        compiler_params=pltpu.CompilerParams(
            dimension_semantics=("parallel", "arbitrary"),
            vmem_limit_bytes=VMEM_LIMIT),
        name="dsa_attention",
    )(q, qi, wi, kv, ki)


def _merge_kernel(att_ref, rnn_ref, ga_ref, gl_ref, x_ref, wa_ref, wr_ref,
                  wo_ref, o_ref, slab_scr, *, batch):
    att = _seq_to_rows(att_ref, slab_scr, batch).astype(BF16)
    pa = _dot(att, wa_ref[...])
    pr = _dot(rnn_ref[...], wr_ref[...])
    merged = (jax.nn.sigmoid(ga_ref[...].astype(F32)) * pa
              + jax.nn.sigmoid(gl_ref[...].astype(F32)) * pr)
    o_ref[...] = x_ref[...] + _dot(merged.astype(BF16), wo_ref[...])


def _merge(att, rnn, ga, gl, x, wa, wr, wo, layer, batch):
    rows, d = x.shape
    tt = ROW_TILE // batch

    def row_spec(c):
        return pl.BlockSpec((ROW_TILE, c), lambda i: (i, 0))

    return pl.pallas_call(
        functools.partial(_merge_kernel, batch=batch),
        out_shape=jax.ShapeDtypeStruct((rows, d), F32),
        grid=(rows // ROW_TILE,),
        in_specs=[pl.BlockSpec((tt, att.shape[1]), lambda i: (i, 0)),
                  row_spec(rnn.shape[1]), row_spec(d), row_spec(d),
                  row_spec(d), _resident(wa.shape[1:], layer),
                  _resident(wr.shape[1:], layer),
                  _resident(wo.shape[1:], layer)],
        out_specs=row_spec(d),
        scratch_shapes=[pltpu.VMEM((_slabs(D_ATT), ROW_TILE, LANES), F32)],
        compiler_params=pltpu.CompilerParams(
            dimension_semantics=("parallel",), vmem_limit_bytes=VMEM_LIMIT),
        name="merge",
    )(att, rnn, ga, gl, x, wa, wr, wo)


def _inproj_weights(w_in, d_rnn, d_model):
    splits = (D_ATT, HEAD_DIM, HEAD_DIM, IDX_HEADS * IDX_DIM, IDX_DIM,
              IDX_HEADS)
    parts, c = [], 0
    for n in splits:
        parts.append(w_in[..., c:c + n])
        c += n
    assert c + 2 * d_rnn + 2 * d_model == w_in.shape[-1]
    wq, wk, wv, wqi, wki, wwi = parts
    wi_pad = jnp.zeros(w_in.shape[:-1] + (LANES - IDX_HEADS,), w_in.dtype)
    cols = [wq, wk, wv, wv, wk, wqi] + [wki] * (LANES // IDX_DIM)
    cols += [wwi, wi_pad]
    return (jnp.concatenate(cols, axis=-1).astype(BF16),
            w_in[..., c:].astype(BF16))


def _block_diag_tiles(w):
    depth, n_blocks, bw, _ = w.shape
    per = MXU_DIM // bw
    w = w.reshape(depth, n_blocks // per, per, bw, bw)
    eye = jnp.eye(per, dtype=w.dtype)
    t = w[:, :, :, :, None, :] * eye[None, None, :, None, :, None]
    return t.reshape(depth, n_blocks // per, MXU_DIM, MXU_DIM).astype(BF16)


def kernel(x, ffn1_norm, ffn1_wg, ffn1_wu, ffn1_wd, mix_norm, w_in, conv_w,
           conv_b, rg_wa, rg_ba, rg_wx, rg_bx, rg_lam, w_att_proj, w_rnn_proj,
           w_out, ffn2_norm, ffn2_wg, ffn2_wu, ffn2_wd, final_norm):
    batch, seq, d = x.shape
    depth = ffn1_norm.shape[0]
    d_rnn = conv_w.shape[-1]
    assert batch == SUBLANES and seq % Q_TILE == 0
    assert (batch * seq) % FFN_ROW_TILE == 0 and ROW_TILE % batch == 0

    def row(v):
        return v.reshape(1, -1)

    wg1, wu1, wd1 = (w.astype(BF16) for w in (ffn1_wg, ffn1_wu, ffn1_wd))
    wg2, wu2, wd2 = (w.astype(BF16) for w in (ffn2_wg, ffn2_wu, ffn2_wd))
    w_att, w_wide = _inproj_weights(w_in, d_rnn, d)
    wa_t, wx_t = _block_diag_tiles(rg_wa), _block_diag_tiles(rg_wx)
    w_ap, w_rp, w_o = (w.astype(BF16) for w in (w_att_proj, w_rnn_proj, w_out))

    h = x
    for l in range(depth):
        h = _ffn(h, row(ffn1_norm[l]), wg1, wu1, wd1, l, batch, first=(l == 0))
        q, kv, qi, ki, wi, ga, gl, rnn = _mixer_in(
            h, row(mix_norm[l]), w_att, w_wide, conv_w[l], row(conv_b[l]), wa_t,
            row(rg_ba[l]), wx_t, row(rg_bx[l]), row(rg_lam[l]), l, batch)
        att = _attention(q, qi, wi, kv, ki, batch)
        h = _merge(att, rnn, ga, gl, h, w_ap, w_rp, w_o, l, batch)
        last = l == depth - 1
        h = _ffn(h, row(ffn2_norm[l]), wg2, wu2, wd2, l, batch,
                 final_g=row(final_norm) if last else None)
    return h
```

```python
import functools

import jax
import jax.numpy as jnp
from jax import lax
from jax.experimental import pallas as pl
from jax.experimental.pallas import tpu as pltpu

F32 = jnp.float32
BF16 = jnp.bfloat16
I32 = jnp.int32

N_HEADS = 8
HEAD_DIM = 64
D_ATT = N_HEADS * HEAD_DIM
IDX_HEADS = 8
IDX_DIM = 32
TOPK_MAX = 256
CONV_W = 4
LRU_C = 8.0
EPS = 1e-6

LANES = 128
SUBLANES = 8
BF16_SUBLANES = 16
MXU_DIM = 256
VMEM_LIMIT = 56 * 1024 * 1024

INT_MIN = -2 ** 31
MASK_BIAS = -1e30

ROW_TILE = 512
FFN_ROW_TILE = 1024
Q_TILE = 256
K_CHUNK = 256


def _resident(shape, layer=None):
    zeros = (0,) * len(shape)
    if layer is None:
        return pl.BlockSpec(shape, lambda *_: zeros,
                            pipeline_mode=pl.Buffered(1))
    return pl.BlockSpec((None,) + tuple(shape), lambda *_: (layer,) + zeros,
                        pipeline_mode=pl.Buffered(1))


def _rmsnorm(x, g):
    ms = jnp.mean(x * x, axis=-1, keepdims=True)
    return x * lax.rsqrt(ms + EPS) * g


def _dot(a, b):
    return jnp.dot(a, b, preferred_element_type=F32)


def _dot_nt(a, b):
    return lax.dot_general(a, b, (((1,), (1,)), ((), ())),
                           preferred_element_type=F32)


def _ff_chunks(d_ff):
    chunks, c0 = [], 0
    while c0 < d_ff:
        c1 = min(c0 + 2 * MXU_DIM, d_ff)
        chunks.append((c0, c1))
        c0 = c1
    return chunks


def _slabs(width):
    return width // LANES


def _batch_to_rows(x_ref, slab_scr, batch):
    tt, d = x_ref.shape[1:]
    for b in range(batch):
        for j in range(_slabs(d)):
            slab_scr[j, pl.ds(b, tt, stride=batch), :] = (
                x_ref[b, :, j * LANES:(j + 1) * LANES])
    return jnp.concatenate([slab_scr[j] for j in range(_slabs(d))], axis=1)


def _rows_to_batch(o_ref, y, slab_scr, batch):
    tt, d = o_ref.shape[1:]
    for j in range(_slabs(d)):
        slab_scr[j] = y[:, j * LANES:(j + 1) * LANES]
    for b in range(batch):
        for j in range(_slabs(d)):
            o_ref[b, :, j * LANES:(j + 1) * LANES] = (
                slab_scr[j, pl.ds(b, tt, stride=batch), :])


def _rows_to_seq(o_ref, z, slab_scr, slab0, batch):
    tt = o_ref.shape[0]
    c = z.shape[1]
    for j in range(_slabs(c)):
        slab_scr[slab0 + j] = z[:, j * LANES:(j + 1) * LANES]
    for b in range(batch):
        for j in range(_slabs(c)):
            o_ref[:, b * c + j * LANES:b * c + (j + 1) * LANES] = (
                slab_scr[slab0 + j, pl.ds(b, tt, stride=batch), :]
                .astype(o_ref.dtype))


def _seq_to_rows(x_ref, slab_scr, batch):
    tt = x_ref.shape[0]
    c = x_ref.shape[1] // batch
    for b in range(batch):
        for j in range(_slabs(c)):
            slab_scr[j, pl.ds(b, tt, stride=batch), :] = (
                x_ref[:, b * c + j * LANES:b * c + (j + 1) * LANES].astype(F32))
    return jnp.concatenate([slab_scr[j] for j in range(_slabs(c))], axis=1)


def _ffn_kernel(x_ref, g_ref, wg_ref, wu_ref, wd_ref, *rest, first, final,
                batch):
    rest = list(rest)
    fg_ref = rest.pop(0) if final else None
    o_ref = rest.pop(0)
    slab_scr = rest.pop(0) if (first or final) else None
    x = _batch_to_rows(x_ref, slab_scr, batch) if first else x_ref[...]
    h = _rmsnorm(x, g_ref[...]).astype(BF16)
    acc = None
    for c0, c1 in _ff_chunks(wg_ref.shape[1]):
        g = _dot(h, wg_ref[:, c0:c1])
        u = _dot(h, wu_ref[:, c0:c1])
        a = (g * jax.nn.sigmoid(g) * u).astype(BF16)
        d = _dot(a, wd_ref[c0:c1, :])
        acc = d if acc is None else acc + d
    y = x + 0.5 * acc
    if final:
        _rows_to_batch(o_ref, _rmsnorm(y, fg_ref[...]), slab_scr, batch)
    else:
        o_ref[...] = y


def _ffn(x, g, wg, wu, wd, layer, batch, first=False, final_g=None):
    final = final_g is not None
    if first:
        _, seq, d = x.shape
        rows = seq * batch
    else:
        rows, d = x.shape
        seq = rows // batch
    d_ff = wg.shape[2]
    tt = FFN_ROW_TILE // batch
    row_spec = pl.BlockSpec((FFN_ROW_TILE, d), lambda i: (i, 0))
    batch_spec = pl.BlockSpec((batch, tt, d), lambda i: (0, i, 0))
    in_specs = [batch_spec if first else row_spec, _resident((1, d)),
                _resident((d, d_ff), layer), _resident((d, d_ff), layer),
                _resident((d_ff, d), layer)]
    args = [x, g, wg, wu, wd]
    if final:
        in_specs.append(_resident((1, d)))
        args.append(final_g)
    scratch = ([pltpu.VMEM((_slabs(d), FFN_ROW_TILE, LANES), F32)]
               if (first or final) else [])
    out_shape = (batch, seq, d) if final else (rows, d)
    return pl.pallas_call(
        functools.partial(_ffn_kernel, first=first, final=final, batch=batch),
        out_shape=jax.ShapeDtypeStruct(out_shape, F32),
        grid=(rows // FFN_ROW_TILE,),
        in_specs=in_specs,
        out_specs=batch_spec if final else row_spec,
        scratch_shapes=scratch,
        compiler_params=pltpu.CompilerParams(
            dimension_semantics=("parallel",), vmem_limit_bytes=VMEM_LIMIT),
        name="ffn_first" if first else ("ffn_final" if final else "ffn"),
    )(*args)


ATT_WIDTHS = (D_ATT, 4 * HEAD_DIM, IDX_HEADS * IDX_DIM, LANES, LANES)
ATT_DTYPES = (BF16, BF16, BF16, BF16, F32)


def _mixer_in_kernel(x_ref, g_ref, w_att_ref, w_ref, cw_ref, cb_ref, wa_ref,
                     ba_ref,
                     wx_ref, bx_ref, lam_ref,
                     q_ref, kv_ref, qi_ref, ki_ref, wi_ref, ga_ref, gl_ref,
                     rnn_ref, slab_scr, xs_scr, a_scr, u_scr, h_scr, hc_scr,
                     *, batch):
    tm = x_ref.shape[0]
    c = cw_ref.shape[1]
    d_model = ga_ref.shape[1]
    halo = (CONV_W - 1) * batch
    h = _rmsnorm(x_ref[...], g_ref[...]).astype(BF16)

    @pl.when(pl.program_id(0) == 0)
    def _():
        xs_scr[0:halo, :] = jnp.zeros((halo, c), F32)
        hc_scr[...] = jnp.zeros(hc_scr.shape, F32)

    scales = (HEAD_DIM ** -0.5, None, None, None, (IDX_HEADS * IDX_DIM) ** -0.5)
    att_refs = (q_ref, kv_ref, qi_ref, ki_ref, wi_ref)
    gate_col0 = 2 * c

    def att_operand(i):
        col = sum(ATT_WIDTHS[:i])
        z = _dot(h, w_att_ref[:, col:col + ATT_WIDTHS[i]])
        if scales[i] is not None:
            z = z * scales[i]
        _rows_to_seq(att_refs[i], z, slab_scr, _slabs(sum(ATT_WIDTHS[:i])),
                     batch)

    def gate_logits(o_ref):
        col = gate_col0 if o_ref is ga_ref else gate_col0 + d_model
        o_ref[...] = _dot(h, w_ref[:, col:col + d_model]).astype(o_ref.dtype)

    fillers = (lambda: gate_logits(ga_ref),
               lambda: gate_logits(gl_ref),
               lambda: (att_operand(0), att_operand(1)),
               lambda: (att_operand(2), att_operand(3), att_operand(4)))

    n_tiles = c // MXU_DIM
    assert n_tiles == len(fillers)
    for n in range(n_tiles):
        cs = slice(n * MXU_DIM, (n + 1) * MXU_DIM)
        xr = _dot(h, w_ref[:, cs])
        gr = _dot(h, w_ref[:, c + n * MXU_DIM:c + (n + 1) * MXU_DIM])
        xs_scr[halo:halo + tm, cs] = xr
        xc = cb_ref[:, cs]
        for j in range(CONV_W):
            xc = xc + cw_ref[j:j + 1, cs] * xs_scr[j * batch:j * batch + tm, cs]
        xs_scr[0:halo, cs] = xr[tm - halo:tm, :]

        xb = xc.astype(BF16)
        r = jax.nn.sigmoid(_dot(xb, wa_ref[n]) + ba_ref[:, cs])
        gate_i = jax.nn.sigmoid(_dot(xb, wx_ref[n]) + bx_ref[:, cs])
        fillers[n]()
        neg_lam = -lam_ref[:, cs]
        softplus = (jnp.maximum(neg_lam, 0.0)
                    + jnp.log1p(jnp.exp(-jnp.abs(neg_lam))))
        log_a = -LRU_C * r * softplus
        a_scr[:, cs] = jnp.exp(log_a)
        th = jnp.tanh(log_a)
        u_scr[:, cs] = jnp.sqrt(-2.0 * th / (1.0 - th)) * gate_i * xc

        hcur = hc_scr[:, cs]
        for t in range(tm // batch):
            rs = slice(t * batch, (t + 1) * batch)
            hcur = a_scr[rs, cs] * hcur + u_scr[rs, cs]
            h_scr[rs, cs] = hcur
        hc_scr[:, cs] = hcur
        rnn_ref[:, cs] = (h_scr[:, cs] * jax.nn.gelu(gr)).astype(BF16)


def _mixer_in(x, g, w_att, w, cw, cb, wa, ba, wx, bx, lam, layer, batch):
    rows, d = x.shape
    c = cw.shape[1]
    tm = ROW_TILE
    tt = tm // batch
    seq = rows // batch
    halo = (CONV_W - 1) * batch
    assert w.shape[2] == 2 * c + 2 * d and w_att.shape[2] == sum(ATT_WIDTHS)

    def row_spec(n):
        return pl.BlockSpec((tm, n), lambda i: (i, 0))

    def seq_spec(n):
        return pl.BlockSpec((tt, batch * n), lambda i: (i, 0))

    out_shape = [jax.ShapeDtypeStruct((seq, batch * n), t)
                 for n, t in zip(ATT_WIDTHS, ATT_DTYPES)]
    out_shape += [jax.ShapeDtypeStruct((rows, d), BF16)] * 2
    out_shape += [jax.ShapeDtypeStruct((rows, c), BF16)]
    out_specs = [seq_spec(n) for n in ATT_WIDTHS]
    out_specs += [row_spec(d), row_spec(d), row_spec(c)]
    return pl.pallas_call(
        functools.partial(_mixer_in_kernel, batch=batch),
        out_shape=out_shape,
        grid=(rows // tm,),
        in_specs=[row_spec(d), _resident((1, d)),
                  _resident(w_att.shape[1:], layer),
                  _resident(w.shape[1:], layer), _resident(cw.shape),
                  _resident((1, c)), _resident(wa.shape[1:], layer),
                  _resident((1, c)), _resident(wx.shape[1:], layer),
                  _resident((1, c)), _resident((1, c))],
        out_specs=out_specs,
        scratch_shapes=[
            pltpu.VMEM((_slabs(sum(ATT_WIDTHS)), tm, LANES), F32),
            pltpu.VMEM((halo + tm, c), F32),
            pltpu.VMEM((tm, c), F32), pltpu.VMEM((tm, c), F32),
            pltpu.VMEM((tm, c), F32), pltpu.VMEM((batch, c), F32)],
        compiler_params=pltpu.CompilerParams(
            dimension_semantics=("arbitrary",), vmem_limit_bytes=VMEM_LIMIT),
        name="mixer_in",
    )(x, g, w_att, w, cw, cb, wa, ba, wx, bx, lam)


VT_ROWS = HEAD_DIM + 16


def _key_to_float(key):
    bits = jnp.where(key < 0, INT_MIN - key, key)
    return pltpu.bitcast(bits, F32)


def _for_chunks(n, body):
    def quad(i, carry):
        for j in range(4):
            body(4 * i + j)
        return carry

    lax.fori_loop(0, n // 4, quad, 0)
    base = (n // 4) * 4

    @pl.when(n % 4 >= 2)
    def _():
        body(base)
        body(base + 1)

    @pl.when(n % 2 == 1)
    def _():
        body(n - 1)


def _fold_chunks(n, chunk, init):
    def pair(i, c):
        return chunk(2 * i + 1, chunk(2 * i, c))

    c = lax.fori_loop(0, n // 2, pair, init)
    return lax.cond(n % 2 == 1, lambda c: chunk(n - 1, c), lambda c: c, c)


def _attn_kernel(q_ref, qi_ref, wi_ref, kv_ref, ki_ref, o_ref,
                 qm_scr, qim_scr, wit_scr, vt_scr, sc_scr, scb_scr, sall_scr, m_scr,
                 acc_scr, mpart_scr, ties_scr, *, topk):
    tq, ck = Q_TILE, K_CHUNK
    n_kc = sc_scr.shape[0]
    seq = n_kc * ck
    qt = pl.program_id(1)
    n_chunks = (qt * tq) // ck + tq // ck
    lane = lax.broadcasted_iota(I32, (tq, LANES), 1)
    key_iota = lax.broadcasted_iota(I32, (ck, tq), 0)
    qry_iota = lax.broadcasted_iota(I32, (ck, tq), 1)
    kf = float(topk)

    for h in range(N_HEADS):
        qp = q_ref[:, (h // 2) * LANES:(h // 2 + 1) * LANES].astype(F32)
        lo = (h % 2) * HEAD_DIM
        qm_scr[h] = jnp.where((lane >= lo) & (lane < lo + HEAD_DIM), qp,
                              0.0).astype(BF16)
    for h in range(IDX_HEADS):
        qp = qi_ref[:, (h // 4) * LANES:(h // 4 + 1) * LANES].astype(F32)
        lo = (h % 4) * IDX_DIM
        qim_scr[h] = jnp.where((lane >= lo) & (lane < lo + IDX_DIM), qp,
                               0.0).astype(BF16)
    wit_scr[...] = jnp.transpose(wi_ref[...])

    @pl.when(qt == 0)
    def _():
        lane_c = lax.broadcasted_iota(I32, (ck, LANES), 1)
        for c in range(n_kc):
            v1 = jnp.where(lane_c < HEAD_DIM,
                           kv_ref[c * ck:(c + 1) * ck, LANES:2 * LANES].astype(F32),
                           1.0)
            vt_scr[c] = jnp.transpose(v1)[0:VT_ROWS, :].astype(BF16)

    def score_chunk(kc):
        k0 = pl.multiple_of(kc * ck, ck)
        kic = ki_ref[pl.ds(k0, ck), :]
        acc = jnp.zeros((ck, tq), F32)
        for h in range(IDX_HEADS):
            acc = acc + (jnp.maximum(_dot_nt(kic, qim_scr[h]), 0.0)
                         * wit_scr[h:h + 1, :])
        causal = key_iota - qry_iota <= qt * tq - kc * ck
        sc = jnp.where(causal, acc, -jnp.inf)
        sc_scr[kc] = sc
        scb_scr[kc] = sc.astype(BF16)

    _for_chunks(n_chunks, score_chunk)

    def count(pred):
        acc_rows = 2 * SUBLANES

        def chunk(kc, c):
            hit = jnp.where(pred(sc_scr[kc], kc), 1.0, 0.0)
            return c + jnp.sum(hit.reshape(ck // acc_rows, acc_rows, tq), axis=0)

        c = _fold_chunks(n_chunks, chunk, jnp.zeros((acc_rows, tq), F32))
        return jnp.sum(c, axis=0, keepdims=True)

    def count_bf16(cand):
        acc_rows = 2 * BF16_SUBLANES
        one, zero = jnp.ones((), BF16), jnp.zeros((), BF16)

        def chunk(kc, c):
            hit = jnp.where(scb_scr[kc] >= cand, one, zero)
            part = hit[0:acc_rows]
            for r0 in range(acc_rows, ck, acc_rows):
                part = part + hit[r0:r0 + acc_rows]
            return c + part.astype(F32)

        c = _fold_chunks(n_chunks, chunk, jnp.zeros((acc_rows, tq), F32))
        return jnp.sum(c, axis=0, keepdims=True)

    def coarse_step(i, thr16):
        cand16 = thr16 + lax.shift_left(jnp.int32(1), 15 - i)
        cand_f = _key_to_float(cand16 * 65536)
        tot = count_bf16(cand_f.astype(BF16))
        return jnp.where(tot >= kf, cand16, thr16)

    thr16 = lax.fori_loop(0, 16, coarse_step,
                          jnp.full((1, tq), -2 ** 15, I32))

    def fine_step(i, carry):
        thr, above = carry
        cand = thr + lax.shift_left(jnp.int32(1), 16 - i)
        cand_f = _key_to_float(cand)
        tot = count(lambda s, kc: s >= cand_f)
        ok = tot >= kf
        return jnp.where(ok, cand, thr), jnp.where(ok, above, tot)

    thr, above = lax.fori_loop(
        0, 17, fine_step,
        ((thr16 - 1) * 65536, jnp.full((1, tq), -1.0, F32)))

    thr_f = _key_to_float(thr)
    short = jnp.logical_not(thr_f > -jnp.inf)
    thr_f = jnp.where(short, -jnp.inf, thr_f)
    n_gt = lax.cond(jnp.min(above) < 0.0,
                    lambda: count(lambda s, kc: s > thr_f), lambda: above)
    need = jnp.where(short, 0.0, kf - n_gt)
    lower_tri = jnp.where(
        lax.broadcasted_iota(I32, (ck, ck), 0)
        >= lax.broadcasted_iota(I32, (ck, ck), 1), 1.0, 0.0).astype(BF16)

    mpart_scr[...] = jnp.full(mpart_scr.shape, MASK_BIAS, F32)
    acc_scr[...] = jnp.zeros(acc_scr.shape, F32)

    ties_scr[...] = jnp.zeros(ties_scr.shape, F32)

    def scores_chunk(kc):
        k0 = pl.multiple_of(kc * ck, ck)
        s_idx = sc_scr[kc]
        tie = s_idx == thr_f
        tie_f = jnp.where(tie, 1.0, 0.0)
        ties_before = ties_scr[...]
        ties_scr[...] = ties_before + jnp.sum(tie_f, axis=0, keepdims=True)
        rank = ties_before + _dot(lower_tri, tie_f.astype(BF16))
        keep = (s_idx > thr_f) | (tie & (rank <= need))
        bias = jnp.where(keep, 0.0, MASK_BIAS)
        for h in range(N_HEADS):
            c0 = 0 if h % 2 == 0 else LANES
            k_mat = kv_ref[pl.ds(k0, ck), c0:c0 + LANES]
            s = _dot_nt(k_mat, qm_scr[h]) + bias
            sall_scr[kc, h] = s
            mpart_scr[h] = jnp.maximum(
                mpart_scr[h],
                jnp.max(s.reshape(ck // SUBLANES, SUBLANES, tq), axis=0))

    _for_chunks(n_chunks, scores_chunk)
    for h in range(N_HEADS):
        m_scr[h:h + 1, :] = jnp.max(mpart_scr[h], axis=0, keepdims=True)

    def value_chunk(kc):
        vt_c = vt_scr[kc]
        for h in range(N_HEADS):
            p = jnp.exp(sall_scr[kc, h] - m_scr[h:h + 1, :]).astype(BF16)
            acc_scr[h] = acc_scr[h] + _dot(vt_c, p)

    _for_chunks(n_chunks, value_chunk)

    outs = []
    for h in range(N_HEADS):
        a = acc_scr[h]
        outs.append(a[0:HEAD_DIM, :] / a[HEAD_DIM:HEAD_DIM + 1, :])
    o_ref[...] = jnp.transpose(jnp.concatenate(outs, axis=0)).astype(BF16)


def _attention(q, qi, wi, kv, ki, batch):
    tq, ck = Q_TILE, K_CHUNK
    seq = q.shape[0]
    topk = min(TOPK_MAX, seq // 4)
    nq = seq // tq

    def q_spec(c):
        return pl.BlockSpec((tq, c), lambda b, i: (i, b))

    def kv_spec(c):
        return pl.BlockSpec((seq, c), lambda b, i: (0, b))

    return pl.pallas_call(
        functools.partial(_attn_kernel, topk=topk),
        out_shape=jax.ShapeDtypeStruct((seq, batch * D_ATT), BF16),
        grid=(batch, nq),
        in_specs=[q_spec(D_ATT), q_spec(IDX_HEADS * IDX_DIM), q_spec(LANES),
                  kv_spec(4 * HEAD_DIM), kv_spec(LANES)],
        out_specs=q_spec(D_ATT),
        scratch_shapes=[
            pltpu.VMEM((N_HEADS, tq, LANES), BF16),
            pltpu.VMEM((IDX_HEADS, tq, LANES), BF16),
            pltpu.VMEM((LANES, tq), F32),
            pltpu.VMEM((seq // ck, VT_ROWS, ck), BF16),
            pltpu.VMEM((seq // ck, ck, tq), F32),
            pltpu.VMEM((seq // ck, ck, tq), BF16),
            pltpu.VMEM((seq // ck, N_HEADS, ck, tq), F32),
            pltpu.VMEM((N_HEADS, tq), F32),
            pltpu.VMEM((N_HEADS, VT_ROWS, tq), F32),
            pltpu.VMEM((N_HEADS, SUBLANES, tq), F32),
            pltpu.VMEM((1, tq), F32),
        ],
        compiler_params=pltpu.CompilerParams(
            dimension_semantics=("parallel", "arbitrary"),
            vmem_limit_bytes=VMEM_LIMIT),
        name="dsa_attention",
    )(q, qi, wi, kv, ki)


def _merge_kernel(att_ref, rnn_ref, ga_ref, gl_ref, x_ref, wa_ref, wr_ref,
                  wo_ref, o_ref, slab_scr, *, batch):
    att = _seq_to_rows(att_ref, slab_scr, batch).astype(BF16)
    pa = _dot(att, wa_ref[...])
    pr = _dot(rnn_ref[...], wr_ref[...])
    merged = (jax.nn.sigmoid(ga_ref[...].astype(F32)) * pa
              + jax.nn.sigmoid(gl_ref[...].astype(F32)) * pr)
    o_ref[...] = x_ref[...] + _dot(merged.astype(BF16), wo_ref[...])


def _merge(att, rnn, ga, gl, x, wa, wr, wo, layer, batch):
    rows, d = x.shape
    tt = FFN_ROW_TILE // batch

    def row_spec(c):
        return pl.BlockSpec((FFN_ROW_TILE, c), lambda i: (i, 0))

    return pl.pallas_call(
        functools.partial(_merge_kernel, batch=batch),
        out_shape=jax.ShapeDtypeStruct((rows, d), F32),
        grid=(rows // FFN_ROW_TILE,),
        in_specs=[pl.BlockSpec((tt, att.shape[1]), lambda i: (i, 0)),
                  row_spec(rnn.shape[1]), row_spec(d), row_spec(d),
                  row_spec(d), _resident(wa.shape[1:], layer),
                  _resident(wr.shape[1:], layer),
                  _resident(wo.shape[1:], layer)],
        out_specs=row_spec(d),
        scratch_shapes=[pltpu.VMEM((_slabs(D_ATT), FFN_ROW_TILE, LANES), F32)],
        compiler_params=pltpu.CompilerParams(
            dimension_semantics=("parallel",), vmem_limit_bytes=VMEM_LIMIT),
        name="merge",
    )(att, rnn, ga, gl, x, wa, wr, wo)


def _inproj_weights(w_in, d_rnn, d_model):
    splits = (D_ATT, HEAD_DIM, HEAD_DIM, IDX_HEADS * IDX_DIM, IDX_DIM,
              IDX_HEADS)
    parts, c = [], 0
    for n in splits:
        parts.append(w_in[..., c:c + n])
        c += n
    assert c + 2 * d_rnn + 2 * d_model == w_in.shape[-1]
    wq, wk, wv, wqi, wki, wwi = parts
    wi_pad = jnp.zeros(w_in.shape[:-1] + (LANES - IDX_HEADS,), w_in.dtype)
    cols = [wq, wk, wv, wv, wk, wqi] + [wki] * (LANES // IDX_DIM)
    cols += [wwi, wi_pad]
    return (jnp.concatenate(cols, axis=-1).astype(BF16),
            w_in[..., c:].astype(BF16))


def _block_diag_tiles(w):
    depth, n_blocks, bw, _ = w.shape
    per = MXU_DIM // bw
    w = w.reshape(depth, n_blocks // per, per, bw, bw)
    eye = jnp.eye(per, dtype=w.dtype)
    t = w[:, :, :, :, None, :] * eye[None, None, :, None, :, None]
    return t.reshape(depth, n_blocks // per, MXU_DIM, MXU_DIM).astype(BF16)


def kernel(x, ffn1_norm, ffn1_wg, ffn1_wu, ffn1_wd, mix_norm, w_in, conv_w,
           conv_b, rg_wa, rg_ba, rg_wx, rg_bx, rg_lam, w_att_proj, w_rnn_proj,
           w_out, ffn2_norm, ffn2_wg, ffn2_wu, ffn2_wd, final_norm):
    batch, seq, d = x.shape
    depth = ffn1_norm.shape[0]
    d_rnn = conv_w.shape[-1]
    assert batch == SUBLANES and seq % Q_TILE == 0
    assert (batch * seq) % FFN_ROW_TILE == 0 and ROW_TILE % batch == 0

    def row(v):
        return v.reshape(1, -1)

    wg1, wu1, wd1 = (w.astype(BF16) for w in (ffn1_wg, ffn1_wu, ffn1_wd))
    wg2, wu2, wd2 = (w.astype(BF16) for w in (ffn2_wg, ffn2_wu, ffn2_wd))
    w_att, w_wide = _inproj_weights(w_in, d_rnn, d)
    wa_t, wx_t = _block_diag_tiles(rg_wa), _block_diag_tiles(rg_wx)
    w_ap, w_rp, w_o = (w.astype(BF16) for w in (w_att_proj, w_rnn_proj, w_out))

    h = x
    for l in range(depth):
        h = _ffn(h, row(ffn1_norm[l]), wg1, wu1, wd1, l, batch, first=(l == 0))
        q, kv, qi, ki, wi, ga, gl, rnn = _mixer_in(
            h, row(mix_norm[l]), w_att, w_wide, conv_w[l], row(conv_b[l]), wa_t,
            row(rg_ba[l]), wx_t, row(rg_bx[l]), row(rg_lam[l]), l, batch)
        att = _attention(q, qi, wi, kv, ki, batch)
        h = _merge(att, rnn, ga, gl, h, w_ap, w_rp, w_o, l, batch)
        last = l == depth - 1
        h = _ffn(h, row(ffn2_norm[l]), wg2, wu2, wd2, l, batch,
                 final_g=row(final_norm) if last else None)
    return h
```

```python
import functools

import jax
import jax.numpy as jnp
from jax import lax
from jax.experimental import pallas as pl
from jax.experimental.pallas import tpu as pltpu

F32 = jnp.float32
BF16 = jnp.bfloat16
I32 = jnp.int32

N_HEADS = 8
HEAD_DIM = 64
D_ATT = N_HEADS * HEAD_DIM
IDX_HEADS = 8
IDX_DIM = 32
TOPK_MAX = 256
CONV_W = 4
LRU_C = 8.0
EPS = 1e-6

LANES = 128
SUBLANES = 8
BF16_SUBLANES = 16
MXU_DIM = 256
VMEM_LIMIT = 56 * 1024 * 1024

INT_MIN = -2 ** 31
MASK_BIAS = -1e30

ROW_TILE = 512
MERGE_ROW_TILE = 1024
Q_TILE = 256
K_CHUNK = 256


def _resident(shape, layer=None):
    zeros = (0,) * len(shape)
    if layer is None:
        return pl.BlockSpec(shape, lambda *_: zeros,
                            pipeline_mode=pl.Buffered(1))
    return pl.BlockSpec((None,) + tuple(shape), lambda *_: (layer,) + zeros,
                        pipeline_mode=pl.Buffered(1))


def _rmsnorm(x, g):
    ms = jnp.mean(x * x, axis=-1, keepdims=True)
    return x * lax.rsqrt(ms + EPS) * g


def _dot(a, b):
    return jnp.dot(a, b, preferred_element_type=F32)


def _dot_nt(a, b):
    return lax.dot_general(a, b, (((1,), (1,)), ((), ())),
                           preferred_element_type=F32)


def _ff_chunks(d_ff):
    chunks, c0 = [], 0
    while c0 < d_ff:
        c1 = min(c0 + 2 * MXU_DIM, d_ff)
        chunks.append((c0, c1))
        c0 = c1
    return chunks


def _slabs(width):
    return width // LANES


def _batch_to_rows(x_ref, slab_scr, batch):
    tt, d = x_ref.shape[1:]
    for b in range(batch):
        for j in range(_slabs(d)):
            slab_scr[j, pl.ds(b, tt, stride=batch), :] = (
                x_ref[b, :, j * LANES:(j + 1) * LANES])
    return jnp.concatenate([slab_scr[j] for j in range(_slabs(d))], axis=1)


def _rows_to_batch(o_ref, y, slab_scr, batch):
    tt, d = o_ref.shape[1:]
    for j in range(_slabs(d)):
        slab_scr[j] = y[:, j * LANES:(j + 1) * LANES]
    for b in range(batch):
        for j in range(_slabs(d)):
            o_ref[b, :, j * LANES:(j + 1) * LANES] = (
                slab_scr[j, pl.ds(b, tt, stride=batch), :])


def _rows_to_seq(o_ref, z, slab_scr, slab0, batch):
    tt = o_ref.shape[0]
    c = z.shape[1]
    for j in range(_slabs(c)):
        slab_scr[slab0 + j] = z[:, j * LANES:(j + 1) * LANES]
    for b in range(batch):
        for j in range(_slabs(c)):
            o_ref[:, b * c + j * LANES:b * c + (j + 1) * LANES] = (
                slab_scr[slab0 + j, pl.ds(b, tt, stride=batch), :]
                .astype(o_ref.dtype))


def _seq_to_rows(x_ref, slab_scr, batch):
    tt = x_ref.shape[0]
    c = x_ref.shape[1] // batch
    for b in range(batch):
        for j in range(_slabs(c)):
            slab_scr[j, pl.ds(b, tt, stride=batch), :] = (
                x_ref[:, b * c + j * LANES:b * c + (j + 1) * LANES].astype(F32))
    return jnp.concatenate([slab_scr[j] for j in range(_slabs(c))], axis=1)


def _ffn_kernel(x_ref, g_ref, wg_ref, wu_ref, wd_ref, *rest, first, final,
                batch):
    rest = list(rest)
    fg_ref = rest.pop(0) if final else None
    o_ref = rest.pop(0)
    slab_scr = rest.pop(0) if (first or final) else None
    x = _batch_to_rows(x_ref, slab_scr, batch) if first else x_ref[...]
    h = _rmsnorm(x, g_ref[...]).astype(BF16)
    acc = None
    for c0, c1 in _ff_chunks(wg_ref.shape[1]):
        g = _dot(h, wg_ref[:, c0:c1].astype(BF16))
        u = _dot(h, wu_ref[:, c0:c1].astype(BF16))
        a = (g * jax.nn.sigmoid(g) * u).astype(BF16)
        d = _dot(a, wd_ref[c0:c1, :].astype(BF16))
        acc = d if acc is None else acc + d
    y = x + 0.5 * acc
    if final:
        _rows_to_batch(o_ref, _rmsnorm(y, fg_ref[...]), slab_scr, batch)
    else:
        o_ref[...] = y


def _ffn(x, g, wg, wu, wd, layer, batch, first=False, final_g=None):
    final = final_g is not None
    if first:
        _, seq, d = x.shape
        rows = seq * batch
    else:
        rows, d = x.shape
        seq = rows // batch
    d_ff = wg.shape[2]
    tt = ROW_TILE // batch
    row_spec = pl.BlockSpec((ROW_TILE, d), lambda i: (i, 0))
    batch_spec = pl.BlockSpec((batch, tt, d), lambda i: (0, i, 0))
    in_specs = [batch_spec if first else row_spec, _resident((1, d)),
                _resident((d, d_ff), layer), _resident((d, d_ff), layer),
                _resident((d_ff, d), layer)]
    args = [x, g, wg, wu, wd]
    if final:
        in_specs.append(_resident((1, d)))
        args.append(final_g)
    scratch = ([pltpu.VMEM((_slabs(d), ROW_TILE, LANES), F32)]
               if (first or final) else [])
    out_shape = (batch, seq, d) if final else (rows, d)
    return pl.pallas_call(
        functools.partial(_ffn_kernel, first=first, final=final, batch=batch),
        out_shape=jax.ShapeDtypeStruct(out_shape, F32),
        grid=(rows // ROW_TILE,),
        in_specs=in_specs,
        out_specs=batch_spec if final else row_spec,
        scratch_shapes=scratch,
        compiler_params=pltpu.CompilerParams(
            dimension_semantics=("parallel",), vmem_limit_bytes=VMEM_LIMIT),
        name="ffn_first" if first else ("ffn_final" if final else "ffn"),
    )(*args)


ATT_WIDTHS = (D_ATT, 4 * HEAD_DIM, IDX_HEADS * IDX_DIM, LANES, LANES)
ATT_DTYPES = (BF16, BF16, BF16, BF16, F32)


def _mixer_in_kernel(x_ref, g_ref, w_att_ref, w_ref, cw_ref, cb_ref, wa_ref,
                     ba_ref,
                     wx_ref, bx_ref, lam_ref,
                     q_ref, kv_ref, qi_ref, ki_ref, wi_ref, ga_ref, gl_ref,
                     rnn_ref, slab_scr, xs_scr, a_scr, u_scr, h_scr, hc_scr,
                     *, batch):
    tm = x_ref.shape[0]
    c = cw_ref.shape[1]
    d_model = ga_ref.shape[1]
    halo = (CONV_W - 1) * batch
    h = _rmsnorm(x_ref[...], g_ref[...]).astype(BF16)

    @pl.when(pl.program_id(0) == 0)
    def _():
        xs_scr[0:halo, :] = jnp.zeros((halo, c), F32)
        hc_scr[...] = jnp.zeros(hc_scr.shape, F32)

    scales = (HEAD_DIM ** -0.5, None, None, None, (IDX_HEADS * IDX_DIM) ** -0.5)
    att_refs = (q_ref, kv_ref, qi_ref, ki_ref, wi_ref)
    gate_col0 = 2 * c

    def att_operand(i):
        col = sum(ATT_WIDTHS[:i])
        z = _dot(h, w_att_ref[:, col:col + ATT_WIDTHS[i]])
        if scales[i] is not None:
            z = z * scales[i]
        _rows_to_seq(att_refs[i], z, slab_scr, _slabs(sum(ATT_WIDTHS[:i])),
                     batch)

    def gate_logits(o_ref):
        col = gate_col0 if o_ref is ga_ref else gate_col0 + d_model
        o_ref[...] = _dot(h, w_ref[:, col:col + d_model]).astype(o_ref.dtype)

    fillers = (lambda: gate_logits(ga_ref),
               lambda: gate_logits(gl_ref),
               lambda: (att_operand(0), att_operand(1)),
               lambda: (att_operand(2), att_operand(3), att_operand(4)))

    n_tiles = c // MXU_DIM
    assert n_tiles == len(fillers)
    for n in range(n_tiles):
        cs = slice(n * MXU_DIM, (n + 1) * MXU_DIM)
        xr = _dot(h, w_ref[:, cs])
        gr = _dot(h, w_ref[:, c + n * MXU_DIM:c + (n + 1) * MXU_DIM])
        xs_scr[halo:halo + tm, cs] = xr
        xc = cb_ref[:, cs]
        for j in range(CONV_W):
            xc = xc + cw_ref[j:j + 1, cs] * xs_scr[j * batch:j * batch + tm, cs]
        xs_scr[0:halo, cs] = xr[tm - halo:tm, :]

        xb = xc.astype(BF16)
        r = jax.nn.sigmoid(_dot(xb, wa_ref[n]) + ba_ref[:, cs])
        gate_i = jax.nn.sigmoid(_dot(xb, wx_ref[n]) + bx_ref[:, cs])
        fillers[n]()
        neg_lam = -lam_ref[:, cs]
        softplus = (jnp.maximum(neg_lam, 0.0)
                    + jnp.log1p(jnp.exp(-jnp.abs(neg_lam))))
        log_a = -LRU_C * r * softplus
        a_scr[:, cs] = jnp.exp(log_a)
        th = jnp.tanh(log_a)
        u_scr[:, cs] = jnp.sqrt(-2.0 * th / (1.0 - th)) * gate_i * xc

        hcur = hc_scr[:, cs]
        for t in range(tm // batch):
            rs = slice(t * batch, (t + 1) * batch)
            hcur = a_scr[rs, cs] * hcur + u_scr[rs, cs]
            h_scr[rs, cs] = hcur
        hc_scr[:, cs] = hcur
        rnn_ref[:, cs] = (h_scr[:, cs] * jax.nn.gelu(gr)).astype(BF16)


def _mixer_in(x, g, w_att, w, cw, cb, wa, ba, wx, bx, lam, layer, batch):
    rows, d = x.shape
    c = cw.shape[1]
    tm = ROW_TILE
    tt = tm // batch
    seq = rows // batch
    halo = (CONV_W - 1) * batch
    assert w.shape[2] == 2 * c + 2 * d and w_att.shape[2] == sum(ATT_WIDTHS)

    def row_spec(n):
        return pl.BlockSpec((tm, n), lambda i: (i, 0))

    def seq_spec(n):
        return pl.BlockSpec((tt, batch * n), lambda i: (i, 0))

    out_shape = [jax.ShapeDtypeStruct((seq, batch * n), t)
                 for n, t in zip(ATT_WIDTHS, ATT_DTYPES)]
    out_shape += [jax.ShapeDtypeStruct((rows, d), BF16)] * 2
    out_shape += [jax.ShapeDtypeStruct((rows, c), BF16)]
    out_specs = [seq_spec(n) for n in ATT_WIDTHS]
    out_specs += [row_spec(d), row_spec(d), row_spec(c)]
    return pl.pallas_call(
        functools.partial(_mixer_in_kernel, batch=batch),
        out_shape=out_shape,
        grid=(rows // tm,),
        in_specs=[row_spec(d), _resident((1, d)),
                  _resident(w_att.shape[1:], layer),
                  _resident(w.shape[1:], layer), _resident(cw.shape),
                  _resident((1, c)), _resident(wa.shape[1:], layer),
                  _resident((1, c)), _resident(wx.shape[1:], layer),
                  _resident((1, c)), _resident((1, c))],
        out_specs=out_specs,
        scratch_shapes=[
            pltpu.VMEM((_slabs(sum(ATT_WIDTHS)), tm, LANES), F32),
            pltpu.VMEM((halo + tm, c), F32),
            pltpu.VMEM((tm, c), F32), pltpu.VMEM((tm, c), F32),
            pltpu.VMEM((tm, c), F32), pltpu.VMEM((batch, c), F32)],
        compiler_params=pltpu.CompilerParams(
            dimension_semantics=("arbitrary",), vmem_limit_bytes=VMEM_LIMIT),
        name="mixer_in",
    )(x, g, w_att, w, cw, cb, wa, ba, wx, bx, lam)


VT_ROWS = HEAD_DIM + 16


def _key_to_float(key):
    bits = jnp.where(key < 0, INT_MIN - key, key)
    return pltpu.bitcast(bits, F32)


def _for_chunks(n, body):
    def quad(i, carry):
        for j in range(4):
            body(4 * i + j)
        return carry

    lax.fori_loop(0, n // 4, quad, 0)
    base = (n // 4) * 4

    @pl.when(n % 4 >= 2)
    def _():
        body(base)
        body(base + 1)

    @pl.when(n % 2 == 1)
    def _():
        body(n - 1)


def _fold_chunks(n, chunk, init):
    def pair(i, c):
        return chunk(2 * i + 1, chunk(2 * i, c))

    c = lax.fori_loop(0, n // 2, pair, init)
    return lax.cond(n % 2 == 1, lambda c: chunk(n - 1, c), lambda c: c, c)


def _attn_kernel(q_ref, qi_ref, wi_ref, kv_ref, ki_ref, o_ref,
                 qm_scr, qim_scr, wit_scr, vt_scr, sc_scr, scb_scr, sall_scr, m_scr,
                 acc_scr, mpart_scr, ties_scr, *, topk):
    tq, ck = Q_TILE, K_CHUNK
    n_kc = sc_scr.shape[0]
    seq = n_kc * ck
    qt = pl.program_id(1)
    n_chunks = (qt * tq) // ck + tq // ck
    lane = lax.broadcasted_iota(I32, (tq, LANES), 1)
    key_iota = lax.broadcasted_iota(I32, (ck, tq), 0)
    qry_iota = lax.broadcasted_iota(I32, (ck, tq), 1)
    kf = float(topk)

    for h in range(N_HEADS):
        qp = q_ref[:, (h // 2) * LANES:(h // 2 + 1) * LANES].astype(F32)
        lo = (h % 2) * HEAD_DIM
        qm_scr[h] = jnp.where((lane >= lo) & (lane < lo + HEAD_DIM), qp,
                              0.0).astype(BF16)
    for h in range(IDX_HEADS):
        qp = qi_ref[:, (h // 4) * LANES:(h // 4 + 1) * LANES].astype(F32)
        lo = (h % 4) * IDX_DIM
        qim_scr[h] = jnp.where((lane >= lo) & (lane < lo + IDX_DIM), qp,
                               0.0).astype(BF16)
    wit_scr[...] = jnp.transpose(wi_ref[...])

    @pl.when(qt == 0)
    def _():
        lane_c = lax.broadcasted_iota(I32, (ck, LANES), 1)
        for c in range(n_kc):
            v1 = jnp.where(lane_c < HEAD_DIM,
                           kv_ref[c * ck:(c + 1) * ck, LANES:2 * LANES].astype(F32),
                           1.0)
            vt_scr[c] = jnp.transpose(v1)[0:VT_ROWS, :].astype(BF16)

    def score_chunk(kc):
        k0 = pl.multiple_of(kc * ck, ck)
        kic = ki_ref[pl.ds(k0, ck), :]
        acc = jnp.zeros((ck, tq), F32)
        for h in range(IDX_HEADS):
            acc = acc + (jnp.maximum(_dot_nt(kic, qim_scr[h]), 0.0)
                         * wit_scr[h:h + 1, :])
        causal = key_iota - qry_iota <= qt * tq - kc * ck
        sc = jnp.where(causal, acc, -jnp.inf)
        sc_scr[kc] = sc
        scb_scr[kc] = sc.astype(BF16)

    _for_chunks(n_chunks, score_chunk)

    def count(pred):
        acc_rows = 2 * SUBLANES

        def chunk(kc, c):
            hit = jnp.where(pred(sc_scr[kc], kc), 1.0, 0.0)
            return c + jnp.sum(hit.reshape(ck // acc_rows, acc_rows, tq), axis=0)

        c = _fold_chunks(n_chunks, chunk, jnp.zeros((acc_rows, tq), F32))
        return jnp.sum(c, axis=0, keepdims=True)

    def count_bf16(cand):
        acc_rows = 2 * BF16_SUBLANES
        one, zero = jnp.ones((), BF16), jnp.zeros((), BF16)

        def chunk(kc, c):
            hit = jnp.where(scb_scr[kc] >= cand, one, zero)
            part = hit[0:acc_rows]
            for r0 in range(acc_rows, ck, acc_rows):
                part = part + hit[r0:r0 + acc_rows]
            return c + part.astype(F32)

        c = _fold_chunks(n_chunks, chunk, jnp.zeros((acc_rows, tq), F32))
        return jnp.sum(c, axis=0, keepdims=True)

    def coarse_step(i, thr16):
        cand16 = thr16 + lax.shift_left(jnp.int32(1), 15 - i)
        cand_f = _key_to_float(cand16 * 65536)
        tot = count_bf16(cand_f.astype(BF16))
        return jnp.where(tot >= kf, cand16, thr16)

    thr16 = lax.fori_loop(0, 16, coarse_step,
                          jnp.full((1, tq), -2 ** 15, I32))

    def fine_step(i, carry):
        thr, above = carry
        cand = thr + lax.shift_left(jnp.int32(1), 16 - i)
        cand_f = _key_to_float(cand)
        tot = count(lambda s, kc: s >= cand_f)
        ok = tot >= kf
        return jnp.where(ok, cand, thr), jnp.where(ok, above, tot)

    thr, above = lax.fori_loop(
        0, 17, fine_step,
        ((thr16 - 1) * 65536, jnp.full((1, tq), -1.0, F32)))

    thr_f = _key_to_float(thr)
    short = jnp.logical_not(thr_f > -jnp.inf)
    thr_f = jnp.where(short, -jnp.inf, thr_f)
    n_gt = lax.cond(jnp.min(above) < 0.0,
                    lambda: count(lambda s, kc: s > thr_f), lambda: above)
    need = jnp.where(short, 0.0, kf - n_gt)
    lower_tri = jnp.where(
        lax.broadcasted_iota(I32, (ck, ck), 0)
        >= lax.broadcasted_iota(I32, (ck, ck), 1), 1.0, 0.0).astype(BF16)

    mpart_scr[...] = jnp.full(mpart_scr.shape, MASK_BIAS, F32)
    acc_scr[...] = jnp.zeros(acc_scr.shape, F32)

    ties_scr[...] = jnp.zeros(ties_scr.shape, F32)

    def scores_chunk(kc):
        k0 = pl.multiple_of(kc * ck, ck)
        s_idx = sc_scr[kc]
        tie = s_idx == thr_f
        tie_f = jnp.where(tie, 1.0, 0.0)
        ties_before = ties_scr[...]
        ties_scr[...] = ties_before + jnp.sum(tie_f, axis=0, keepdims=True)
        rank = ties_before + _dot(lower_tri, tie_f.astype(BF16))
        keep = (s_idx > thr_f) | (tie & (rank <= need))
        bias = jnp.where(keep, 0.0, MASK_BIAS)
        for h in range(N_HEADS):
            c0 = 0 if h % 2 == 0 else LANES
            k_mat = kv_ref[pl.ds(k0, ck), c0:c0 + LANES]
            s = _dot_nt(k_mat, qm_scr[h]) + bias
            sall_scr[kc, h] = s
            mpart_scr[h] = jnp.maximum(
                mpart_scr[h],
                jnp.max(s.reshape(ck // SUBLANES, SUBLANES, tq), axis=0))

    _for_chunks(n_chunks, scores_chunk)
    for h in range(N_HEADS):
        m_scr[h:h + 1, :] = jnp.max(mpart_scr[h], axis=0, keepdims=True)

    def value_chunk(kc):
        vt_c = vt_scr[kc]
        for h in range(N_HEADS):
            p = jnp.exp(sall_scr[kc, h] - m_scr[h:h + 1, :]).astype(BF16)
            acc_scr[h] = acc_scr[h] + _dot(vt_c, p)

    _for_chunks(n_chunks, value_chunk)

    outs = []
    for h in range(N_HEADS):
        a = acc_scr[h]
        outs.append(a[0:HEAD_DIM, :] / a[HEAD_DIM:HEAD_DIM + 1, :])
    o_ref[...] = jnp.transpose(jnp.concatenate(outs, axis=0)).astype(BF16)


def _attention(q, qi, wi, kv, ki, batch):
    tq, ck = Q_TILE, K_CHUNK
    seq = q.shape[0]
    topk = min(TOPK_MAX, seq // 4)
    nq = seq // tq

    def q_spec(c):
        return pl.BlockSpec((tq, c), lambda b, i: (i, b))

    def kv_spec(c):
        return pl.BlockSpec((seq, c), lambda b, i: (0, b))

    return pl.pallas_call(
        functools.partial(_attn_kernel, topk=topk),
        out_shape=jax.ShapeDtypeStruct((seq, batch * D_ATT), BF16),
        grid=(batch, nq),
        in_specs=[q_spec(D_ATT), q_spec(IDX_HEADS * IDX_DIM), q_spec(LANES),
                  kv_spec(4 * HEAD_DIM), kv_spec(LANES)],
        out_specs=q_spec(D_ATT),
        scratch_shapes=[
            pltpu.VMEM((N_HEADS, tq, LANES), BF16),
            pltpu.VMEM((IDX_HEADS, tq, LANES), BF16),
            pltpu.VMEM((LANES, tq), F32),
            pltpu.VMEM((seq // ck, VT_ROWS, ck), BF16),
            pltpu.VMEM((seq // ck, ck, tq), F32),
            pltpu.VMEM((seq // ck, ck, tq), BF16),
            pltpu.VMEM((seq // ck, N_HEADS, ck, tq), F32),
            pltpu.VMEM((N_HEADS, tq), F32),
            pltpu.VMEM((N_HEADS, VT_ROWS, tq), F32),
            pltpu.VMEM((N_HEADS, SUBLANES, tq), F32),
            pltpu.VMEM((1, tq), F32),
        ],
        compiler_params=pltpu.CompilerParams(
            dimension_semantics=("parallel", "arbitrary"),
            vmem_limit_bytes=VMEM_LIMIT),
        name="dsa_attention",
    )(q, qi, wi, kv, ki)


def _merge_kernel(att_ref, rnn_ref, ga_ref, gl_ref, x_ref, wa_ref, wr_ref,
                  wo_ref, o_ref, slab_scr, *, batch):
    att = _seq_to_rows(att_ref, slab_scr, batch).astype(BF16)
    pa = _dot(att, wa_ref[...])
    pr = _dot(rnn_ref[...], wr_ref[...])
    merged = (jax.nn.sigmoid(ga_ref[...].astype(F32)) * pa
              + jax.nn.sigmoid(gl_ref[...].astype(F32)) * pr)
    o_ref[...] = x_ref[...] + _dot(merged.astype(BF16), wo_ref[...])


def _merge(att, rnn, ga, gl, x, wa, wr, wo, layer, batch):
    rows, d = x.shape
    tt = MERGE_ROW_TILE // batch

    def row_spec(c):
        return pl.BlockSpec((MERGE_ROW_TILE, c), lambda i: (i, 0))

    return pl.pallas_call(
        functools.partial(_merge_kernel, batch=batch),
        out_shape=jax.ShapeDtypeStruct((rows, d), F32),
        grid=(rows // MERGE_ROW_TILE,),
        in_specs=[pl.BlockSpec((tt, att.shape[1]), lambda i: (i, 0)),
                  row_spec(rnn.shape[1]), row_spec(d), row_spec(d),
                  row_spec(d), _resident(wa.shape[1:], layer),
                  _resident(wr.shape[1:], layer),
                  _resident(wo.shape[1:], layer)],
        out_specs=row_spec(d),
        scratch_shapes=[pltpu.VMEM((_slabs(D_ATT), MERGE_ROW_TILE, LANES), F32)],
        compiler_params=pltpu.CompilerParams(
            dimension_semantics=("parallel",), vmem_limit_bytes=VMEM_LIMIT),
        name="merge",
    )(att, rnn, ga, gl, x, wa, wr, wo)


def _inproj_weights(w_in, d_rnn, d_model):
    splits = (D_ATT, HEAD_DIM, HEAD_DIM, IDX_HEADS * IDX_DIM, IDX_DIM,
              IDX_HEADS)
    parts, c = [], 0
    for n in splits:
        parts.append(w_in[..., c:c + n])
        c += n
    assert c + 2 * d_rnn + 2 * d_model == w_in.shape[-1]
    wq, wk, wv, wqi, wki, wwi = parts
    wi_pad = jnp.zeros(w_in.shape[:-1] + (LANES - IDX_HEADS,), w_in.dtype)
    cols = [wq, wk, wv, wv, wk, wqi] + [wki] * (LANES // IDX_DIM)
    cols += [wwi, wi_pad]
    return (jnp.concatenate(cols, axis=-1).astype(BF16),
            w_in[..., c:].astype(BF16))


def _block_diag_tiles(w):
    depth, n_blocks, bw, _ = w.shape
    per = MXU_DIM // bw
    w = w.reshape(depth, n_blocks // per, per, bw, bw)
    eye = jnp.eye(per, dtype=w.dtype)
    t = w[:, :, :, :, None, :] * eye[None, None, :, None, :, None]
    return t.reshape(depth, n_blocks // per, MXU_DIM, MXU_DIM).astype(BF16)


def kernel(x, ffn1_norm, ffn1_wg, ffn1_wu, ffn1_wd, mix_norm, w_in, conv_w,
           conv_b, rg_wa, rg_ba, rg_wx, rg_bx, rg_lam, w_att_proj, w_rnn_proj,
           w_out, ffn2_norm, ffn2_wg, ffn2_wu, ffn2_wd, final_norm):
    batch, seq, d = x.shape
    depth = ffn1_norm.shape[0]
    d_rnn = conv_w.shape[-1]
    assert batch == SUBLANES and seq % Q_TILE == 0
    assert (batch * seq) % MERGE_ROW_TILE == 0 and ROW_TILE % batch == 0

    def row(v):
        return v.reshape(1, -1)

    wg1, wu1, wd1 = ffn1_wg, ffn1_wu, ffn1_wd
    wg2, wu2, wd2 = ffn2_wg, ffn2_wu, ffn2_wd
    w_att, w_wide = _inproj_weights(w_in, d_rnn, d)
    wa_t, wx_t = _block_diag_tiles(rg_wa), _block_diag_tiles(rg_wx)
    w_ap, w_rp, w_o = (w.astype(BF16) for w in (w_att_proj, w_rnn_proj, w_out))

    h = x
    for l in range(depth):
        h = _ffn(h, row(ffn1_norm[l]), wg1, wu1, wd1, l, batch, first=(l == 0))
        q, kv, qi, ki, wi, ga, gl, rnn = _mixer_in(
            h, row(mix_norm[l]), w_att, w_wide, conv_w[l], row(conv_b[l]), wa_t,
            row(rg_ba[l]), wx_t, row(rg_bx[l]), row(rg_lam[l]), l, batch)
        att = _attention(q, qi, wi, kv, ki, batch)
        h = _merge(att, rnn, ga, gl, h, w_ap, w_rp, w_o, l, batch)
        last = l == depth - 1
        h = _ffn(h, row(ffn2_norm[l]), wg2, wu2, wd2, l, batch,
                 final_g=row(final_norm) if last else None)
    return h
```

```python
import functools

import jax
import jax.numpy as jnp
from jax import lax
from jax.experimental import pallas as pl
from jax.experimental.pallas import tpu as pltpu

F32 = jnp.float32
BF16 = jnp.bfloat16
I32 = jnp.int32

N_HEADS = 8
HEAD_DIM = 64
D_ATT = N_HEADS * HEAD_DIM
IDX_HEADS = 8
IDX_DIM = 32
TOPK_MAX = 256
CONV_W = 4
LRU_C = 8.0
EPS = 1e-6

LANES = 128
SUBLANES = 8
BF16_SUBLANES = 16
MXU_DIM = 256
VMEM_LIMIT = 56 * 1024 * 1024

INT_MIN = -2 ** 31
MASK_BIAS = -1e30

ROW_TILE = 512
MERGE_ROW_TILE = 1024
Q_TILE = 256
K_CHUNK = 256


def _resident(shape, layer=None):
    zeros = (0,) * len(shape)
    if layer is None:
        return pl.BlockSpec(shape, lambda *_: zeros,
                            pipeline_mode=pl.Buffered(1))
    return pl.BlockSpec((None,) + tuple(shape), lambda *_: (layer,) + zeros,
                        pipeline_mode=pl.Buffered(1))


def _rmsnorm(x, g):
    ms = jnp.mean(x * x, axis=-1, keepdims=True)
    return x * lax.rsqrt(ms + EPS) * g


def _dot(a, b):
    return jnp.dot(a, b, preferred_element_type=F32)


def _dot_nt(a, b):
    return lax.dot_general(a, b, (((1,), (1,)), ((), ())),
                           preferred_element_type=F32)


def _ff_chunks(d_ff):
    chunks, c0 = [], 0
    while c0 < d_ff:
        c1 = min(c0 + 2 * MXU_DIM, d_ff)
        chunks.append((c0, c1))
        c0 = c1
    return chunks


def _slabs(width):
    return width // LANES


def _batch_to_rows(x_ref, slab_scr, batch):
    tt, d = x_ref.shape[1:]
    for b in range(batch):
        for j in range(_slabs(d)):
            slab_scr[j, pl.ds(b, tt, stride=batch), :] = (
                x_ref[b, :, j * LANES:(j + 1) * LANES])
    return jnp.concatenate([slab_scr[j] for j in range(_slabs(d))], axis=1)


def _rows_to_batch(o_ref, y, slab_scr, batch):
    tt, d = o_ref.shape[1:]
    for j in range(_slabs(d)):
        slab_scr[j] = y[:, j * LANES:(j + 1) * LANES]
    for b in range(batch):
        for j in range(_slabs(d)):
            o_ref[b, :, j * LANES:(j + 1) * LANES] = (
                slab_scr[j, pl.ds(b, tt, stride=batch), :])


def _rows_to_seq(o_ref, z, slab_scr, slab0, batch):
    tt = o_ref.shape[0]
    c = z.shape[1]
    for j in range(_slabs(c)):
        slab_scr[slab0 + j] = z[:, j * LANES:(j + 1) * LANES]
    for b in range(batch):
        for j in range(_slabs(c)):
            o_ref[:, b * c + j * LANES:b * c + (j + 1) * LANES] = (
                slab_scr[slab0 + j, pl.ds(b, tt, stride=batch), :]
                .astype(o_ref.dtype))


def _seq_to_rows(x_ref, slab_scr, batch):
    tt = x_ref.shape[0]
    c = x_ref.shape[1] // batch
    for b in range(batch):
        for j in range(_slabs(c)):
            slab_scr[j, pl.ds(b, tt, stride=batch), :] = (
                x_ref[:, b * c + j * LANES:b * c + (j + 1) * LANES].astype(F32))
    return jnp.concatenate([slab_scr[j] for j in range(_slabs(c))], axis=1)


def _ffn_kernel(x_ref, g_ref, wg_ref, wu_ref, wd_ref, *rest, first, final,
                batch):
    rest = list(rest)
    fg_ref = rest.pop(0) if final else None
    o_ref = rest.pop(0)
    slab_scr = rest.pop(0) if (first or final) else None
    x = _batch_to_rows(x_ref, slab_scr, batch) if first else x_ref[...]
    h = _rmsnorm(x, g_ref[...]).astype(BF16)
    acc = None
    for c0, c1 in _ff_chunks(wg_ref.shape[1]):
        g = _dot(h, wg_ref[:, c0:c1].astype(BF16))
        u = _dot(h, wu_ref[:, c0:c1].astype(BF16))
        a = (g * jax.nn.sigmoid(g) * u).astype(BF16)
        d = _dot(a, wd_ref[c0:c1, :].astype(BF16))
        acc = d if acc is None else acc + d
    y = x + 0.5 * acc
    if final:
        _rows_to_batch(o_ref, _rmsnorm(y, fg_ref[...]), slab_scr, batch)
    else:
        o_ref[...] = y


def _ffn(x, g, wg, wu, wd, layer, batch, first=False, final_g=None):
    final = final_g is not None
    if first:
        _, seq, d = x.shape
        rows = seq * batch
    else:
        rows, d = x.shape
        seq = rows // batch
    d_ff = wg.shape[2]
    tt = ROW_TILE // batch
    row_spec = pl.BlockSpec((ROW_TILE, d), lambda i: (i, 0))
    batch_spec = pl.BlockSpec((batch, tt, d), lambda i: (0, i, 0))
    in_specs = [batch_spec if first else row_spec, _resident((1, d)),
                _resident((d, d_ff), layer), _resident((d, d_ff), layer),
                _resident((d_ff, d), layer)]
    args = [x, g, wg, wu, wd]
    if final:
        in_specs.append(_resident((1, d)))
        args.append(final_g)
    scratch = ([pltpu.VMEM((_slabs(d), ROW_TILE, LANES), F32)]
               if (first or final) else [])
    out_shape = (batch, seq, d) if final else (rows, d)
    return pl.pallas_call(
        functools.partial(_ffn_kernel, first=first, final=final, batch=batch),
        out_shape=jax.ShapeDtypeStruct(out_shape, F32),
        grid=(rows // ROW_TILE,),
        in_specs=in_specs,
        out_specs=batch_spec if final else row_spec,
        scratch_shapes=scratch,
        compiler_params=pltpu.CompilerParams(
            dimension_semantics=("parallel",), vmem_limit_bytes=VMEM_LIMIT),
        name="ffn_first" if first else ("ffn_final" if final else "ffn"),
    )(*args)


ATT_WIDTHS = (D_ATT, 4 * HEAD_DIM, IDX_HEADS * IDX_DIM, LANES, LANES)
ATT_DTYPES = (BF16, BF16, BF16, BF16, F32)


def _mixer_in_kernel(x_ref, g_ref, w_att_ref, w_ref, cw_ref, cb_ref, wa_ref,
                     ba_ref,
                     wx_ref, bx_ref, lam_ref,
                     q_ref, kv_ref, qi_ref, ki_ref, wi_ref, ga_ref, gl_ref,
                     rnn_ref, slab_scr, xs_scr, a_scr, u_scr, h_scr, hc_scr,
                     *, batch):
    tm = x_ref.shape[0]
    c = cw_ref.shape[1]
    d_model = ga_ref.shape[1]
    halo = (CONV_W - 1) * batch
    h = _rmsnorm(x_ref[...], g_ref[...]).astype(BF16)

    @pl.when(pl.program_id(0) == 0)
    def _():
        xs_scr[0:halo, :] = jnp.zeros((halo, c), F32)
        hc_scr[...] = jnp.zeros(hc_scr.shape, F32)

    scales = (HEAD_DIM ** -0.5, None, None, None, (IDX_HEADS * IDX_DIM) ** -0.5)
    att_refs = (q_ref, kv_ref, qi_ref, ki_ref, wi_ref)
    gate_col0 = 2 * c

    def att_operand(i):
        col = sum(ATT_WIDTHS[:i])
        z = _dot(h, w_att_ref[:, col:col + ATT_WIDTHS[i]])
        if scales[i] is not None:
            z = z * scales[i]
        _rows_to_seq(att_refs[i], z, slab_scr, _slabs(sum(ATT_WIDTHS[:i])),
                     batch)

    def gate_logits(o_ref):
        col = gate_col0 if o_ref is ga_ref else gate_col0 + d_model
        o_ref[...] = _dot(
            h, w_ref[:, col:col + d_model].astype(BF16)).astype(o_ref.dtype)

    fillers = (lambda: gate_logits(ga_ref),
               lambda: gate_logits(gl_ref),
               lambda: (att_operand(0), att_operand(1)),
               lambda: (att_operand(2), att_operand(3), att_operand(4)))

    n_tiles = c // MXU_DIM
    assert n_tiles == len(fillers)
    for n in range(n_tiles):
        cs = slice(n * MXU_DIM, (n + 1) * MXU_DIM)
        xr = _dot(h, w_ref[:, cs].astype(BF16))
        gr = _dot(h, w_ref[:, c + n * MXU_DIM:c + (n + 1) * MXU_DIM]
                  .astype(BF16))
        xs_scr[halo:halo + tm, cs] = xr
        xc = cb_ref[:, cs]
        for j in range(CONV_W):
            xc = xc + cw_ref[j:j + 1, cs] * xs_scr[j * batch:j * batch + tm, cs]
        xs_scr[0:halo, cs] = xr[tm - halo:tm, :]

        xb = xc.astype(BF16)
        r = jax.nn.sigmoid(_dot(xb, wa_ref[n]) + ba_ref[:, cs])
        gate_i = jax.nn.sigmoid(_dot(xb, wx_ref[n]) + bx_ref[:, cs])
        fillers[n]()
        neg_lam = -lam_ref[:, cs]
        softplus = (jnp.maximum(neg_lam, 0.0)
                    + jnp.log1p(jnp.exp(-jnp.abs(neg_lam))))
        log_a = -LRU_C * r * softplus
        a_scr[:, cs] = jnp.exp(log_a)
        th = jnp.tanh(log_a)
        u_scr[:, cs] = jnp.sqrt(-2.0 * th / (1.0 - th)) * gate_i * xc

        hcur = hc_scr[:, cs]
        for t in range(tm // batch):
            rs = slice(t * batch, (t + 1) * batch)
            hcur = a_scr[rs, cs] * hcur + u_scr[rs, cs]
            h_scr[rs, cs] = hcur
        hc_scr[:, cs] = hcur
        rnn_ref[:, cs] = (h_scr[:, cs] * jax.nn.gelu(gr)).astype(BF16)


def _mixer_in(x, g, w_att, w, cw, cb, wa, ba, wx, bx, lam, layer, batch):
    rows, d = x.shape
    c = cw.shape[1]
    tm = ROW_TILE
    tt = tm // batch
    seq = rows // batch
    halo = (CONV_W - 1) * batch
    assert w.shape[2] == 2 * c + 2 * d and w_att.shape[2] == sum(ATT_WIDTHS)

    def row_spec(n):
        return pl.BlockSpec((tm, n), lambda i: (i, 0))

    def seq_spec(n):
        return pl.BlockSpec((tt, batch * n), lambda i: (i, 0))

    out_shape = [jax.ShapeDtypeStruct((seq, batch * n), t)
                 for n, t in zip(ATT_WIDTHS, ATT_DTYPES)]
    out_shape += [jax.ShapeDtypeStruct((rows, d), BF16)] * 2
    out_shape += [jax.ShapeDtypeStruct((rows, c), BF16)]
    out_specs = [seq_spec(n) for n in ATT_WIDTHS]
    out_specs += [row_spec(d), row_spec(d), row_spec(c)]
    return pl.pallas_call(
        functools.partial(_mixer_in_kernel, batch=batch),
        out_shape=out_shape,
        grid=(rows // tm,),
        in_specs=[row_spec(d), _resident((1, d)),
                  _resident(w_att.shape[1:], layer),
                  _resident(w.shape[1:], layer), _resident(cw.shape),
                  _resident((1, c)), _resident(wa.shape[1:], layer),
                  _resident((1, c)), _resident(wx.shape[1:], layer),
                  _resident((1, c)), _resident((1, c))],
        out_specs=out_specs,
        scratch_shapes=[
            pltpu.VMEM((_slabs(sum(ATT_WIDTHS)), tm, LANES), F32),
            pltpu.VMEM((halo + tm, c), F32),
            pltpu.VMEM((tm, c), F32), pltpu.VMEM((tm, c), F32),
            pltpu.VMEM((tm, c), F32), pltpu.VMEM((batch, c), F32)],
        compiler_params=pltpu.CompilerParams(
            dimension_semantics=("arbitrary",), vmem_limit_bytes=VMEM_LIMIT),
        name="mixer_in",
    )(x, g, w_att, w, cw, cb, wa, ba, wx, bx, lam)


VT_ROWS = HEAD_DIM + 16


def _key_to_float(key):
    bits = jnp.where(key < 0, INT_MIN - key, key)
    return pltpu.bitcast(bits, F32)


def _for_chunks(n, body):
    def quad(i, carry):
        for j in range(4):
            body(4 * i + j)
        return carry

    lax.fori_loop(0, n // 4, quad, 0)
    base = (n // 4) * 4

    @pl.when(n % 4 >= 2)
    def _():
        body(base)
        body(base + 1)

    @pl.when(n % 2 == 1)
    def _():
        body(n - 1)


def _fold_chunks(n, chunk, init):
    def pair(i, c):
        return chunk(2 * i + 1, chunk(2 * i, c))

    c = lax.fori_loop(0, n // 2, pair, init)
    return lax.cond(n % 2 == 1, lambda c: chunk(n - 1, c), lambda c: c, c)


def _attn_kernel(q_ref, qi_ref, wi_ref, kv_ref, ki_ref, o_ref,
                 qm_scr, qim_scr, wit_scr, vt_scr, sc_scr, scb_scr, sall_scr, m_scr,
                 acc_scr, mpart_scr, ties_scr, *, topk):
    tq, ck = Q_TILE, K_CHUNK
    n_kc = sc_scr.shape[0]
    seq = n_kc * ck
    qt = pl.program_id(1)
    n_chunks = (qt * tq) // ck + tq // ck
    lane = lax.broadcasted_iota(I32, (tq, LANES), 1)
    key_iota = lax.broadcasted_iota(I32, (ck, tq), 0)
    qry_iota = lax.broadcasted_iota(I32, (ck, tq), 1)
    kf = float(topk)

    for h in range(N_HEADS):
        qp = q_ref[:, (h // 2) * LANES:(h // 2 + 1) * LANES].astype(F32)
        lo = (h % 2) * HEAD_DIM
        qm_scr[h] = jnp.where((lane >= lo) & (lane < lo + HEAD_DIM), qp,
                              0.0).astype(BF16)
    for h in range(IDX_HEADS):
        qp = qi_ref[:, (h // 4) * LANES:(h // 4 + 1) * LANES].astype(F32)
        lo = (h % 4) * IDX_DIM
        qim_scr[h] = jnp.where((lane >= lo) & (lane < lo + IDX_DIM), qp,
                               0.0).astype(BF16)
    wit_scr[...] = jnp.transpose(wi_ref[...])

    @pl.when(qt == 0)
    def _():
        lane_c = lax.broadcasted_iota(I32, (ck, LANES), 1)
        for c in range(n_kc):
            v1 = jnp.where(lane_c < HEAD_DIM,
                           kv_ref[c * ck:(c + 1) * ck, LANES:2 * LANES].astype(F32),
                           1.0)
            vt_scr[c] = jnp.transpose(v1)[0:VT_ROWS, :].astype(BF16)

    def score_chunk(kc):
        k0 = pl.multiple_of(kc * ck, ck)
        kic = ki_ref[pl.ds(k0, ck), :]
        acc = jnp.zeros((ck, tq), F32)
        for h in range(IDX_HEADS):
            acc = acc + (jnp.maximum(_dot_nt(kic, qim_scr[h]), 0.0)
                         * wit_scr[h:h + 1, :])
        causal = key_iota - qry_iota <= qt * tq - kc * ck
        sc = jnp.where(causal, acc, -jnp.inf)
        sc_scr[kc] = sc
        scb_scr[kc] = sc.astype(BF16)

    _for_chunks(n_chunks, score_chunk)

    def count(pred):
        acc_rows = 2 * SUBLANES

        def chunk(kc, c):
            hit = jnp.where(pred(sc_scr[kc], kc), 1.0, 0.0)
            return c + jnp.sum(hit.reshape(ck // acc_rows, acc_rows, tq), axis=0)

        c = _fold_chunks(n_chunks, chunk, jnp.zeros((acc_rows, tq), F32))
        return jnp.sum(c, axis=0, keepdims=True)

    def count_bf16(cand):
        acc_rows = 2 * BF16_SUBLANES
        one, zero = jnp.ones((), BF16), jnp.zeros((), BF16)

        def chunk(kc, c):
            hit = jnp.where(scb_scr[kc] >= cand, one, zero)
            part = hit[0:acc_rows]
            for r0 in range(acc_rows, ck, acc_rows):
                part = part + hit[r0:r0 + acc_rows]
            return c + part.astype(F32)

        c = _fold_chunks(n_chunks, chunk, jnp.zeros((acc_rows, tq), F32))
        return jnp.sum(c, axis=0, keepdims=True)

    def coarse_step(i, thr16):
        cand16 = thr16 + lax.shift_left(jnp.int32(1), 15 - i)
        cand_f = _key_to_float(cand16 * 65536)
        tot = count_bf16(cand_f.astype(BF16))
        return jnp.where(tot >= kf, cand16, thr16)

    thr16 = lax.fori_loop(0, 16, coarse_step,
                          jnp.full((1, tq), -2 ** 15, I32))

    def fine_step(i, carry):
        thr, above = carry
        cand = thr + lax.shift_left(jnp.int32(1), 16 - i)
        cand_f = _key_to_float(cand)
        tot = count(lambda s, kc: s >= cand_f)
        ok = tot >= kf
        return jnp.where(ok, cand, thr), jnp.where(ok, above, tot)

    thr, above = lax.fori_loop(
        0, 17, fine_step,
        ((thr16 - 1) * 65536, jnp.full((1, tq), -1.0, F32)))

    thr_f = _key_to_float(thr)
    short = jnp.logical_not(thr_f > -jnp.inf)
    thr_f = jnp.where(short, -jnp.inf, thr_f)
    n_gt = lax.cond(jnp.min(above) < 0.0,
                    lambda: count(lambda s, kc: s > thr_f), lambda: above)
    need = jnp.where(short, 0.0, kf - n_gt)
    lower_tri = jnp.where(
        lax.broadcasted_iota(I32, (ck, ck), 0)
        >= lax.broadcasted_iota(I32, (ck, ck), 1), 1.0, 0.0).astype(BF16)

    mpart_scr[...] = jnp.full(mpart_scr.shape, MASK_BIAS, F32)
    acc_scr[...] = jnp.zeros(acc_scr.shape, F32)

    ties_scr[...] = jnp.zeros(ties_scr.shape, F32)

    def scores_chunk(kc):
        k0 = pl.multiple_of(kc * ck, ck)
        s_idx = sc_scr[kc]
        tie = s_idx == thr_f
        tie_f = jnp.where(tie, 1.0, 0.0)
        ties_before = ties_scr[...]
        ties_scr[...] = ties_before + jnp.sum(tie_f, axis=0, keepdims=True)
        rank = ties_before + _dot(lower_tri, tie_f.astype(BF16))
        keep = (s_idx > thr_f) | (tie & (rank <= need))
        bias = jnp.where(keep, 0.0, MASK_BIAS)
        for h in range(N_HEADS):
            c0 = 0 if h % 2 == 0 else LANES
            k_mat = kv_ref[pl.ds(k0, ck), c0:c0 + LANES]
            s = _dot_nt(k_mat, qm_scr[h]) + bias
            sall_scr[kc, h] = s
            mpart_scr[h] = jnp.maximum(
                mpart_scr[h],
                jnp.max(s.reshape(ck // SUBLANES, SUBLANES, tq), axis=0))

    _for_chunks(n_chunks, scores_chunk)
    for h in range(N_HEADS):
        m_scr[h:h + 1, :] = jnp.max(mpart_scr[h], axis=0, keepdims=True)

    def value_chunk(kc):
        vt_c = vt_scr[kc]
        for h in range(N_HEADS):
            p = jnp.exp(sall_scr[kc, h] - m_scr[h:h + 1, :]).astype(BF16)
            acc_scr[h] = acc_scr[h] + _dot(vt_c, p)

    _for_chunks(n_chunks, value_chunk)

    outs = []
    for h in range(N_HEADS):
        a = acc_scr[h]
        outs.append(a[0:HEAD_DIM, :] / a[HEAD_DIM:HEAD_DIM + 1, :])
    o_ref[...] = jnp.transpose(jnp.concatenate(outs, axis=0)).astype(BF16)


def _attention(q, qi, wi, kv, ki, batch):
    tq, ck = Q_TILE, K_CHUNK
    seq = q.shape[0]
    topk = min(TOPK_MAX, seq // 4)
    nq = seq // tq

    def q_spec(c):
        return pl.BlockSpec((tq, c), lambda b, i: (i, b))

    def kv_spec(c):
        return pl.BlockSpec((seq, c), lambda b, i: (0, b))

    return pl.pallas_call(
        functools.partial(_attn_kernel, topk=topk),
        out_shape=jax.ShapeDtypeStruct((seq, batch * D_ATT), BF16),
        grid=(batch, nq),
        in_specs=[q_spec(D_ATT), q_spec(IDX_HEADS * IDX_DIM), q_spec(LANES),
                  kv_spec(4 * HEAD_DIM), kv_spec(LANES)],
        out_specs=q_spec(D_ATT),
        scratch_shapes=[
            pltpu.VMEM((N_HEADS, tq, LANES), BF16),
            pltpu.VMEM((IDX_HEADS, tq, LANES), BF16),
            pltpu.VMEM((LANES, tq), F32),
            pltpu.VMEM((seq // ck, VT_ROWS, ck), BF16),
            pltpu.VMEM((seq // ck, ck, tq), F32),
            pltpu.VMEM((seq // ck, ck, tq), BF16),
            pltpu.VMEM((seq // ck, N_HEADS, ck, tq), F32),
            pltpu.VMEM((N_HEADS, tq), F32),
            pltpu.VMEM((N_HEADS, VT_ROWS, tq), F32),
            pltpu.VMEM((N_HEADS, SUBLANES, tq), F32),
            pltpu.VMEM((1, tq), F32),
        ],
        compiler_params=pltpu.CompilerParams(
            dimension_semantics=("parallel", "arbitrary"),
            vmem_limit_bytes=VMEM_LIMIT),
        name="dsa_attention",
    )(q, qi, wi, kv, ki)


def _merge_kernel(att_ref, rnn_ref, ga_ref, gl_ref, x_ref, wa_ref, wr_ref,
                  wo_ref, o_ref, slab_scr, *, batch):
    att = _seq_to_rows(att_ref, slab_scr, batch).astype(BF16)
    pa = _dot(att, wa_ref[...].astype(BF16))
    pr = _dot(rnn_ref[...], wr_ref[...].astype(BF16))
    merged = (jax.nn.sigmoid(ga_ref[...].astype(F32)) * pa
              + jax.nn.sigmoid(gl_ref[...].astype(F32)) * pr)
    o_ref[...] = x_ref[...] + _dot(merged.astype(BF16),
                                   wo_ref[...].astype(BF16))


def _merge(att, rnn, ga, gl, x, wa, wr, wo, layer, batch):
    rows, d = x.shape
    tt = MERGE_ROW_TILE // batch

    def row_spec(c):
        return pl.BlockSpec((MERGE_ROW_TILE, c), lambda i: (i, 0))

    return pl.pallas_call(
        functools.partial(_merge_kernel, batch=batch),
        out_shape=jax.ShapeDtypeStruct((rows, d), F32),
        grid=(rows // MERGE_ROW_TILE,),
        in_specs=[pl.BlockSpec((tt, att.shape[1]), lambda i: (i, 0)),
                  row_spec(rnn.shape[1]), row_spec(d), row_spec(d),
                  row_spec(d), _resident(wa.shape[1:], layer),
                  _resident(wr.shape[1:], layer),
                  _resident(wo.shape[1:], layer)],
        out_specs=row_spec(d),
        scratch_shapes=[pltpu.VMEM((_slabs(D_ATT), MERGE_ROW_TILE, LANES), F32)],
        compiler_params=pltpu.CompilerParams(
            dimension_semantics=("parallel",), vmem_limit_bytes=VMEM_LIMIT),
        name="merge",
    )(att, rnn, ga, gl, x, wa, wr, wo)


def _inproj_weights(w_in, d_rnn, d_model):
    splits = (D_ATT, HEAD_DIM, HEAD_DIM, IDX_HEADS * IDX_DIM, IDX_DIM,
              IDX_HEADS)
    parts, c = [], 0
    for n in splits:
        parts.append(w_in[..., c:c + n])
        c += n
    assert c + 2 * d_rnn + 2 * d_model == w_in.shape[-1]
    wq, wk, wv, wqi, wki, wwi = parts
    wi_pad = jnp.zeros(w_in.shape[:-1] + (LANES - IDX_HEADS,), w_in.dtype)
    cols = [wq, wk, wv, wv, wk, wqi] + [wki] * (LANES // IDX_DIM)
    cols += [wwi, wi_pad]
    return (jnp.concatenate(cols, axis=-1).astype(BF16),
            w_in[..., c:])


def _block_diag_tiles(w):
    depth, n_blocks, bw, _ = w.shape
    per = MXU_DIM // bw
    w = w.reshape(depth, n_blocks // per, per, bw, bw)
    eye = jnp.eye(per, dtype=w.dtype)
    t = w[:, :, :, :, None, :] * eye[None, None, :, None, :, None]
    return t.reshape(depth, n_blocks // per, MXU_DIM, MXU_DIM).astype(BF16)


def kernel(x, ffn1_norm, ffn1_wg, ffn1_wu, ffn1_wd, mix_norm, w_in, conv_w,
           conv_b, rg_wa, rg_ba, rg_wx, rg_bx, rg_lam, w_att_proj, w_rnn_proj,
           w_out, ffn2_norm, ffn2_wg, ffn2_wu, ffn2_wd, final_norm):
    batch, seq, d = x.shape
    depth = ffn1_norm.shape[0]
    d_rnn = conv_w.shape[-1]
    assert batch == SUBLANES and seq % Q_TILE == 0
    assert (batch * seq) % MERGE_ROW_TILE == 0 and ROW_TILE % batch == 0

    def row(v):
        return v.reshape(1, -1)

    wg1, wu1, wd1 = ffn1_wg, ffn1_wu, ffn1_wd
    wg2, wu2, wd2 = ffn2_wg, ffn2_wu, ffn2_wd
    w_att, w_wide = _inproj_weights(w_in, d_rnn, d)
    wa_t, wx_t = _block_diag_tiles(rg_wa), _block_diag_tiles(rg_wx)

    h = x
    for l in range(depth):
        h = _ffn(h, row(ffn1_norm[l]), wg1, wu1, wd1, l, batch, first=(l == 0))
        q, kv, qi, ki, wi, ga, gl, rnn = _mixer_in(
            h, row(mix_norm[l]), w_att, w_wide, conv_w[l], row(conv_b[l]), wa_t,
            row(rg_ba[l]), wx_t, row(rg_bx[l]), row(rg_lam[l]), l, batch)
        att = _attention(q, qi, wi, kv, ki, batch)
        h = _merge(att, rnn, ga, gl, h, w_att_proj, w_rnn_proj, w_out, l,
                   batch)
        last = l == depth - 1
        h = _ffn(h, row(ffn2_norm[l]), wg2, wu2, wd2, l, batch,
                 final_g=row(final_norm) if last else None)
    return h
```

```python
import functools

import jax
import jax.numpy as jnp
from jax import lax
from jax.experimental import pallas as pl
from jax.experimental.pallas import tpu as pltpu

F32 = jnp.float32
BF16 = jnp.bfloat16
I32 = jnp.int32

N_HEADS = 8
HEAD_DIM = 64
D_ATT = N_HEADS * HEAD_DIM
IDX_HEADS = 8
IDX_DIM = 32
TOPK_MAX = 256
CONV_W = 4
LRU_C = 8.0
EPS = 1e-6

LANES = 128
SUBLANES = 8
BF16_SUBLANES = 16
MXU_DIM = 256
VMEM_LIMIT = 56 * 1024 * 1024

INT_MIN = -2 ** 31
MASK_BIAS = -1e30

ROW_TILE = 512
MERGE_ROW_TILE = 1024
Q_TILE = 256
K_CHUNK = 256


def _resident(shape, layer=None):
    zeros = (0,) * len(shape)
    if layer is None:
        return pl.BlockSpec(shape, lambda *_: zeros,
                            pipeline_mode=pl.Buffered(1))
    return pl.BlockSpec((None,) + tuple(shape), lambda *_: (layer,) + zeros,
                        pipeline_mode=pl.Buffered(1))


def _rmsnorm(x, g):
    ms = jnp.mean(x * x, axis=-1, keepdims=True)
    return x * lax.rsqrt(ms + EPS) * g


def _dot(a, b):
    return jnp.dot(a, b, preferred_element_type=F32)


def _dot_nt(a, b):
    return lax.dot_general(a, b, (((1,), (1,)), ((), ())),
                           preferred_element_type=F32)


def _ff_chunks(d_ff):
    chunks, c0 = [], 0
    while c0 < d_ff:
        c1 = min(c0 + MXU_DIM, d_ff)
        chunks.append((c0, c1))
        c0 = c1
    return chunks


def _slabs(width):
    return width // LANES


def _batch_to_rows(x_ref, slab_scr, batch):
    tt, d = x_ref.shape[1:]
    for b in range(batch):
        for j in range(_slabs(d)):
            slab_scr[j, pl.ds(b, tt, stride=batch), :] = (
                x_ref[b, :, j * LANES:(j + 1) * LANES])
    return jnp.concatenate([slab_scr[j] for j in range(_slabs(d))], axis=1)


def _rows_to_batch(o_ref, y, slab_scr, batch):
    tt, d = o_ref.shape[1:]
    for j in range(_slabs(d)):
        slab_scr[j] = y[:, j * LANES:(j + 1) * LANES]
    for b in range(batch):
        for j in range(_slabs(d)):
            o_ref[b, :, j * LANES:(j + 1) * LANES] = (
                slab_scr[j, pl.ds(b, tt, stride=batch), :])


def _rows_to_seq(o_ref, z, slab_scr, slab0, batch):
    tt = o_ref.shape[0]
    c = z.shape[1]
    for j in range(_slabs(c)):
        slab_scr[slab0 + j] = z[:, j * LANES:(j + 1) * LANES]
    for b in range(batch):
        for j in range(_slabs(c)):
            o_ref[:, b * c + j * LANES:b * c + (j + 1) * LANES] = (
                slab_scr[slab0 + j, pl.ds(b, tt, stride=batch), :]
                .astype(o_ref.dtype))


def _seq_to_rows(x_ref, slab_scr, batch):
    tt = x_ref.shape[0]
    c = x_ref.shape[1] // batch
    for b in range(batch):
        for j in range(_slabs(c)):
            slab_scr[j, pl.ds(b, tt, stride=batch), :] = (
                x_ref[:, b * c + j * LANES:b * c + (j + 1) * LANES].astype(F32))
    return jnp.concatenate([slab_scr[j] for j in range(_slabs(c))], axis=1)


def _ffn_kernel(x_ref, g_ref, wg_ref, wu_ref, wd_ref, *rest, first, final,
                batch):
    rest = list(rest)
    fg_ref = rest.pop(0) if final else None
    o_ref = rest.pop(0)
    slab_scr = rest.pop(0) if (first or final) else None
    x = _batch_to_rows(x_ref, slab_scr, batch) if first else x_ref[...]
    h = _rmsnorm(x, g_ref[...]).astype(BF16)
    acc = None
    for c0, c1 in _ff_chunks(wg_ref.shape[1]):
        g = _dot(h, wg_ref[:, c0:c1].astype(BF16))
        u = _dot(h, wu_ref[:, c0:c1].astype(BF16))
        a = (g * jax.nn.sigmoid(g) * u).astype(BF16)
        d = _dot(a, wd_ref[c0:c1, :].astype(BF16))
        acc = d if acc is None else acc + d
    y = x + 0.5 * acc
    if final:
        _rows_to_batch(o_ref, _rmsnorm(y, fg_ref[...]), slab_scr, batch)
    else:
        o_ref[...] = y


def _ffn(x, g, wg, wu, wd, layer, batch, first=False, final_g=None):
    final = final_g is not None
    if first:
        _, seq, d = x.shape
        rows = seq * batch
    else:
        rows, d = x.shape
        seq = rows // batch
    d_ff = wg.shape[2]
    tt = ROW_TILE // batch
    row_spec = pl.BlockSpec((ROW_TILE, d), lambda i: (i, 0))
    batch_spec = pl.BlockSpec((batch, tt, d), lambda i: (0, i, 0))
    in_specs = [batch_spec if first else row_spec, _resident((1, d)),
                _resident((d, d_ff), layer), _resident((d, d_ff), layer),
                _resident((d_ff, d), layer)]
    args = [x, g, wg, wu, wd]
    if final:
        in_specs.append(_resident((1, d)))
        args.append(final_g)
    scratch = ([pltpu.VMEM((_slabs(d), ROW_TILE, LANES), F32)]
               if (first or final) else [])
    out_shape = (batch, seq, d) if final else (rows, d)
    return pl.pallas_call(
        functools.partial(_ffn_kernel, first=first, final=final, batch=batch),
        out_shape=jax.ShapeDtypeStruct(out_shape, F32),
        grid=(rows // ROW_TILE,),
        in_specs=in_specs,
        out_specs=batch_spec if final else row_spec,
        scratch_shapes=scratch,
        compiler_params=pltpu.CompilerParams(
            dimension_semantics=("parallel",), vmem_limit_bytes=VMEM_LIMIT),
        name="ffn_first" if first else ("ffn_final" if final else "ffn"),
    )(*args)


ATT_WIDTHS = (D_ATT, 4 * HEAD_DIM, IDX_HEADS * IDX_DIM, LANES, LANES)
ATT_DTYPES = (BF16, BF16, BF16, BF16, F32)


def _mixer_in_kernel(x_ref, g_ref, w_att_ref, w_ref, cw_ref, cb_ref, wa_ref,
                     ba_ref,
                     wx_ref, bx_ref, lam_ref,
                     q_ref, kv_ref, qi_ref, ki_ref, wi_ref, ga_ref, gl_ref,
                     rnn_ref, slab_scr, xs_scr, a_scr, u_scr, h_scr, hc_scr,
                     *, batch):
    tm = x_ref.shape[0]
    c = cw_ref.shape[1]
    d_model = ga_ref.shape[1]
    halo = (CONV_W - 1) * batch
    h = _rmsnorm(x_ref[...], g_ref[...]).astype(BF16)

    @pl.when(pl.program_id(0) == 0)
    def _():
        xs_scr[0:halo, :] = jnp.zeros((halo, c), F32)
        hc_scr[...] = jnp.zeros(hc_scr.shape, F32)

    scales = (HEAD_DIM ** -0.5, None, None, None, (IDX_HEADS * IDX_DIM) ** -0.5)
    att_refs = (q_ref, kv_ref, qi_ref, ki_ref, wi_ref)
    gate_col0 = 2 * c

    def att_operand(i):
        col = sum(ATT_WIDTHS[:i])
        z = _dot(h, w_att_ref[:, col:col + ATT_WIDTHS[i]])
        if scales[i] is not None:
            z = z * scales[i]
        _rows_to_seq(att_refs[i], z, slab_scr, _slabs(sum(ATT_WIDTHS[:i])),
                     batch)

    def gate_logits(o_ref):
        col = gate_col0 if o_ref is ga_ref else gate_col0 + d_model
        o_ref[...] = _dot(h, w_ref[:, col:col + d_model]).astype(o_ref.dtype)

    fillers = (lambda: gate_logits(ga_ref),
               lambda: gate_logits(gl_ref),
               lambda: (att_operand(0), att_operand(1)),
               lambda: (att_operand(2), att_operand(3), att_operand(4)))

    n_tiles = c // MXU_DIM
    assert n_tiles == len(fillers)
    for n in range(n_tiles):
        cs = slice(n * MXU_DIM, (n + 1) * MXU_DIM)
        xr = _dot(h, w_ref[:, cs])
        gr = _dot(h, w_ref[:, c + n * MXU_DIM:c + (n + 1) * MXU_DIM])
        xs_scr[halo:halo + tm, cs] = xr
        xc = cb_ref[:, cs]
        for j in range(CONV_W):
            xc = xc + cw_ref[j:j + 1, cs] * xs_scr[j * batch:j * batch + tm, cs]
        xs_scr[0:halo, cs] = xr[tm - halo:tm, :]

        xb = xc.astype(BF16)
        r = jax.nn.sigmoid(_dot(xb, wa_ref[n]) + ba_ref[:, cs])
        gate_i = jax.nn.sigmoid(_dot(xb, wx_ref[n]) + bx_ref[:, cs])
        fillers[n]()
        neg_lam = -lam_ref[:, cs]
        softplus = (jnp.maximum(neg_lam, 0.0)
                    + jnp.log1p(jnp.exp(-jnp.abs(neg_lam))))
        log_a = -LRU_C * r * softplus
        a_scr[:, cs] = jnp.exp(log_a)
        th = jnp.tanh(log_a)
        u_scr[:, cs] = jnp.sqrt(-2.0 * th / (1.0 - th)) * gate_i * xc

        hcur = hc_scr[:, cs]
        for t in range(tm // batch):
            rs = slice(t * batch, (t + 1) * batch)
            hcur = a_scr[rs, cs] * hcur + u_scr[rs, cs]
            h_scr[rs, cs] = hcur
        hc_scr[:, cs] = hcur
        rnn_ref[:, cs] = (h_scr[:, cs] * jax.nn.gelu(gr)).astype(BF16)


def _mixer_in(x, g, w_att, w, cw, cb, wa, ba, wx, bx, lam, layer, batch):
    rows, d = x.shape
    c = cw.shape[1]
    tm = ROW_TILE
    tt = tm // batch
    seq = rows // batch
    halo = (CONV_W - 1) * batch
    assert w.shape[2] == 2 * c + 2 * d and w_att.shape[2] == sum(ATT_WIDTHS)

    def row_spec(n):
        return pl.BlockSpec((tm, n), lambda i: (i, 0))

    def seq_spec(n):
        return pl.BlockSpec((tt, batch * n), lambda i: (i, 0))

    out_shape = [jax.ShapeDtypeStruct((seq, batch * n), t)
                 for n, t in zip(ATT_WIDTHS, ATT_DTYPES)]
    out_shape += [jax.ShapeDtypeStruct((rows, d), BF16)] * 2
    out_shape += [jax.ShapeDtypeStruct((rows, c), BF16)]
    out_specs = [seq_spec(n) for n in ATT_WIDTHS]
    out_specs += [row_spec(d), row_spec(d), row_spec(c)]
    return pl.pallas_call(
        functools.partial(_mixer_in_kernel, batch=batch),
        out_shape=out_shape,
        grid=(rows // tm,),
        in_specs=[row_spec(d), _resident((1, d)),
                  _resident(w_att.shape[1:], layer),
                  _resident(w.shape[1:], layer), _resident(cw.shape),
                  _resident((1, c)), _resident(wa.shape[1:], layer),
                  _resident((1, c)), _resident(wx.shape[1:], layer),
                  _resident((1, c)), _resident((1, c))],
        out_specs=out_specs,
        scratch_shapes=[
            pltpu.VMEM((_slabs(sum(ATT_WIDTHS)), tm, LANES), F32),
            pltpu.VMEM((halo + tm, c), F32),
            pltpu.VMEM((tm, c), F32), pltpu.VMEM((tm, c), F32),
            pltpu.VMEM((tm, c), F32), pltpu.VMEM((batch, c), F32)],
        compiler_params=pltpu.CompilerParams(
            dimension_semantics=("arbitrary",), vmem_limit_bytes=VMEM_LIMIT),
        name="mixer_in",
    )(x, g, w_att, w, cw, cb, wa, ba, wx, bx, lam)


VT_ROWS = HEAD_DIM + 16


def _key_to_float(key):
    bits = jnp.where(key < 0, INT_MIN - key, key)
    return pltpu.bitcast(bits, F32)


def _for_chunks(n, body):
    def quad(i, carry):
        for j in range(4):
            body(4 * i + j)
        return carry

    lax.fori_loop(0, n // 4, quad, 0)
    base = (n // 4) * 4

    @pl.when(n % 4 >= 2)
    def _():
        body(base)
        body(base + 1)

    @pl.when(n % 2 == 1)
    def _():
        body(n - 1)


def _fold_chunks(n, chunk, init):
    def pair(i, c):
        return chunk(2 * i + 1, chunk(2 * i, c))

    c = lax.fori_loop(0, n // 2, pair, init)
    return lax.cond(n % 2 == 1, lambda c: chunk(n - 1, c), lambda c: c, c)


def _attn_kernel(q_ref, qi_ref, wi_ref, kv_ref, ki_ref, o_ref,
                 qm_scr, qim_scr, wit_scr, vt_scr, sc_scr, scb_scr, sall_scr, m_scr,
                 acc_scr, mpart_scr, ties_scr, *, topk):
    tq, ck = Q_TILE, K_CHUNK
    n_kc = sc_scr.shape[0]
    seq = n_kc * ck
    qt = pl.program_id(1)
    n_chunks = (qt * tq) // ck + tq // ck
    lane = lax.broadcasted_iota(I32, (tq, LANES), 1)
    key_iota = lax.broadcasted_iota(I32, (ck, tq), 0)
    qry_iota = lax.broadcasted_iota(I32, (ck, tq), 1)
    kf = float(topk)

    for h in range(N_HEADS):
        qp = q_ref[:, (h // 2) * LANES:(h // 2 + 1) * LANES].astype(F32)
        lo = (h % 2) * HEAD_DIM
        qm_scr[h] = jnp.where((lane >= lo) & (lane < lo + HEAD_DIM), qp,
                              0.0).astype(BF16)
    for h in range(IDX_HEADS):
        qp = qi_ref[:, (h // 4) * LANES:(h // 4 + 1) * LANES].astype(F32)
        lo = (h % 4) * IDX_DIM
        qim_scr[h] = jnp.where((lane >= lo) & (lane < lo + IDX_DIM), qp,
                               0.0).astype(BF16)
    wit_scr[...] = jnp.transpose(wi_ref[...])

    @pl.when(qt == 0)
    def _():
        lane_c = lax.broadcasted_iota(I32, (ck, LANES), 1)
        for c in range(n_kc):
            v1 = jnp.where(lane_c < HEAD_DIM,
                           kv_ref[c * ck:(c + 1) * ck, LANES:2 * LANES].astype(F32),
                           1.0)
            vt_scr[c] = jnp.transpose(v1)[0:VT_ROWS, :].astype(BF16)

    def score_chunk(kc):
        k0 = pl.multiple_of(kc * ck, ck)
        kic = ki_ref[pl.ds(k0, ck), :]
        acc = jnp.zeros((ck, tq), F32)
        for h in range(IDX_HEADS):
            acc = acc + (jnp.maximum(_dot_nt(kic, qim_scr[h]), 0.0)
                         * wit_scr[h:h + 1, :])
        causal = key_iota - qry_iota <= qt * tq - kc * ck
        sc = jnp.where(causal, acc, -jnp.inf)
        sc_scr[kc] = sc
        scb_scr[kc] = sc.astype(BF16)

    _for_chunks(n_chunks, score_chunk)

    def count(pred):
        acc_rows = 2 * SUBLANES

        def chunk(kc, c):
            hit = jnp.where(pred(sc_scr[kc], kc), 1.0, 0.0)
            return c + jnp.sum(hit.reshape(ck // acc_rows, acc_rows, tq), axis=0)

        c = _fold_chunks(n_chunks, chunk, jnp.zeros((acc_rows, tq), F32))
        return jnp.sum(c, axis=0, keepdims=True)

    def count_bf16(cand):
        acc_rows = 2 * BF16_SUBLANES
        one, zero = jnp.ones((), BF16), jnp.zeros((), BF16)

        def chunk(kc, c):
            hit = jnp.where(scb_scr[kc] >= cand, one, zero)
            part = hit[0:acc_rows]
            for r0 in range(acc_rows, ck, acc_rows):
                part = part + hit[r0:r0 + acc_rows]
            return c + part.astype(F32)

        c = _fold_chunks(n_chunks, chunk, jnp.zeros((acc_rows, tq), F32))
        return jnp.sum(c, axis=0, keepdims=True)

    def coarse_step(i, thr16):
        cand16 = thr16 + lax.shift_left(jnp.int32(1), 15 - i)
        cand_f = _key_to_float(cand16 * 65536)
        tot = count_bf16(cand_f.astype(BF16))
        return jnp.where(tot >= kf, cand16, thr16)

    thr16 = lax.fori_loop(0, 16, coarse_step,
                          jnp.full((1, tq), -2 ** 15, I32))

    def fine_step(i, carry):
        thr, above = carry
        cand = thr + lax.shift_left(jnp.int32(1), 16 - i)
        cand_f = _key_to_float(cand)
        tot = count(lambda s, kc: s >= cand_f)
        ok = tot >= kf
        return jnp.where(ok, cand, thr), jnp.where(ok, above, tot)

    thr, above = lax.fori_loop(
        0, 17, fine_step,
        ((thr16 - 1) * 65536, jnp.full((1, tq), -1.0, F32)))

    thr_f = _key_to_float(thr)
    short = jnp.logical_not(thr_f > -jnp.inf)
    thr_f = jnp.where(short, -jnp.inf, thr_f)
    n_gt = lax.cond(jnp.min(above) < 0.0,
                    lambda: count(lambda s, kc: s > thr_f), lambda: above)
    need = jnp.where(short, 0.0, kf - n_gt)
    lower_tri = jnp.where(
        lax.broadcasted_iota(I32, (ck, ck), 0)
        >= lax.broadcasted_iota(I32, (ck, ck), 1), 1.0, 0.0).astype(BF16)

    mpart_scr[...] = jnp.full(mpart_scr.shape, MASK_BIAS, F32)
    acc_scr[...] = jnp.zeros(acc_scr.shape, F32)

    ties_scr[...] = jnp.zeros(ties_scr.shape, F32)

    def scores_chunk(kc):
        k0 = pl.multiple_of(kc * ck, ck)
        s_idx = sc_scr[kc]
        tie = s_idx == thr_f
        tie_f = jnp.where(tie, 1.0, 0.0)
        ties_before = ties_scr[...]
        ties_scr[...] = ties_before + jnp.sum(tie_f, axis=0, keepdims=True)
        rank = ties_before + _dot(lower_tri, tie_f.astype(BF16))
        keep = (s_idx > thr_f) | (tie & (rank <= need))
        bias = jnp.where(keep, 0.0, MASK_BIAS)
        for h in range(N_HEADS):
            c0 = 0 if h % 2 == 0 else LANES
            k_mat = kv_ref[pl.ds(k0, ck), c0:c0 + LANES]
            s = _dot_nt(k_mat, qm_scr[h]) + bias
            sall_scr[kc, h] = s
            mpart_scr[h] = jnp.maximum(
                mpart_scr[h],
                jnp.max(s.reshape(ck // SUBLANES, SUBLANES, tq), axis=0))

    _for_chunks(n_chunks, scores_chunk)
    for h in range(N_HEADS):
        m_scr[h:h + 1, :] = jnp.max(mpart_scr[h], axis=0, keepdims=True)

    def value_chunk(kc):
        vt_c = vt_scr[kc]
        for h in range(N_HEADS):
            p = jnp.exp(sall_scr[kc, h] - m_scr[h:h + 1, :]).astype(BF16)
            acc_scr[h] = acc_scr[h] + _dot(vt_c, p)

    _for_chunks(n_chunks, value_chunk)

    outs = []
    for h in range(N_HEADS):
        a = acc_scr[h]
        outs.append(a[0:HEAD_DIM, :] / a[HEAD_DIM:HEAD_DIM + 1, :])
    o_ref[...] = jnp.transpose(jnp.concatenate(outs, axis=0)).astype(BF16)


def _attention(q, qi, wi, kv, ki, batch):
    tq, ck = Q_TILE, K_CHUNK
    seq = q.shape[0]
    topk = min(TOPK_MAX, seq // 4)
    nq = seq // tq

    def q_spec(c):
        return pl.BlockSpec((tq, c), lambda b, i: (i, b))

    def kv_spec(c):
        return pl.BlockSpec((seq, c), lambda b, i: (0, b))

    return pl.pallas_call(
        functools.partial(_attn_kernel, topk=topk),
        out_shape=jax.ShapeDtypeStruct((seq, batch * D_ATT), BF16),
        grid=(batch, nq),
        in_specs=[q_spec(D_ATT), q_spec(IDX_HEADS * IDX_DIM), q_spec(LANES),
                  kv_spec(4 * HEAD_DIM), kv_spec(LANES)],
        out_specs=q_spec(D_ATT),
        scratch_shapes=[
            pltpu.VMEM((N_HEADS, tq, LANES), BF16),
            pltpu.VMEM((IDX_HEADS, tq, LANES), BF16),
            pltpu.VMEM((LANES, tq), F32),
            pltpu.VMEM((seq // ck, VT_ROWS, ck), BF16),
            pltpu.VMEM((seq // ck, ck, tq), F32),
            pltpu.VMEM((seq // ck, ck, tq), BF16),
            pltpu.VMEM((seq // ck, N_HEADS, ck, tq), F32),
            pltpu.VMEM((N_HEADS, tq), F32),
            pltpu.VMEM((N_HEADS, VT_ROWS, tq), F32),
            pltpu.VMEM((N_HEADS, SUBLANES, tq), F32),
            pltpu.VMEM((1, tq), F32),
        ],
        compiler_params=pltpu.CompilerParams(
            dimension_semantics=("parallel", "arbitrary"),
            vmem_limit_bytes=VMEM_LIMIT),
        name="dsa_attention",
    )(q, qi, wi, kv, ki)


def _merge_kernel(att_ref, rnn_ref, ga_ref, gl_ref, x_ref, wa_ref, wr_ref,
                  wo_ref, o_ref, slab_scr, *, batch):
    att = _seq_to_rows(att_ref, slab_scr, batch).astype(BF16)
    pa = _dot(att, wa_ref[...])
    pr = _dot(rnn_ref[...], wr_ref[...])
    merged = (jax.nn.sigmoid(ga_ref[...].astype(F32)) * pa
              + jax.nn.sigmoid(gl_ref[...].astype(F32)) * pr)
    o_ref[...] = x_ref[...] + _dot(merged.astype(BF16), wo_ref[...])


def _merge(att, rnn, ga, gl, x, wa, wr, wo, layer, batch):
    rows, d = x.shape
    tt = MERGE_ROW_TILE // batch

    def row_spec(c):
        return pl.BlockSpec((MERGE_ROW_TILE, c), lambda i: (i, 0))

    return pl.pallas_call(
        functools.partial(_merge_kernel, batch=batch),
        out_shape=jax.ShapeDtypeStruct((rows, d), F32),
        grid=(rows // MERGE_ROW_TILE,),
        in_specs=[pl.BlockSpec((tt, att.shape[1]), lambda i: (i, 0)),
                  row_spec(rnn.shape[1]), row_spec(d), row_spec(d),
                  row_spec(d), _resident(wa.shape[1:], layer),
                  _resident(wr.shape[1:], layer),
                  _resident(wo.shape[1:], layer)],
        out_specs=row_spec(d),
        scratch_shapes=[pltpu.VMEM((_slabs(D_ATT), MERGE_ROW_TILE, LANES), F32)],
        compiler_params=pltpu.CompilerParams(
            dimension_semantics=("parallel",), vmem_limit_bytes=VMEM_LIMIT),
        name="merge",
    )(att, rnn, ga, gl, x, wa, wr, wo)


def _inproj_weights(w_in, d_rnn, d_model):
    splits = (D_ATT, HEAD_DIM, HEAD_DIM, IDX_HEADS * IDX_DIM, IDX_DIM,
              IDX_HEADS)
    parts, c = [], 0
    for n in splits:
        parts.append(w_in[..., c:c + n])
        c += n
    assert c + 2 * d_rnn + 2 * d_model == w_in.shape[-1]
    wq, wk, wv, wqi, wki, wwi = parts
    wi_pad = jnp.zeros(w_in.shape[:-1] + (LANES - IDX_HEADS,), w_in.dtype)
    cols = [wq, wk, wv, wv, wk, wqi] + [wki] * (LANES // IDX_DIM)
    cols += [wwi, wi_pad]
    return (jnp.concatenate(cols, axis=-1).astype(BF16),
            w_in[..., c:].astype(BF16))


def _block_diag_tiles(w):
    depth, n_blocks, bw, _ = w.shape
    per = MXU_DIM // bw
    w = w.reshape(depth, n_blocks // per, per, bw, bw)
    eye = jnp.eye(per, dtype=w.dtype)
    t = w[:, :, :, :, None, :] * eye[None, None, :, None, :, None]
    return t.reshape(depth, n_blocks // per, MXU_DIM, MXU_DIM).astype(BF16)


def kernel(x, ffn1_norm, ffn1_wg, ffn1_wu, ffn1_wd, mix_norm, w_in, conv_w,
           conv_b, rg_wa, rg_ba, rg_wx, rg_bx, rg_lam, w_att_proj, w_rnn_proj,
           w_out, ffn2_norm, ffn2_wg, ffn2_wu, ffn2_wd, final_norm):
    batch, seq, d = x.shape
    depth = ffn1_norm.shape[0]
    d_rnn = conv_w.shape[-1]
    assert batch == SUBLANES and seq % Q_TILE == 0
    assert (batch * seq) % MERGE_ROW_TILE == 0 and ROW_TILE % batch == 0

    def row(v):
        return v.reshape(1, -1)

    wg1, wu1, wd1 = ffn1_wg, ffn1_wu, ffn1_wd
    wg2, wu2, wd2 = ffn2_wg, ffn2_wu, ffn2_wd
    w_att, w_wide = _inproj_weights(w_in, d_rnn, d)
    wa_t, wx_t = _block_diag_tiles(rg_wa), _block_diag_tiles(rg_wx)
    w_ap, w_rp, w_o = (w.astype(BF16) for w in (w_att_proj, w_rnn_proj, w_out))

    h = x
    for l in range(depth):
        h = _ffn(h, row(ffn1_norm[l]), wg1, wu1, wd1, l, batch, first=(l == 0))
        q, kv, qi, ki, wi, ga, gl, rnn = _mixer_in(
            h, row(mix_norm[l]), w_att, w_wide, conv_w[l], row(conv_b[l]), wa_t,
            row(rg_ba[l]), wx_t, row(rg_bx[l]), row(rg_lam[l]), l, batch)
        att = _attention(q, qi, wi, kv, ki, batch)
        h = _merge(att, rnn, ga, gl, h, w_ap, w_rp, w_o, l, batch)
        last = l == depth - 1
        h = _ffn(h, row(ffn2_norm[l]), wg2, wu2, wd2, l, batch,
                 final_g=row(final_norm) if last else None)
    return h
```

```python
import functools

import jax
import jax.numpy as jnp
from jax import lax
from jax.experimental import pallas as pl
from jax.experimental.pallas import tpu as pltpu

F32 = jnp.float32
BF16 = jnp.bfloat16
I32 = jnp.int32

N_HEADS = 8
HEAD_DIM = 64
D_ATT = N_HEADS * HEAD_DIM
IDX_HEADS = 8
IDX_DIM = 32
TOPK_MAX = 256
CONV_W = 4
LRU_C = 8.0
EPS = 1e-6

LANES = 128
SUBLANES = 8
BF16_SUBLANES = 16
MXU_DIM = 256
VMEM_LIMIT = 56 * 1024 * 1024

INT_MIN = -2 ** 31
MASK_BIAS = -1e30

ROW_TILE = 512
MERGE_ROW_TILE = 1024
Q_TILE = 256
K_CHUNK = 256


def _resident(shape, layer=None):
    zeros = (0,) * len(shape)
    if layer is None:
        return pl.BlockSpec(shape, lambda *_: zeros,
                            pipeline_mode=pl.Buffered(1))
    return pl.BlockSpec((None,) + tuple(shape), lambda *_: (layer,) + zeros,
                        pipeline_mode=pl.Buffered(1))


def _rmsnorm(x, g):
    ms = jnp.mean(x * x, axis=-1, keepdims=True)
    return x * lax.rsqrt(ms + EPS) * g


def _dot(a, b):
    return jnp.dot(a, b, preferred_element_type=F32)


def _dot_nt(a, b):
    return lax.dot_general(a, b, (((1,), (1,)), ((), ())),
                           preferred_element_type=F32)


def _ff_chunks(d_ff):
    chunks, c0 = [], 0
    while c0 < d_ff:
        c1 = min(c0 + MXU_DIM, d_ff)
        chunks.append((c0, c1))
        c0 = c1
    return chunks


def _slabs(width):
    return width // LANES


def _batch_to_rows(x_ref, slab_scr, batch):
    tt, d = x_ref.shape[1:]
    for b in range(batch):
        for j in range(_slabs(d)):
            slab_scr[j, pl.ds(b, tt, stride=batch), :] = (
                x_ref[b, :, j * LANES:(j + 1) * LANES])
    return jnp.concatenate([slab_scr[j] for j in range(_slabs(d))], axis=1)


def _rows_to_batch(o_ref, y, slab_scr, batch):
    tt, d = o_ref.shape[1:]
    for j in range(_slabs(d)):
        slab_scr[j] = y[:, j * LANES:(j + 1) * LANES]
    for b in range(batch):
        for j in range(_slabs(d)):
            o_ref[b, :, j * LANES:(j + 1) * LANES] = (
                slab_scr[j, pl.ds(b, tt, stride=batch), :])


def _rows_to_seq(o_ref, z, slab_scr, slab0, batch):
    tt = o_ref.shape[0]
    c = z.shape[1]
    for j in range(_slabs(c)):
        slab_scr[slab0 + j] = z[:, j * LANES:(j + 1) * LANES]
    for b in range(batch):
        for j in range(_slabs(c)):
            o_ref[:, b * c + j * LANES:b * c + (j + 1) * LANES] = (
                slab_scr[slab0 + j, pl.ds(b, tt, stride=batch), :]
                .astype(o_ref.dtype))


def _seq_to_rows(x_ref, slab_scr, batch):
    tt = x_ref.shape[0]
    c = x_ref.shape[1] // batch
    for b in range(batch):
        for j in range(_slabs(c)):
            slab_scr[j, pl.ds(b, tt, stride=batch), :] = (
                x_ref[:, b * c + j * LANES:b * c + (j + 1) * LANES].astype(F32))
    return jnp.concatenate([slab_scr[j] for j in range(_slabs(c))], axis=1)


def _ffn_kernel(x_ref, g_ref, wg_ref, wu_ref, wd_ref, *rest, first, final,
                batch):
    rest = list(rest)
    fg_ref = rest.pop(0) if final else None
    o_ref = rest.pop(0)
    slab_scr = rest.pop(0) if (first or final) else None
    x = _batch_to_rows(x_ref, slab_scr, batch) if first else x_ref[...]
    h = _rmsnorm(x, g_ref[...]).astype(BF16)
    acc = None
    for c0, c1 in _ff_chunks(wg_ref.shape[1]):
        g = _dot(h, wg_ref[:, c0:c1].astype(BF16))
        u = _dot(h, wu_ref[:, c0:c1].astype(BF16))
        a = (g * jax.nn.sigmoid(g) * u).astype(BF16)
        d = _dot(a, wd_ref[c0:c1, :].astype(BF16))
        acc = d if acc is None else acc + d
    y = x + 0.5 * acc
    if final:
        _rows_to_batch(o_ref, _rmsnorm(y, fg_ref[...]), slab_scr, batch)
    else:
        o_ref[...] = y


def _ffn(x, g, wg, wu, wd, layer, batch, first=False, final_g=None):
    final = final_g is not None
    if first:
        _, seq, d = x.shape
        rows = seq * batch
    else:
        rows, d = x.shape
        seq = rows // batch
    d_ff = wg.shape[2]
    tt = ROW_TILE // batch
    row_spec = pl.BlockSpec((ROW_TILE, d), lambda i: (i, 0))
    batch_spec = pl.BlockSpec((batch, tt, d), lambda i: (0, i, 0))
    in_specs = [batch_spec if first else row_spec, _resident((1, d)),
                _resident((d, d_ff), layer), _resident((d, d_ff), layer),
                _resident((d_ff, d), layer)]
    args = [x, g, wg, wu, wd]
    if final:
        in_specs.append(_resident((1, d)))
        args.append(final_g)
    scratch = ([pltpu.VMEM((_slabs(d), ROW_TILE, LANES), F32)]
               if (first or final) else [])
    out_shape = (batch, seq, d) if final else (rows, d)
    return pl.pallas_call(
        functools.partial(_ffn_kernel, first=first, final=final, batch=batch),
        out_shape=jax.ShapeDtypeStruct(out_shape, F32),
        grid=(rows // ROW_TILE,),
        in_specs=in_specs,
        out_specs=batch_spec if final else row_spec,
        scratch_shapes=scratch,
        compiler_params=pltpu.CompilerParams(
            dimension_semantics=("parallel",), vmem_limit_bytes=VMEM_LIMIT),
        name="ffn_first" if first else ("ffn_final" if final else "ffn"),
    )(*args)


ATT_WIDTHS = (D_ATT, 4 * HEAD_DIM, IDX_HEADS * IDX_DIM, LANES, LANES)
ATT_DTYPES = (BF16, BF16, BF16, BF16, F32)


def _mixer_in_kernel(x_ref, g_ref, wt_ref, cw_ref, cb_ref, wa_ref, ba_ref,
                     wx_ref, bx_ref, lam_ref,
                     q_ref, kv_ref, qi_ref, ki_ref, wi_ref, ga_ref, gl_ref,
                     rnn_ref, slab_scr, xs_scr, a_scr, u_scr, h_scr, hc_scr,
                     *, batch):
    tm = x_ref.shape[0]
    c = cw_ref.shape[1]
    d_model = ga_ref.shape[1]
    halo = (CONV_W - 1) * batch
    h = _rmsnorm(x_ref[...], g_ref[...]).astype(BF16)

    @pl.when(pl.program_id(0) == 0)
    def _():
        xs_scr[0:halo, :] = jnp.zeros((halo, c), F32)
        hc_scr[...] = jnp.zeros(hc_scr.shape, F32)

    scales = (HEAD_DIM ** -0.5, None, None, None, (IDX_HEADS * IDX_DIM) ** -0.5)
    att_refs = (q_ref, kv_ref, qi_ref, ki_ref, wi_ref)
    r_q, r_k, r_v = 0, D_ATT, D_ATT + HEAD_DIM
    r_qi = r_v + HEAD_DIM
    r_ki = r_qi + IDX_HEADS * IDX_DIM
    r_wi = r_ki + IDX_DIM
    r_xr = r_wi + IDX_HEADS
    r_gr, r_ga = r_xr + c, r_xr + 2 * c
    r_gl = r_ga + d_model

    def wt(*ranges):
        parts = [jnp.zeros((-r0, d_model), F32) if r0 < 0 else wt_ref[r0:r1, :]
                 for r0, r1 in ranges]
        w = parts[0] if len(parts) == 1 else jnp.concatenate(parts, axis=0)
        return w.astype(BF16)

    k_rows, v_rows = (r_k, r_v), (r_v, r_qi)
    att_rows = (((r_q, r_k),), (k_rows, v_rows, v_rows, k_rows),
                ((r_qi, r_ki),), ((r_ki, r_wi),) * (LANES // IDX_DIM),
                ((r_wi, r_xr), (IDX_HEADS - LANES, 0)))

    def att_operand(i):
        z = _dot_nt(h, wt(*att_rows[i]))
        if scales[i] is not None:
            z = z * scales[i]
        _rows_to_seq(att_refs[i], z, slab_scr, _slabs(sum(ATT_WIDTHS[:i])),
                     batch)

    def gate_logits(o_ref):
        r0 = r_ga if o_ref is ga_ref else r_gl
        o_ref[...] = _dot_nt(h, wt((r0, r0 + d_model))).astype(o_ref.dtype)

    fillers = (lambda: gate_logits(ga_ref),
               lambda: gate_logits(gl_ref),
               lambda: (att_operand(0), att_operand(1)),
               lambda: (att_operand(2), att_operand(3), att_operand(4)))

    n_tiles = c // MXU_DIM
    assert n_tiles == len(fillers)
    for n in range(n_tiles):
        cs = slice(n * MXU_DIM, (n + 1) * MXU_DIM)
        xr = _dot_nt(h, wt((r_xr + cs.start, r_xr + cs.stop)))
        gr = _dot_nt(h, wt((r_gr + cs.start, r_gr + cs.stop)))
        xs_scr[halo:halo + tm, cs] = xr
        xc = cb_ref[:, cs]
        for j in range(CONV_W):
            xc = xc + cw_ref[j:j + 1, cs] * xs_scr[j * batch:j * batch + tm, cs]
        xs_scr[0:halo, cs] = xr[tm - halo:tm, :]

        xb = xc.astype(BF16)
        r = jax.nn.sigmoid(_dot(xb, wa_ref[n]) + ba_ref[:, cs])
        gate_i = jax.nn.sigmoid(_dot(xb, wx_ref[n]) + bx_ref[:, cs])
        fillers[n]()
        neg_lam = -lam_ref[:, cs]
        softplus = (jnp.maximum(neg_lam, 0.0)
                    + jnp.log1p(jnp.exp(-jnp.abs(neg_lam))))
        log_a = -LRU_C * r * softplus
        a_scr[:, cs] = jnp.exp(log_a)
        th = jnp.tanh(log_a)
        u_scr[:, cs] = jnp.sqrt(-2.0 * th / (1.0 - th)) * gate_i * xc

        hcur = hc_scr[:, cs]
        for t in range(tm // batch):
            rs = slice(t * batch, (t + 1) * batch)
            hcur = a_scr[rs, cs] * hcur + u_scr[rs, cs]
            h_scr[rs, cs] = hcur
        hc_scr[:, cs] = hcur
        rnn_ref[:, cs] = (h_scr[:, cs] * jax.nn.gelu(gr)).astype(BF16)


def _mixer_in(x, g, wt, cw, cb, wa, ba, wx, bx, lam, layer, batch):
    rows, d = x.shape
    c = cw.shape[1]
    tm = ROW_TILE
    tt = tm // batch
    seq = rows // batch
    halo = (CONV_W - 1) * batch
    assert wt.shape[1] == (D_ATT + 2 * HEAD_DIM + IDX_HEADS * IDX_DIM + IDX_DIM
                           + IDX_HEADS + 2 * c + 2 * d)

    def row_spec(n):
        return pl.BlockSpec((tm, n), lambda i: (i, 0))

    def seq_spec(n):
        return pl.BlockSpec((tt, batch * n), lambda i: (i, 0))

    out_shape = [jax.ShapeDtypeStruct((seq, batch * n), t)
                 for n, t in zip(ATT_WIDTHS, ATT_DTYPES)]
    out_shape += [jax.ShapeDtypeStruct((rows, d), BF16)] * 2
    out_shape += [jax.ShapeDtypeStruct((rows, c), BF16)]
    out_specs = [seq_spec(n) for n in ATT_WIDTHS]
    out_specs += [row_spec(d), row_spec(d), row_spec(c)]
    return pl.pallas_call(
        functools.partial(_mixer_in_kernel, batch=batch),
        out_shape=out_shape,
        grid=(rows // tm,),
        in_specs=[row_spec(d), _resident((1, d)),
                  _resident(wt.shape[1:], layer), _resident(cw.shape),
                  _resident((1, c)), _resident(wa.shape[1:], layer),
                  _resident((1, c)), _resident(wx.shape[1:], layer),
                  _resident((1, c)), _resident((1, c))],
        out_specs=out_specs,
        scratch_shapes=[
            pltpu.VMEM((_slabs(sum(ATT_WIDTHS)), tm, LANES), F32),
            pltpu.VMEM((halo + tm, c), F32),
            pltpu.VMEM((tm, c), F32), pltpu.VMEM((tm, c), F32),
            pltpu.VMEM((tm, c), F32), pltpu.VMEM((batch, c), F32)],
        compiler_params=pltpu.CompilerParams(
            dimension_semantics=("arbitrary",), vmem_limit_bytes=VMEM_LIMIT),
        name="mixer_in",
    )(x, g, wt, cw, cb, wa, ba, wx, bx, lam)


VT_ROWS = HEAD_DIM + 16


def _key_to_float(key):
    bits = jnp.where(key < 0, INT_MIN - key, key)
    return pltpu.bitcast(bits, F32)


def _for_chunks(n, body):
    def quad(i, carry):
        for j in range(4):
            body(4 * i + j)
        return carry

    lax.fori_loop(0, n // 4, quad, 0)
    base = (n // 4) * 4

    @pl.when(n % 4 >= 2)
    def _():
        body(base)
        body(base + 1)

    @pl.when(n % 2 == 1)
    def _():
        body(n - 1)


def _fold_chunks(n, chunk, init):
    def pair(i, c):
        return chunk(2 * i + 1, chunk(2 * i, c))

    c = lax.fori_loop(0, n // 2, pair, init)
    return lax.cond(n % 2 == 1, lambda c: chunk(n - 1, c), lambda c: c, c)


def _attn_kernel(q_ref, qi_ref, wi_ref, kv_ref, ki_ref, o_ref,
                 qm_scr, qim_scr, wit_scr, vt_scr, sc_scr, scb_scr, sall_scr, m_scr,
                 acc_scr, mpart_scr, ties_scr, *, topk):
    tq, ck = Q_TILE, K_CHUNK
    n_kc = sc_scr.shape[0]
    seq = n_kc * ck
    qt = pl.program_id(1)
    n_chunks = (qt * tq) // ck + tq // ck
    lane = lax.broadcasted_iota(I32, (tq, LANES), 1)
    key_iota = lax.broadcasted_iota(I32, (ck, tq), 0)
    qry_iota = lax.broadcasted_iota(I32, (ck, tq), 1)
    kf = float(topk)

    for h in range(N_HEADS):
        qp = q_ref[:, (h // 2) * LANES:(h // 2 + 1) * LANES].astype(F32)
        lo = (h % 2) * HEAD_DIM
        qm_scr[h] = jnp.where((lane >= lo) & (lane < lo + HEAD_DIM), qp,
                              0.0).astype(BF16)
    for h in range(IDX_HEADS):
        qp = qi_ref[:, (h // 4) * LANES:(h // 4 + 1) * LANES].astype(F32)
        lo = (h % 4) * IDX_DIM
        qim_scr[h] = jnp.where((lane >= lo) & (lane < lo + IDX_DIM), qp,
                               0.0).astype(BF16)
    wit_scr[...] = jnp.transpose(wi_ref[...])

    @pl.when(qt == 0)
    def _():
        lane_c = lax.broadcasted_iota(I32, (ck, LANES), 1)
        for c in range(n_kc):
            v1 = jnp.where(lane_c < HEAD_DIM,
                           kv_ref[c * ck:(c + 1) * ck, LANES:2 * LANES].astype(F32),
                           1.0)
            vt_scr[c] = jnp.transpose(v1)[0:VT_ROWS, :].astype(BF16)

    def score_chunk(kc):
        k0 = pl.multiple_of(kc * ck, ck)
        kic = ki_ref[pl.ds(k0, ck), :]
        acc = jnp.zeros((ck, tq), F32)
        for h in range(IDX_HEADS):
            acc = acc + (jnp.maximum(_dot_nt(kic, qim_scr[h]), 0.0)
                         * wit_scr[h:h + 1, :])
        causal = key_iota - qry_iota <= qt * tq - kc * ck
        sc = jnp.where(causal, acc, -jnp.inf)
        sc_scr[kc] = sc
        scb_scr[kc] = sc.astype(BF16)

    _for_chunks(n_chunks, score_chunk)

    def count(pred):
        acc_rows = 2 * SUBLANES

        def chunk(kc, c):
            hit = jnp.where(pred(sc_scr[kc], kc), 1.0, 0.0)
            return c + jnp.sum(hit.reshape(ck // acc_rows, acc_rows, tq), axis=0)

        c = _fold_chunks(n_chunks, chunk, jnp.zeros((acc_rows, tq), F32))
        return jnp.sum(c, axis=0, keepdims=True)

    def count_bf16(cand):
        acc_rows = 2 * BF16_SUBLANES
        one, zero = jnp.ones((), BF16), jnp.zeros((), BF16)

        def chunk(kc, c):
            hit = jnp.where(scb_scr[kc] >= cand, one, zero)
            part = hit[0:acc_rows]
            for r0 in range(acc_rows, ck, acc_rows):
                part = part + hit[r0:r0 + acc_rows]
            return c + part.astype(F32)

        c = _fold_chunks(n_chunks, chunk, jnp.zeros((acc_rows, tq), F32))
        return jnp.sum(c, axis=0, keepdims=True)

    def coarse_step(i, thr16):
        cand16 = thr16 + lax.shift_left(jnp.int32(1), 15 - i)
        cand_f = _key_to_float(cand16 * 65536)
        tot = count_bf16(cand_f.astype(BF16))
        return jnp.where(tot >= kf, cand16, thr16)

    thr16 = lax.fori_loop(0, 16, coarse_step,
                          jnp.full((1, tq), -2 ** 15, I32))

    def fine_step(i, carry):
        thr, above = carry
        cand = thr + lax.shift_left(jnp.int32(1), 16 - i)
        cand_f = _key_to_float(cand)
        tot = count(lambda s, kc: s >= cand_f)
        ok = tot >= kf
        return jnp.where(ok, cand, thr), jnp.where(ok, above, tot)

    thr, above = lax.fori_loop(
        0, 17, fine_step,
        ((thr16 - 1) * 65536, jnp.full((1, tq), -1.0, F32)))

    thr_f = _key_to_float(thr)
    short = jnp.logical_not(thr_f > -jnp.inf)
    thr_f = jnp.where(short, -jnp.inf, thr_f)
    n_gt = lax.cond(jnp.min(above) < 0.0,
                    lambda: count(lambda s, kc: s > thr_f), lambda: above)
    need = jnp.where(short, 0.0, kf - n_gt)
    lower_tri = jnp.where(
        lax.broadcasted_iota(I32, (ck, ck), 0)
        >= lax.broadcasted_iota(I32, (ck, ck), 1), 1.0, 0.0).astype(BF16)

    mpart_scr[...] = jnp.full(mpart_scr.shape, MASK_BIAS, F32)
    acc_scr[...] = jnp.zeros(acc_scr.shape, F32)

    ties_scr[...] = jnp.zeros(ties_scr.shape, F32)

    def scores_chunk(kc):
        k0 = pl.multiple_of(kc * ck, ck)
        s_idx = sc_scr[kc]
        tie = s_idx == thr_f
        tie_f = jnp.where(tie, 1.0, 0.0)
        ties_before = ties_scr[...]
        ties_scr[...] = ties_before + jnp.sum(tie_f, axis=0, keepdims=True)
        rank = ties_before + _dot(lower_tri, tie_f.astype(BF16))
        keep = (s_idx > thr_f) | (tie & (rank <= need))
        bias = jnp.where(keep, 0.0, MASK_BIAS)
        for h in range(N_HEADS):
            c0 = 0 if h % 2 == 0 else LANES
            k_mat = kv_ref[pl.ds(k0, ck), c0:c0 + LANES]
            s = _dot_nt(k_mat, qm_scr[h]) + bias
            sall_scr[kc, h] = s
            mpart_scr[h] = jnp.maximum(
                mpart_scr[h],
                jnp.max(s.reshape(ck // SUBLANES, SUBLANES, tq), axis=0))

    _for_chunks(n_chunks, scores_chunk)
    for h in range(N_HEADS):
        m_scr[h:h + 1, :] = jnp.max(mpart_scr[h], axis=0, keepdims=True)

    def value_chunk(kc):
        vt_c = vt_scr[kc]
        for h in range(N_HEADS):
            p = jnp.exp(sall_scr[kc, h] - m_scr[h:h + 1, :]).astype(BF16)
            acc_scr[h] = acc_scr[h] + _dot(vt_c, p)

    _for_chunks(n_chunks, value_chunk)

    outs = []
    for h in range(N_HEADS):
        a = acc_scr[h]
        outs.append(a[0:HEAD_DIM, :] / a[HEAD_DIM:HEAD_DIM + 1, :])
    o_ref[...] = jnp.transpose(jnp.concatenate(outs, axis=0)).astype(BF16)


def _attention(q, qi, wi, kv, ki, batch):
    tq, ck = Q_TILE, K_CHUNK
    seq = q.shape[0]
    topk = min(TOPK_MAX, seq // 4)
    nq = seq // tq

    def q_spec(c):
        return pl.BlockSpec((tq, c), lambda b, i: (i, b))

    def kv_spec(c):
        return pl.BlockSpec((seq, c), lambda b, i: (0, b))

    return pl.pallas_call(
        functools.partial(_attn_kernel, topk=topk),
        out_shape=jax.ShapeDtypeStruct((seq, batch * D_ATT), BF16),
        grid=(batch, nq),
        in_specs=[q_spec(D_ATT), q_spec(IDX_HEADS * IDX_DIM), q_spec(LANES),
                  kv_spec(4 * HEAD_DIM), kv_spec(LANES)],
        out_specs=q_spec(D_ATT),
        scratch_shapes=[
            pltpu.VMEM((N_HEADS, tq, LANES), BF16),
            pltpu.VMEM((IDX_HEADS, tq, LANES), BF16),
            pltpu.VMEM((LANES, tq), F32),
            pltpu.VMEM((seq // ck, VT_ROWS, ck), BF16),
            pltpu.VMEM((seq // ck, ck, tq), F32),
            pltpu.VMEM((seq // ck, ck, tq), BF16),
            pltpu.VMEM((seq // ck, N_HEADS, ck, tq), F32),
            pltpu.VMEM((N_HEADS, tq), F32),
            pltpu.VMEM((N_HEADS, VT_ROWS, tq), F32),
            pltpu.VMEM((N_HEADS, SUBLANES, tq), F32),
            pltpu.VMEM((1, tq), F32),
        ],
        compiler_params=pltpu.CompilerParams(
            dimension_semantics=("parallel", "arbitrary"),
            vmem_limit_bytes=VMEM_LIMIT),
        name="dsa_attention",
    )(q, qi, wi, kv, ki)


def _merge_kernel(att_ref, rnn_ref, ga_ref, gl_ref, x_ref, wa_ref, wr_ref,
                  wo_ref, o_ref, slab_scr, *, batch):
    att = _seq_to_rows(att_ref, slab_scr, batch).astype(BF16)
    pa = _dot(att, wa_ref[...])
    pr = _dot(rnn_ref[...], wr_ref[...])
    merged = (jax.nn.sigmoid(ga_ref[...].astype(F32)) * pa
              + jax.nn.sigmoid(gl_ref[...].astype(F32)) * pr)
    o_ref[...] = x_ref[...] + _dot(merged.astype(BF16), wo_ref[...])


def _merge(att, rnn, ga, gl, x, wa, wr, wo, layer, batch):
    rows, d = x.shape
    tt = MERGE_ROW_TILE // batch

    def row_spec(c):
        return pl.BlockSpec((MERGE_ROW_TILE, c), lambda i: (i, 0))

    return pl.pallas_call(
        functools.partial(_merge_kernel, batch=batch),
        out_shape=jax.ShapeDtypeStruct((rows, d), F32),
        grid=(rows // MERGE_ROW_TILE,),
        in_specs=[pl.BlockSpec((tt, att.shape[1]), lambda i: (i, 0)),
                  row_spec(rnn.shape[1]), row_spec(d), row_spec(d),
                  row_spec(d), _resident(wa.shape[1:], layer),
                  _resident(wr.shape[1:], layer),
                  _resident(wo.shape[1:], layer)],
        out_specs=row_spec(d),
        scratch_shapes=[pltpu.VMEM((_slabs(D_ATT), MERGE_ROW_TILE, LANES), F32)],
        compiler_params=pltpu.CompilerParams(
            dimension_semantics=("parallel",), vmem_limit_bytes=VMEM_LIMIT),
        name="merge",
    )(att, rnn, ga, gl, x, wa, wr, wo)


def _block_diag_tiles(w):
    depth, n_blocks, bw, _ = w.shape
    per = MXU_DIM // bw
    w = w.reshape(depth, n_blocks // per, per, bw, bw)
    eye = jnp.eye(per, dtype=w.dtype)
    t = w[:, :, :, :, None, :] * eye[None, None, :, None, :, None]
    return t.reshape(depth, n_blocks // per, MXU_DIM, MXU_DIM).astype(BF16)


def kernel(x, ffn1_norm, ffn1_wg, ffn1_wu, ffn1_wd, mix_norm, w_in, conv_w,
           conv_b, rg_wa, rg_ba, rg_wx, rg_bx, rg_lam, w_att_proj, w_rnn_proj,
           w_out, ffn2_norm, ffn2_wg, ffn2_wu, ffn2_wd, final_norm):
    batch, seq, d = x.shape
    depth = ffn1_norm.shape[0]
    assert batch == SUBLANES and seq % Q_TILE == 0
    assert (batch * seq) % MERGE_ROW_TILE == 0 and ROW_TILE % batch == 0

    def row(v):
        return v.reshape(1, -1)

    wg1, wu1, wd1 = ffn1_wg, ffn1_wu, ffn1_wd
    wg2, wu2, wd2 = ffn2_wg, ffn2_wu, ffn2_wd
    w_in_t = jnp.swapaxes(w_in, 1, 2)
    wa_t, wx_t = _block_diag_tiles(rg_wa), _block_diag_tiles(rg_wx)
    w_ap, w_rp, w_o = (w.astype(BF16) for w in (w_att_proj, w_rnn_proj, w_out))

    h = x
    for l in range(depth):
        h = _ffn(h, row(ffn1_norm[l]), wg1, wu1, wd1, l, batch, first=(l == 0))
        q, kv, qi, ki, wi, ga, gl, rnn = _mixer_in(
            h, row(mix_norm[l]), w_in_t, conv_w[l], row(conv_b[l]), wa_t,
            row(rg_ba[l]), wx_t, row(rg_bx[l]), row(rg_lam[l]), l, batch)
        att = _attention(q, qi, wi, kv, ki, batch)
        h = _merge(att, rnn, ga, gl, h, w_ap, w_rp, w_o, l, batch)
        last = l == depth - 1
        h = _ffn(h, row(ffn2_norm[l]), wg2, wu2, wd2, l, batch,
                 final_g=row(final_norm) if last else None)
    return h
```

```python
import functools

import jax
import jax.numpy as jnp
from jax import lax
from jax.experimental import pallas as pl
from jax.experimental.pallas import tpu as pltpu

F32 = jnp.float32
BF16 = jnp.bfloat16
I32 = jnp.int32

N_HEADS = 8
HEAD_DIM = 64
D_ATT = N_HEADS * HEAD_DIM
IDX_HEADS = 8
IDX_DIM = 32
TOPK_MAX = 256
CONV_W = 4
LRU_C = 8.0
EPS = 1e-6

LANES = 128
SUBLANES = 8
BF16_SUBLANES = 16
MXU_DIM = 256
VMEM_LIMIT = 56 * 1024 * 1024

INT_MIN = -2 ** 31
MASK_BIAS = -1e30

ROW_TILE = 512
MERGE_ROW_TILE = 1024
Q_TILE = 256
K_CHUNK = 256


def _resident(shape, layer=None):
    zeros = (0,) * len(shape)
    if layer is None:
        return pl.BlockSpec(shape, lambda *_: zeros,
                            pipeline_mode=pl.Buffered(1))
    return pl.BlockSpec((None,) + tuple(shape), lambda *_: (layer,) + zeros,
                        pipeline_mode=pl.Buffered(1))


def _rmsnorm(x, g):
    ms = jnp.mean(x * x, axis=-1, keepdims=True)
    return x * lax.rsqrt(ms + EPS) * g


def _dot(a, b):
    return jnp.dot(a, b, preferred_element_type=F32)


def _dot_nt(a, b):
    return lax.dot_general(a, b, (((1,), (1,)), ((), ())),
                           preferred_element_type=F32)


def _ff_chunks(d_ff):
    chunks, c0 = [], 0
    while c0 < d_ff:
        c1 = min(c0 + MXU_DIM, d_ff)
        chunks.append((c0, c1))
        c0 = c1
    return chunks


def _slabs(width):
    return width // LANES


def _batch_to_rows(x_ref, slab_scr, batch):
    tt, d = x_ref.shape[1:]
    for b in range(batch):
        for j in range(_slabs(d)):
            slab_scr[j, pl.ds(b, tt, stride=batch), :] = (
                x_ref[b, :, j * LANES:(j + 1) * LANES])
    return jnp.concatenate([slab_scr[j] for j in range(_slabs(d))], axis=1)


def _rows_to_batch(o_ref, y, slab_scr, batch):
    tt, d = o_ref.shape[1:]
    for j in range(_slabs(d)):
        slab_scr[j] = y[:, j * LANES:(j + 1) * LANES]
    for b in range(batch):
        for j in range(_slabs(d)):
            o_ref[b, :, j * LANES:(j + 1) * LANES] = (
                slab_scr[j, pl.ds(b, tt, stride=batch), :])


def _rows_to_seq(o_ref, z, slab_scr, slab0, batch):
    tt = o_ref.shape[0]
    c = z.shape[1]
    for j in range(_slabs(c)):
        slab_scr[slab0 + j] = z[:, j * LANES:(j + 1) * LANES]
    for b in range(batch):
        for j in range(_slabs(c)):
            o_ref[:, b * c + j * LANES:b * c + (j + 1) * LANES] = (
                slab_scr[slab0 + j, pl.ds(b, tt, stride=batch), :]
                .astype(o_ref.dtype))


def _seq_to_rows(x_ref, slab_scr, batch):
    tt = x_ref.shape[0]
    c = x_ref.shape[1] // batch
    for b in range(batch):
        for j in range(_slabs(c)):
            slab_scr[j, pl.ds(b, tt, stride=batch), :] = (
                x_ref[:, b * c + j * LANES:b * c + (j + 1) * LANES].astype(F32))
    return jnp.concatenate([slab_scr[j] for j in range(_slabs(c))], axis=1)


def _ffn_kernel(x_ref, g_ref, wg_ref, wu_ref, wd_ref, *rest, first, final,
                batch):
    rest = list(rest)
    fg_ref = rest.pop(0) if final else None
    o_ref = rest.pop(0)
    slab_scr = rest.pop(0) if (first or final) else None
    x = _batch_to_rows(x_ref, slab_scr, batch) if first else x_ref[...]
    h = _rmsnorm(x, g_ref[...]).astype(BF16)
    acc = None
    for c0, c1 in _ff_chunks(wg_ref.shape[1]):
        g = _dot(h, wg_ref[:, c0:c1].astype(BF16))
        u = _dot(h, wu_ref[:, c0:c1].astype(BF16))
        a = (g * jax.nn.sigmoid(g) * u).astype(BF16)
        d = _dot(a, wd_ref[c0:c1, :].astype(BF16))
        acc = d if acc is None else acc + d
    y = x + 0.5 * acc
    if final:
        _rows_to_batch(o_ref, _rmsnorm(y, fg_ref[...]), slab_scr, batch)
    else:
        o_ref[...] = y


def _ffn(x, g, wg, wu, wd, layer, batch, first=False, final_g=None):
    final = final_g is not None
    if first:
        _, seq, d = x.shape
        rows = seq * batch
    else:
        rows, d = x.shape
        seq = rows // batch
    d_ff = wg.shape[2]
    tt = ROW_TILE // batch
    row_spec = pl.BlockSpec((ROW_TILE, d), lambda i: (i, 0))
    batch_spec = pl.BlockSpec((batch, tt, d), lambda i: (0, i, 0))
    in_specs = [batch_spec if first else row_spec, _resident((1, d)),
                _resident((d, d_ff), layer), _resident((d, d_ff), layer),
                _resident((d_ff, d), layer)]
    args = [x, g, wg, wu, wd]
    if final:
        in_specs.append(_resident((1, d)))
        args.append(final_g)
    scratch = ([pltpu.VMEM((_slabs(d), ROW_TILE, LANES), F32)]
               if (first or final) else [])
    out_shape = (batch, seq, d) if final else (rows, d)
    return pl.pallas_call(
        functools.partial(_ffn_kernel, first=first, final=final, batch=batch),
        out_shape=jax.ShapeDtypeStruct(out_shape, F32),
        grid=(rows // ROW_TILE,),
        in_specs=in_specs,
        out_specs=batch_spec if final else row_spec,
        scratch_shapes=scratch,
        compiler_params=pltpu.CompilerParams(
            dimension_semantics=("parallel",), vmem_limit_bytes=VMEM_LIMIT),
        name="ffn_first" if first else ("ffn_final" if final else "ffn"),
    )(*args)


ATT_WIDTHS = (D_ATT, 4 * HEAD_DIM, IDX_HEADS * IDX_DIM, LANES, LANES)
ATT_DTYPES = (BF16, BF16, BF16, BF16, F32)


def _mixer_in_kernel(x_ref, g_ref, wt_ref, cw_ref, cb_ref, wa_ref, ba_ref,
                     wx_ref, bx_ref, lam_ref,
                     q_ref, kv_ref, qi_ref, ki_ref, wi_ref, ga_ref, gl_ref,
                     rnn_ref, slab_scr, xs_scr, a_scr, u_scr, h_scr, hc_scr,
                     *, batch):
    tm = x_ref.shape[0]
    c = cw_ref.shape[1]
    d_model = ga_ref.shape[1]
    halo = (CONV_W - 1) * batch
    h = _rmsnorm(x_ref[...], g_ref[...]).astype(BF16)

    @pl.when(pl.program_id(0) == 0)
    def _():
        xs_scr[0:halo, :] = jnp.zeros((halo, c), F32)
        hc_scr[...] = jnp.zeros(hc_scr.shape, F32)

    scales = (HEAD_DIM ** -0.5, None, None, None, (IDX_HEADS * IDX_DIM) ** -0.5)
    att_refs = (q_ref, kv_ref, qi_ref, ki_ref, wi_ref)
    r_q, r_k, r_v = 0, D_ATT, D_ATT + HEAD_DIM
    r_qi = r_v + HEAD_DIM
    r_ki = r_qi + IDX_HEADS * IDX_DIM
    r_wi = r_ki + IDX_DIM
    r_xr = r_wi + IDX_HEADS
    r_gr, r_ga = r_xr + c, r_xr + 2 * c
    r_gl = r_ga + d_model

    def wt(*ranges):
        parts = [jnp.zeros((-r0, d_model), F32) if r0 < 0 else wt_ref[r0:r1, :]
                 for r0, r1 in ranges]
        w = parts[0] if len(parts) == 1 else jnp.concatenate(parts, axis=0)
        return w.astype(BF16)

    k_rows, v_rows = (r_k, r_v), (r_v, r_qi)
    att_rows = (((r_q, r_k),), (k_rows, v_rows, v_rows, k_rows),
                ((r_qi, r_ki),), ((r_ki, r_wi),) * (LANES // IDX_DIM),
                ((r_wi, r_xr), (IDX_HEADS - LANES, 0)))

    def att_operand(i):
        z = _dot_nt(h, wt(*att_rows[i]))
        if scales[i] is not None:
            z = z * scales[i]
        _rows_to_seq(att_refs[i], z, slab_scr, _slabs(sum(ATT_WIDTHS[:i])),
                     batch)

    def gate_logits(o_ref):
        r0 = r_ga if o_ref is ga_ref else r_gl
        o_ref[...] = _dot_nt(h, wt((r0, r0 + d_model))).astype(o_ref.dtype)

    fillers = (lambda: gate_logits(ga_ref),
               lambda: gate_logits(gl_ref),
               lambda: (att_operand(0), att_operand(1)),
               lambda: (att_operand(2), att_operand(3), att_operand(4)))

    n_tiles = c // MXU_DIM
    assert n_tiles == len(fillers)
    for n in range(n_tiles):
        cs = slice(n * MXU_DIM, (n + 1) * MXU_DIM)
        xr = _dot_nt(h, wt((r_xr + cs.start, r_xr + cs.stop)))
        gr = _dot_nt(h, wt((r_gr + cs.start, r_gr + cs.stop)))
        xs_scr[halo:halo + tm, cs] = xr
        xc = cb_ref[:, cs]
        for j in range(CONV_W):
            xc = xc + cw_ref[j:j + 1, cs] * xs_scr[j * batch:j * batch + tm, cs]
        xs_scr[0:halo, cs] = xr[tm - halo:tm, :]

        xb = xc.astype(BF16)
        r = jax.nn.sigmoid(_dot(xb, wa_ref[n]) + ba_ref[:, cs])
        gate_i = jax.nn.sigmoid(_dot(xb, wx_ref[n]) + bx_ref[:, cs])
        fillers[n]()
        neg_lam = -lam_ref[:, cs]
        softplus = (jnp.maximum(neg_lam, 0.0)
                    + jnp.log1p(jnp.exp(-jnp.abs(neg_lam))))
        log_a = r * (-LRU_C * softplus)
        a_scr[:, cs] = jnp.exp(log_a)
        th = jnp.tanh(log_a)
        u_scr[:, cs] = jnp.sqrt(-2.0 * th / (1.0 - th)) * gate_i * xc

        hcur = hc_scr[:, cs]
        for t in range(tm // batch):
            rs = slice(t * batch, (t + 1) * batch)
            hcur = a_scr[rs, cs] * hcur + u_scr[rs, cs]
            h_scr[rs, cs] = hcur
        hc_scr[:, cs] = hcur
        rnn_ref[:, cs] = (h_scr[:, cs] * jax.nn.gelu(gr)).astype(BF16)


def _mixer_in(x, g, wt, cw, cb, wa, ba, wx, bx, lam, layer, batch):
    rows, d = x.shape
    c = cw.shape[1]
    tm = ROW_TILE
    tt = tm // batch
    seq = rows // batch
    halo = (CONV_W - 1) * batch
    assert wt.shape[1] == (D_ATT + 2 * HEAD_DIM + IDX_HEADS * IDX_DIM + IDX_DIM
                           + IDX_HEADS + 2 * c + 2 * d)

    def row_spec(n):
        return pl.BlockSpec((tm, n), lambda i: (i, 0))

    def seq_spec(n):
        return pl.BlockSpec((tt, batch * n), lambda i: (i, 0))

    out_shape = [jax.ShapeDtypeStruct((seq, batch * n), t)
                 for n, t in zip(ATT_WIDTHS, ATT_DTYPES)]
    out_shape += [jax.ShapeDtypeStruct((rows, d), BF16)] * 2
    out_shape += [jax.ShapeDtypeStruct((rows, c), BF16)]
    out_specs = [seq_spec(n) for n in ATT_WIDTHS]
    out_specs += [row_spec(d), row_spec(d), row_spec(c)]
    return pl.pallas_call(
        functools.partial(_mixer_in_kernel, batch=batch),
        out_shape=out_shape,
        grid=(rows // tm,),
        in_specs=[row_spec(d), _resident((1, d)),
                  _resident(wt.shape[1:], layer), _resident(cw.shape),
                  _resident((1, c)), _resident(wa.shape[1:], layer),
                  _resident((1, c)), _resident(wx.shape[1:], layer),
                  _resident((1, c)), _resident((1, c))],
        out_specs=out_specs,
        scratch_shapes=[
            pltpu.VMEM((_slabs(sum(ATT_WIDTHS)), tm, LANES), F32),
            pltpu.VMEM((halo + tm, c), F32),
            pltpu.VMEM((tm, c), F32), pltpu.VMEM((tm, c), F32),
            pltpu.VMEM((tm, c), F32), pltpu.VMEM((batch, c), F32)],
        compiler_params=pltpu.CompilerParams(
            dimension_semantics=("arbitrary",), vmem_limit_bytes=VMEM_LIMIT),
        name="mixer_in",
    )(x, g, wt, cw, cb, wa, ba, wx, bx, lam)


VT_ROWS = HEAD_DIM + 16


def _key_to_float(key):
    bits = jnp.where(key < 0, INT_MIN - key, key)
    return pltpu.bitcast(bits, F32)


def _for_chunks(n, body):
    def quad(i, carry):
        for j in range(4):
            body(4 * i + j)
        return carry

    lax.fori_loop(0, n // 4, quad, 0)
    base = (n // 4) * 4

    @pl.when(n % 4 >= 2)
    def _():
        body(base)
        body(base + 1)

    @pl.when(n % 2 == 1)
    def _():
        body(n - 1)


def _fold_chunks(n, chunk, init):
    def pair(i, c):
        return chunk(2 * i + 1, chunk(2 * i, c))

    c = lax.fori_loop(0, n // 2, pair, init)
    return lax.cond(n % 2 == 1, lambda c: chunk(n - 1, c), lambda c: c, c)


def _attn_kernel(q_ref, qi_ref, wi_ref, kv_ref, ki_ref, o_ref,
                 qm_scr, qim_scr, wit_scr, vt_scr, sc_scr, scb_scr, sall_scr, m_scr,
                 acc_scr, mpart_scr, ties_scr, *, topk):
    tq, ck = Q_TILE, K_CHUNK
    n_kc = sc_scr.shape[0]
    seq = n_kc * ck
    qt = pl.program_id(1)
    n_chunks = (qt * tq) // ck + tq // ck
    lane = lax.broadcasted_iota(I32, (tq, LANES), 1)
    key_iota = lax.broadcasted_iota(I32, (ck, tq), 0)
    qry_iota = lax.broadcasted_iota(I32, (ck, tq), 1)
    kf = float(topk)

    for h in range(N_HEADS):
        qp = q_ref[:, (h // 2) * LANES:(h // 2 + 1) * LANES].astype(F32)
        lo = (h % 2) * HEAD_DIM
        qm_scr[h] = jnp.where((lane >= lo) & (lane < lo + HEAD_DIM), qp,
                              0.0).astype(BF16)
    for h in range(IDX_HEADS):
        qp = qi_ref[:, (h // 4) * LANES:(h // 4 + 1) * LANES].astype(F32)
        lo = (h % 4) * IDX_DIM
        qim_scr[h] = jnp.where((lane >= lo) & (lane < lo + IDX_DIM), qp,
                               0.0).astype(BF16)
    wit_scr[...] = jnp.transpose(wi_ref[...])

    @pl.when(qt == 0)
    def _():
        lane_c = lax.broadcasted_iota(I32, (ck, LANES), 1)
        for c in range(n_kc):
            v1 = jnp.where(lane_c < HEAD_DIM,
                           kv_ref[c * ck:(c + 1) * ck, LANES:2 * LANES].astype(F32),
                           1.0)
            vt_scr[c] = jnp.transpose(v1)[0:VT_ROWS, :].astype(BF16)

    def score_chunk(kc):
        k0 = pl.multiple_of(kc * ck, ck)
        kic = ki_ref[pl.ds(k0, ck), :]
        acc = jnp.zeros((ck, tq), F32)
        for h in range(IDX_HEADS):
            acc = acc + (jnp.maximum(_dot_nt(kic, qim_scr[h]), 0.0)
                         * wit_scr[h:h + 1, :])
        causal = key_iota - qry_iota <= qt * tq - kc * ck
        sc = jnp.where(causal, acc, -jnp.inf)
        sc_scr[kc] = sc
        scb_scr[kc] = sc.astype(BF16)

    _for_chunks(n_chunks, score_chunk)

    def count(pred):
        acc_rows = 2 * SUBLANES

        def chunk(kc, c):
            hit = jnp.where(pred(sc_scr[kc], kc), 1.0, 0.0)
            return c + jnp.sum(hit.reshape(ck // acc_rows, acc_rows, tq), axis=0)

        c = _fold_chunks(n_chunks, chunk, jnp.zeros((acc_rows, tq), F32))
        return jnp.sum(c, axis=0, keepdims=True)

    def count_bf16(cand):
        acc_rows = 2 * BF16_SUBLANES
        one, zero = jnp.ones((), BF16), jnp.zeros((), BF16)

        def chunk(kc, c):
            hit = jnp.where(scb_scr[kc] >= cand, one, zero)
            part = hit[0:acc_rows]
            for r0 in range(acc_rows, ck, acc_rows):
                part = part + hit[r0:r0 + acc_rows]
            return c + part.astype(F32)

        c = _fold_chunks(n_chunks, chunk, jnp.zeros((acc_rows, tq), F32))
        return jnp.sum(c, axis=0, keepdims=True)

    def coarse_step(i, thr16):
        cand16 = thr16 + lax.shift_left(jnp.int32(1), 15 - i)
        cand_f = _key_to_float(cand16 * 65536)
        tot = count_bf16(cand_f.astype(BF16))
        return jnp.where(tot >= kf, cand16, thr16)

    thr16 = lax.fori_loop(0, 16, coarse_step,
                          jnp.full((1, tq), -2 ** 15, I32))

    def fine_step(i, carry):
        thr, above = carry
        cand = thr + lax.shift_left(jnp.int32(1), 16 - i)
        cand_f = _key_to_float(cand)
        tot = count(lambda s, kc: s >= cand_f)
        ok = tot >= kf
        return jnp.where(ok, cand, thr), jnp.where(ok, above, tot)

    thr, above = lax.fori_loop(
        0, 17, fine_step,
        ((thr16 - 1) * 65536, jnp.full((1, tq), -1.0, F32)))

    thr_f = _key_to_float(thr)
    short = jnp.logical_not(thr_f > -jnp.inf)
    thr_f = jnp.where(short, -jnp.inf, thr_f)
    n_gt = lax.cond(jnp.min(above) < 0.0,
                    lambda: count(lambda s, kc: s > thr_f), lambda: above)
    need = jnp.where(short, 0.0, kf - n_gt)
    lower_tri = jnp.where(
        lax.broadcasted_iota(I32, (ck, ck), 0)
        >= lax.broadcasted_iota(I32, (ck, ck), 1), 1.0, 0.0).astype(BF16)

    mpart_scr[...] = jnp.full(mpart_scr.shape, MASK_BIAS, F32)
    acc_scr[...] = jnp.zeros(acc_scr.shape, F32)

    ties_scr[...] = jnp.zeros(ties_scr.shape, F32)

    def scores_chunk(kc):
        k0 = pl.multiple_of(kc * ck, ck)
        s_idx = sc_scr[kc]
        tie = s_idx == thr_f
        tie_f = jnp.where(tie, 1.0, 0.0)
        ties_before = ties_scr[...]
        ties_scr[...] = ties_before + jnp.sum(tie_f, axis=0, keepdims=True)
        rank = ties_before + _dot(lower_tri, tie_f.astype(BF16))
        keep = (s_idx > thr_f) | (tie & (rank <= need))
        bias = jnp.where(keep, 0.0, MASK_BIAS)
        for h in range(N_HEADS):
            c0 = 0 if h % 2 == 0 else LANES
            k_mat = kv_ref[pl.ds(k0, ck), c0:c0 + LANES]
            s = _dot_nt(k_mat, qm_scr[h]) + bias
            sall_scr[kc, h] = s
            mpart_scr[h] = jnp.maximum(
                mpart_scr[h],
                jnp.max(s.reshape(ck // SUBLANES, SUBLANES, tq), axis=0))

    _for_chunks(n_chunks, scores_chunk)
    for h in range(N_HEADS):
        m_scr[h:h + 1, :] = jnp.max(mpart_scr[h], axis=0, keepdims=True)

    def value_chunk(kc):
        vt_c = vt_scr[kc]
        for h in range(N_HEADS):
            p = jnp.exp(sall_scr[kc, h] - m_scr[h:h + 1, :]).astype(BF16)
            acc_scr[h] = acc_scr[h] + _dot(vt_c, p)

    _for_chunks(n_chunks, value_chunk)

    outs = []
    for h in range(N_HEADS):
        a = acc_scr[h]
        outs.append(a[0:HEAD_DIM, :] / a[HEAD_DIM:HEAD_DIM + 1, :])
    o_ref[...] = jnp.transpose(jnp.concatenate(outs, axis=0)).astype(BF16)


def _attention(q, qi, wi, kv, ki, batch):
    tq, ck = Q_TILE, K_CHUNK
    seq = q.shape[0]
    topk = min(TOPK_MAX, seq // 4)
    nq = seq // tq

    def q_spec(c):
        return pl.BlockSpec((tq, c), lambda b, i: (i, b))

    def kv_spec(c):
        return pl.BlockSpec((seq, c), lambda b, i: (0, b))

    return pl.pallas_call(
        functools.partial(_attn_kernel, topk=topk),
        out_shape=jax.ShapeDtypeStruct((seq, batch * D_ATT), BF16),
        grid=(batch, nq),
        in_specs=[q_spec(D_ATT), q_spec(IDX_HEADS * IDX_DIM), q_spec(LANES),
                  kv_spec(4 * HEAD_DIM), kv_spec(LANES)],
        out_specs=q_spec(D_ATT),
        scratch_shapes=[
            pltpu.VMEM((N_HEADS, tq, LANES), BF16),
            pltpu.VMEM((IDX_HEADS, tq, LANES), BF16),
            pltpu.VMEM((LANES, tq), F32),
            pltpu.VMEM((seq // ck, VT_ROWS, ck), BF16),
            pltpu.VMEM((seq // ck, ck, tq), F32),
            pltpu.VMEM((seq // ck, ck, tq), BF16),
            pltpu.VMEM((seq // ck, N_HEADS, ck, tq), F32),
            pltpu.VMEM((N_HEADS, tq), F32),
            pltpu.VMEM((N_HEADS, VT_ROWS, tq), F32),
            pltpu.VMEM((N_HEADS, SUBLANES, tq), F32),
            pltpu.VMEM((1, tq), F32),
        ],
        compiler_params=pltpu.CompilerParams(
            dimension_semantics=("parallel", "arbitrary"),
            vmem_limit_bytes=VMEM_LIMIT),
        name="dsa_attention",
    )(q, qi, wi, kv, ki)


def _merge_kernel(att_ref, rnn_ref, ga_ref, gl_ref, x_ref, wa_ref, wr_ref,
                  wo_ref, o_ref, slab_scr, *, batch):
    att = _seq_to_rows(att_ref, slab_scr, batch).astype(BF16)
    pa = _dot(att, wa_ref[...].astype(BF16))
    pr = _dot(rnn_ref[...], wr_ref[...].astype(BF16))
    merged = (jax.nn.sigmoid(ga_ref[...].astype(F32)) * pa
              + jax.nn.sigmoid(gl_ref[...].astype(F32)) * pr)
    o_ref[...] = x_ref[...] + _dot(merged.astype(BF16),
                                   wo_ref[...].astype(BF16))


def _merge(att, rnn, ga, gl, x, wa, wr, wo, layer, batch):
    rows, d = x.shape
    tt = MERGE_ROW_TILE // batch

    def row_spec(c):
        return pl.BlockSpec((MERGE_ROW_TILE, c), lambda i: (i, 0))

    return pl.pallas_call(
        functools.partial(_merge_kernel, batch=batch),
        out_shape=jax.ShapeDtypeStruct((rows, d), F32),
        grid=(rows // MERGE_ROW_TILE,),
        in_specs=[pl.BlockSpec((tt, att.shape[1]), lambda i: (i, 0)),
                  row_spec(rnn.shape[1]), row_spec(d), row_spec(d),
                  row_spec(d), _resident(wa.shape[1:], layer),
                  _resident(wr.shape[1:], layer),
                  _resident(wo.shape[1:], layer)],
        out_specs=row_spec(d),
        scratch_shapes=[pltpu.VMEM((_slabs(D_ATT), MERGE_ROW_TILE, LANES), F32)],
        compiler_params=pltpu.CompilerParams(
            dimension_semantics=("parallel",), vmem_limit_bytes=VMEM_LIMIT),
        name="merge",
    )(att, rnn, ga, gl, x, wa, wr, wo)


def _block_diag_tiles(w):
    depth, n_blocks, bw, _ = w.shape
    per = MXU_DIM // bw
    w = w.reshape(depth, n_blocks // per, per, bw, bw)
    eye = jnp.eye(per, dtype=w.dtype)
    t = w[:, :, :, :, None, :] * eye[None, None, :, None, :, None]
    return t.reshape(depth, n_blocks // per, MXU_DIM, MXU_DIM).astype(BF16)


def kernel(x, ffn1_norm, ffn1_wg, ffn1_wu, ffn1_wd, mix_norm, w_in, conv_w,
           conv_b, rg_wa, rg_ba, rg_wx, rg_bx, rg_lam, w_att_proj, w_rnn_proj,
           w_out, ffn2_norm, ffn2_wg, ffn2_wu, ffn2_wd, final_norm):
    batch, seq, d = x.shape
    depth = ffn1_norm.shape[0]
    assert batch == SUBLANES and seq % Q_TILE == 0
    assert (batch * seq) % MERGE_ROW_TILE == 0 and ROW_TILE % batch == 0

    def row(v):
        return v.reshape(1, -1)

    wg1, wu1, wd1 = ffn1_wg, ffn1_wu, ffn1_wd
    wg2, wu2, wd2 = ffn2_wg, ffn2_wu, ffn2_wd
    w_in_t = jnp.swapaxes(w_in, 1, 2)
    wa_t, wx_t = _block_diag_tiles(rg_wa), _block_diag_tiles(rg_wx)

    h = x
    for l in range(depth):
        h = _ffn(h, row(ffn1_norm[l]), wg1, wu1, wd1, l, batch, first=(l == 0))
        q, kv, qi, ki, wi, ga, gl, rnn = _mixer_in(
            h, row(mix_norm[l]), w_in_t, conv_w[l], row(conv_b[l]), wa_t,
            row(rg_ba[l]), wx_t, row(rg_bx[l]), row(rg_lam[l]), l, batch)
        att = _attention(q, qi, wi, kv, ki, batch)
        h = _merge(att, rnn, ga, gl, h, w_att_proj, w_rnn_proj, w_out, l,
                   batch)
        last = l == depth - 1
        h = _ffn(h, row(ffn2_norm[l]), wg2, wu2, wd2, l, batch,
                 final_g=row(final_norm) if last else None)
    return h
```

```python
import functools

import jax
import jax.numpy as jnp
from jax import lax
from jax.experimental import pallas as pl
from jax.experimental.pallas import tpu as pltpu

F32 = jnp.float32
BF16 = jnp.bfloat16
I32 = jnp.int32

N_HEADS = 8
HEAD_DIM = 64
D_ATT = N_HEADS * HEAD_DIM
IDX_HEADS = 8
IDX_DIM = 32
TOPK_MAX = 256
CONV_W = 4
LRU_C = 8.0
EPS = 1e-6

LANES = 128
SUBLANES = 8
BF16_SUBLANES = 16
MXU_DIM = 256
VMEM_LIMIT = 56 * 1024 * 1024

INT_MIN = -2 ** 31
MASK_BIAS = -1e30

ROW_TILE = 512
MERGE_ROW_TILE = 1024
Q_TILE = 256
K_CHUNK = 256


def _resident(shape, layer=None):
    zeros = (0,) * len(shape)
    if layer is None:
        return pl.BlockSpec(shape, lambda *_: zeros,
                            pipeline_mode=pl.Buffered(1))
    return pl.BlockSpec((None,) + tuple(shape), lambda *_: (layer,) + zeros,
                        pipeline_mode=pl.Buffered(1))


def _rmsnorm(x, g):
    ms = jnp.mean(x * x, axis=-1, keepdims=True)
    return x * lax.rsqrt(ms + EPS) * g


def _dot(a, b):
    return jnp.dot(a, b, preferred_element_type=F32)


def _dot_nt(a, b):
    return lax.dot_general(a, b, (((1,), (1,)), ((), ())),
                           preferred_element_type=F32)


def _ff_chunks(d_ff):
    chunks, c0 = [], 0
    while c0 < d_ff:
        c1 = min(c0 + MXU_DIM, d_ff)
        chunks.append((c0, c1))
        c0 = c1
    return chunks


def _slabs(width):
    return width // LANES


def _batch_to_rows(x_ref, slab_scr, batch):
    tt, d = x_ref.shape[1:]
    for b in range(batch):
        for j in range(_slabs(d)):
            slab_scr[j, pl.ds(b, tt, stride=batch), :] = (
                x_ref[b, :, j * LANES:(j + 1) * LANES])
    return jnp.concatenate([slab_scr[j] for j in range(_slabs(d))], axis=1)


def _rows_to_batch(o_ref, y, slab_scr, batch):
    tt, d = o_ref.shape[1:]
    for j in range(_slabs(d)):
        slab_scr[j] = y[:, j * LANES:(j + 1) * LANES]
    for b in range(batch):
        for j in range(_slabs(d)):
            o_ref[b, :, j * LANES:(j + 1) * LANES] = (
                slab_scr[j, pl.ds(b, tt, stride=batch), :])


def _rows_to_seq(o_ref, z, slab_scr, slab0, batch):
    tt = o_ref.shape[0]
    c = z.shape[1]
    for j in range(_slabs(c)):
        slab_scr[slab0 + j] = z[:, j * LANES:(j + 1) * LANES]
    for b in range(batch):
        for j in range(_slabs(c)):
            o_ref[:, b * c + j * LANES:b * c + (j + 1) * LANES] = (
                slab_scr[slab0 + j, pl.ds(b, tt, stride=batch), :]
                .astype(o_ref.dtype))


def _seq_to_rows(x_ref, slab_scr, batch):
    tt = x_ref.shape[0]
    c = x_ref.shape[1] // batch
    for b in range(batch):
        for j in range(_slabs(c)):
            slab_scr[j, pl.ds(b, tt, stride=batch), :] = (
                x_ref[:, b * c + j * LANES:b * c + (j + 1) * LANES].astype(F32))
    return jnp.concatenate([slab_scr[j] for j in range(_slabs(c))], axis=1)


def _ffn_kernel(x_ref, g_ref, wg_ref, wu_ref, wd_ref, *rest, first, final,
                batch):
    rest = list(rest)
    fg_ref = rest.pop(0) if final else None
    o_ref = rest.pop(0)
    slab_scr = rest.pop(0) if (first or final) else None
    x = _batch_to_rows(x_ref, slab_scr, batch) if first else x_ref[...]
    h = _rmsnorm(x, g_ref[...]).astype(BF16)
    acc = None
    for c0, c1 in _ff_chunks(wg_ref.shape[1]):
        g = _dot(h, wg_ref[:, c0:c1].astype(BF16))
        u = _dot(h, wu_ref[:, c0:c1].astype(BF16))
        a = (g * jax.nn.sigmoid(g) * u).astype(BF16)
        d = _dot(a, wd_ref[c0:c1, :].astype(BF16))
        acc = d if acc is None else acc + d
    y = x + 0.5 * acc
    if final:
        _rows_to_batch(o_ref, _rmsnorm(y, fg_ref[...]), slab_scr, batch)
    else:
        o_ref[...] = y


def _ffn(x, g, wg, wu, wd, layer, batch, first=False, final_g=None):
    final = final_g is not None
    if first:
        _, seq, d = x.shape
        rows = seq * batch
    else:
        rows, d = x.shape
        seq = rows // batch
    d_ff = wg.shape[2]
    tt = ROW_TILE // batch
    row_spec = pl.BlockSpec((ROW_TILE, d), lambda i: (i, 0))
    batch_spec = pl.BlockSpec((batch, tt, d), lambda i: (0, i, 0))
    in_specs = [batch_spec if first else row_spec, _resident((1, d)),
                _resident((d, d_ff), layer), _resident((d, d_ff), layer),
                _resident((d_ff, d), layer)]
    args = [x, g, wg, wu, wd]
    if final:
        in_specs.append(_resident((1, d)))
        args.append(final_g)
    scratch = ([pltpu.VMEM((_slabs(d), ROW_TILE, LANES), F32)]
               if (first or final) else [])
    out_shape = (batch, seq, d) if final else (rows, d)
    return pl.pallas_call(
        functools.partial(_ffn_kernel, first=first, final=final, batch=batch),
        out_shape=jax.ShapeDtypeStruct(out_shape, F32),
        grid=(rows // ROW_TILE,),
        in_specs=in_specs,
        out_specs=batch_spec if final else row_spec,
        scratch_shapes=scratch,
        compiler_params=pltpu.CompilerParams(
            dimension_semantics=("parallel",), vmem_limit_bytes=VMEM_LIMIT),
        name="ffn_first" if first else ("ffn_final" if final else "ffn"),
    )(*args)


ATT_WIDTHS = (D_ATT, 4 * HEAD_DIM, IDX_HEADS * IDX_DIM, LANES, LANES)
ATT_DTYPES = (BF16, BF16, BF16, BF16, F32)


def _zero_after(ref_tile):
    bits = pltpu.bitcast(ref_tile.astype(F32), jnp.uint32)
    sh = jnp.uint32(16)
    return pltpu.bitcast(lax.shift_right_logical(
        lax.shift_right_logical(bits, sh), sh), F32)


def _mixer_in_kernel(x_ref, g_ref, wt_ref, cw_ref, cb_ref, wa_ref, ba_ref,
                     wx_ref, bx_ref, lam_ref,
                     q_ref, kv_ref, qi_ref, ki_ref, wi_ref, ga_ref, gl_ref,
                     rnn_ref, slab_scr, xs_scr, a_scr, u_scr, h_scr, hc_scr,
                     *, batch):
    tm = x_ref.shape[0]
    c = cw_ref.shape[1]
    d_model = ga_ref.shape[1]
    halo = (CONV_W - 1) * batch
    h = _rmsnorm(x_ref[...], g_ref[...]).astype(BF16)

    @pl.when(pl.program_id(0) == 0)
    def _():
        xs_scr[0:halo, :] = jnp.zeros((halo, c), F32)
        hc_scr[...] = jnp.zeros(hc_scr.shape, F32)

    scales = (HEAD_DIM ** -0.5, None, None, None, (IDX_HEADS * IDX_DIM) ** -0.5)
    att_refs = (q_ref, kv_ref, qi_ref, ki_ref, wi_ref)
    r_q, r_k, r_v = 0, D_ATT, D_ATT + HEAD_DIM
    r_qi = r_v + HEAD_DIM
    r_ki = r_qi + IDX_HEADS * IDX_DIM
    r_wi = r_ki + IDX_DIM
    r_xr = r_wi + IDX_HEADS
    r_gr, r_ga = r_xr + c, r_xr + 2 * c
    r_gl = r_ga + d_model

    def wt(*ranges):
        parts = [jnp.zeros((-r0, d_model), F32) if r0 < 0 else wt_ref[r0:r1, :]
                 for r0, r1 in ranges]
        w = parts[0] if len(parts) == 1 else jnp.concatenate(parts, axis=0)
        return w.astype(BF16)

    k_rows, v_rows = (r_k, r_v), (r_v, r_qi)
    att_rows = (((r_q, r_k),), (k_rows, v_rows, v_rows, k_rows),
                ((r_qi, r_ki),), ((r_ki, r_wi),) * (LANES // IDX_DIM),
                ((r_wi, r_xr), (IDX_HEADS - LANES, 0)))

    def att_operand(i):
        z = _dot_nt(h, wt(*att_rows[i]))
        if scales[i] is not None:
            z = z * scales[i]
        _rows_to_seq(att_refs[i], z, slab_scr, _slabs(sum(ATT_WIDTHS[:i])),
                     batch)

    def gate_logits(o_ref):
        r0 = r_ga if o_ref is ga_ref else r_gl
        o_ref[...] = _dot_nt(h, wt((r0, r0 + d_model))).astype(o_ref.dtype)

    fillers = (lambda: gate_logits(ga_ref),
               lambda: gate_logits(gl_ref),
               lambda: (att_operand(0), att_operand(1)),
               lambda: (att_operand(2), att_operand(3), att_operand(4)))

    n_tiles = c // MXU_DIM
    assert n_tiles == len(fillers)
    for n in range(n_tiles):
        cs = slice(n * MXU_DIM, (n + 1) * MXU_DIM)
        xr = _dot_nt(h, wt((r_xr + cs.start, r_xr + cs.stop)))
        gr = _dot_nt(h, wt((r_gr + cs.start, r_gr + cs.stop)))
        xs_scr[halo:halo + tm, cs] = xr
        xc = cb_ref[:, cs]
        for j in range(CONV_W):
            xc = xc + cw_ref[j:j + 1, cs] * xs_scr[j * batch:j * batch + tm, cs]
        xs_scr[0:halo, cs] = xr[tm - halo:tm, :]

        xb = xc.astype(BF16)
        r = jax.nn.sigmoid(_dot(xb, wa_ref[n]) + ba_ref[:, cs])
        gate_i = jax.nn.sigmoid(_dot(xb, wx_ref[n]) + bx_ref[:, cs])
        fillers[n]()
        dep_ref = (ga_ref, gl_ref, kv_ref, wi_ref)[n]
        dep = _zero_after(dep_ref[0:SUBLANES, 0:LANES])
        neg_lam = -lam_ref[:, cs]
        softplus = (jnp.maximum(neg_lam, 0.0)
                    + jnp.log1p(jnp.exp(-jnp.abs(neg_lam))))
        log_a = r * (-LRU_C * softplus)
        a_scr[:, cs] = jnp.exp(log_a)
        th = jnp.tanh(log_a)
        u_scr[:, cs] = jnp.sqrt(-2.0 * th / (1.0 - th)) * gate_i * xc

        hcur = hc_scr[:, cs] + jnp.concatenate([dep] * (MXU_DIM // LANES),
                                               axis=1)
        for t in range(tm // batch):
            rs = slice(t * batch, (t + 1) * batch)
            hcur = a_scr[rs, cs] * hcur + u_scr[rs, cs]
            h_scr[rs, cs] = hcur
        hc_scr[:, cs] = hcur
        rnn_ref[:, cs] = (h_scr[:, cs] * jax.nn.gelu(gr)).astype(BF16)


def _mixer_in(x, g, wt, cw, cb, wa, ba, wx, bx, lam, layer, batch):
    rows, d = x.shape
    c = cw.shape[1]
    tm = ROW_TILE
    tt = tm // batch
    seq = rows // batch
    halo = (CONV_W - 1) * batch
    assert wt.shape[1] == (D_ATT + 2 * HEAD_DIM + IDX_HEADS * IDX_DIM + IDX_DIM
                           + IDX_HEADS + 2 * c + 2 * d)

    def row_spec(n):
        return pl.BlockSpec((tm, n), lambda i: (i, 0))

    def seq_spec(n):
        return pl.BlockSpec((tt, batch * n), lambda i: (i, 0))

    out_shape = [jax.ShapeDtypeStruct((seq, batch * n), t)
                 for n, t in zip(ATT_WIDTHS, ATT_DTYPES)]
    out_shape += [jax.ShapeDtypeStruct((rows, d), BF16)] * 2
    out_shape += [jax.ShapeDtypeStruct((rows, c), BF16)]
    out_specs = [seq_spec(n) for n in ATT_WIDTHS]
    out_specs += [row_spec(d), row_spec(d), row_spec(c)]
    return pl.pallas_call(
        functools.partial(_mixer_in_kernel, batch=batch),
        out_shape=out_shape,
        grid=(rows // tm,),
        in_specs=[row_spec(d), _resident((1, d)),
                  _resident(wt.shape[1:], layer), _resident(cw.shape),
                  _resident((1, c)), _resident(wa.shape[1:], layer),
                  _resident((1, c)), _resident(wx.shape[1:], layer),
                  _resident((1, c)), _resident((1, c))],
        out_specs=out_specs,
        scratch_shapes=[
            pltpu.VMEM((_slabs(sum(ATT_WIDTHS)), tm, LANES), F32),
            pltpu.VMEM((halo + tm, c), F32),
            pltpu.VMEM((tm, c), F32), pltpu.VMEM((tm, c), F32),
            pltpu.VMEM((tm, c), F32), pltpu.VMEM((batch, c), F32)],
        compiler_params=pltpu.CompilerParams(
            dimension_semantics=("arbitrary",), vmem_limit_bytes=VMEM_LIMIT),
        name="mixer_in",
    )(x, g, wt, cw, cb, wa, ba, wx, bx, lam)


VT_ROWS = HEAD_DIM + 16


def _key_to_float(key):
    bits = jnp.where(key < 0, INT_MIN - key, key)
    return pltpu.bitcast(bits, F32)


def _for_chunks(n, body):
    def quad(i, carry):
        for j in range(4):
            body(4 * i + j)
        return carry

    lax.fori_loop(0, n // 4, quad, 0)
    base = (n // 4) * 4

    @pl.when(n % 4 >= 2)
    def _():
        body(base)
        body(base + 1)

    @pl.when(n % 2 == 1)
    def _():
        body(n - 1)


def _fold_chunks(n, chunk, init):
    def pair(i, c):
        return chunk(2 * i + 1, chunk(2 * i, c))

    c = lax.fori_loop(0, n // 2, pair, init)
    return lax.cond(n % 2 == 1, lambda c: chunk(n - 1, c), lambda c: c, c)


def _attn_kernel(q_ref, qi_ref, wi_ref, kv_ref, ki_ref, o_ref,
                 qm_scr, qim_scr, wit_scr, vt_scr, sc_scr, scb_scr, sall_scr, m_scr,
                 acc_scr, mpart_scr, ties_scr, *, topk):
    tq, ck = Q_TILE, K_CHUNK
    n_kc = sc_scr.shape[0]
    seq = n_kc * ck
    qt = pl.program_id(1)
    n_chunks = (qt * tq) // ck + tq // ck
    lane = lax.broadcasted_iota(I32, (tq, LANES), 1)
    key_iota = lax.broadcasted_iota(I32, (ck, tq), 0)
    qry_iota = lax.broadcasted_iota(I32, (ck, tq), 1)
    kf = float(topk)

    for h in range(N_HEADS):
        qp = q_ref[:, (h // 2) * LANES:(h // 2 + 1) * LANES].astype(F32)
        lo = (h % 2) * HEAD_DIM
        qm_scr[h] = jnp.where((lane >= lo) & (lane < lo + HEAD_DIM), qp,
                              0.0).astype(BF16)
    for h in range(IDX_HEADS):
        qp = qi_ref[:, (h // 4) * LANES:(h // 4 + 1) * LANES].astype(F32)
        lo = (h % 4) * IDX_DIM
        qim_scr[h] = jnp.where((lane >= lo) & (lane < lo + IDX_DIM), qp,
                               0.0).astype(BF16)
    wit_scr[...] = jnp.transpose(wi_ref[...])

    @pl.when(qt == 0)
    def _():
        lane_c = lax.broadcasted_iota(I32, (ck, LANES), 1)
        for c in range(n_kc):
            v1 = jnp.where(lane_c < HEAD_DIM,
                           kv_ref[c * ck:(c + 1) * ck, LANES:2 * LANES].astype(F32),
                           1.0)
            vt_scr[c] = jnp.transpose(v1)[0:VT_ROWS, :].astype(BF16)

    def score_chunk(kc):
        k0 = pl.multiple_of(kc * ck, ck)
        kic = ki_ref[pl.ds(k0, ck), :]
        acc = jnp.zeros((ck, tq), F32)
        for h in range(IDX_HEADS):
            acc = acc + (jnp.maximum(_dot_nt(kic, qim_scr[h]), 0.0)
                         * wit_scr[h:h + 1, :])
        causal = key_iota - qry_iota <= qt * tq - kc * ck
        sc = jnp.where(causal, acc, -jnp.inf)
        sc_scr[kc] = sc
        scb_scr[kc] = sc.astype(BF16)

    _for_chunks(n_chunks, score_chunk)

    def count(pred):
        acc_rows = 2 * SUBLANES

        def chunk(kc, c):
            hit = jnp.where(pred(sc_scr[kc], kc), 1.0, 0.0)
            return c + jnp.sum(hit.reshape(ck // acc_rows, acc_rows, tq), axis=0)

        c = _fold_chunks(n_chunks, chunk, jnp.zeros((acc_rows, tq), F32))
        return jnp.sum(c, axis=0, keepdims=True)

    def count_bf16(cand):
        acc_rows = 2 * BF16_SUBLANES
        one, zero = jnp.ones((), BF16), jnp.zeros((), BF16)

        def chunk(kc, c):
            hit = jnp.where(scb_scr[kc] >= cand, one, zero)
            part = hit[0:acc_rows]
            for r0 in range(acc_rows, ck, acc_rows):
                part = part + hit[r0:r0 + acc_rows]
            return c + part.astype(F32)

        c = _fold_chunks(n_chunks, chunk, jnp.zeros((acc_rows, tq), F32))
        return jnp.sum(c, axis=0, keepdims=True)

    def coarse_step(i, thr16):
        cand16 = thr16 + lax.shift_left(jnp.int32(1), 15 - i)
        cand_f = _key_to_float(cand16 * 65536)
        tot = count_bf16(cand_f.astype(BF16))
        return jnp.where(tot >= kf, cand16, thr16)

    thr16 = lax.fori_loop(0, 16, coarse_step,
                          jnp.full((1, tq), -2 ** 15, I32))

    def fine_step(i, carry):
        thr, above = carry
        cand = thr + lax.shift_left(jnp.int32(1), 16 - i)
        cand_f = _key_to_float(cand)
        tot = count(lambda s, kc: s >= cand_f)
        ok = tot >= kf
        return jnp.where(ok, cand, thr), jnp.where(ok, above, tot)

    thr, above = lax.fori_loop(
        0, 17, fine_step,
        ((thr16 - 1) * 65536, jnp.full((1, tq), -1.0, F32)))

    thr_f = _key_to_float(thr)
    short = jnp.logical_not(thr_f > -jnp.inf)
    thr_f = jnp.where(short, -jnp.inf, thr_f)
    n_gt = lax.cond(jnp.min(above) < 0.0,
                    lambda: count(lambda s, kc: s > thr_f), lambda: above)
    need = jnp.where(short, 0.0, kf - n_gt)
    lower_tri = jnp.where(
        lax.broadcasted_iota(I32, (ck, ck), 0)
        >= lax.broadcasted_iota(I32, (ck, ck), 1), 1.0, 0.0).astype(BF16)

    mpart_scr[...] = jnp.full(mpart_scr.shape, MASK_BIAS, F32)
    acc_scr[...] = jnp.zeros(acc_scr.shape, F32)

    ties_scr[...] = jnp.zeros(ties_scr.shape, F32)

    def scores_chunk(kc):
        k0 = pl.multiple_of(kc * ck, ck)
        s_idx = sc_scr[kc]
        tie = s_idx == thr_f
        tie_f = jnp.where(tie, 1.0, 0.0)
        ties_before = ties_scr[...]
        ties_scr[...] = ties_before + jnp.sum(tie_f, axis=0, keepdims=True)
        rank = ties_before + _dot(lower_tri, tie_f.astype(BF16))
        keep = (s_idx > thr_f) | (tie & (rank <= need))
        bias = jnp.where(keep, 0.0, MASK_BIAS)
        for h in range(N_HEADS):
            c0 = 0 if h % 2 == 0 else LANES
            k_mat = kv_ref[pl.ds(k0, ck), c0:c0 + LANES]
            s = _dot_nt(k_mat, qm_scr[h]) + bias
            sall_scr[kc, h] = s
            mpart_scr[h] = jnp.maximum(
                mpart_scr[h],
                jnp.max(s.reshape(ck // SUBLANES, SUBLANES, tq), axis=0))

    _for_chunks(n_chunks, scores_chunk)
    for h in range(N_HEADS):
        m_scr[h:h + 1, :] = jnp.max(mpart_scr[h], axis=0, keepdims=True)

    def value_chunk(kc):
        vt_c = vt_scr[kc]
        for h in range(N_HEADS):
            p = jnp.exp(sall_scr[kc, h] - m_scr[h:h + 1, :]).astype(BF16)
            acc_scr[h] = acc_scr[h] + _dot(vt_c, p)

    _for_chunks(n_chunks, value_chunk)

    outs = []
    for h in range(N_HEADS):
        a = acc_scr[h]
        outs.append(a[0:HEAD_DIM, :] / a[HEAD_DIM:HEAD_DIM + 1, :])
    o_ref[...] = jnp.transpose(jnp.concatenate(outs, axis=0)).astype(BF16)


def _attention(q, qi, wi, kv, ki, batch):
    tq, ck = Q_TILE, K_CHUNK
    seq = q.shape[0]
    topk = min(TOPK_MAX, seq // 4)
    nq = seq // tq

    def q_spec(c):
        return pl.BlockSpec((tq, c), lambda b, i: (i, b))

    def kv_spec(c):
        return pl.BlockSpec((seq, c), lambda b, i: (0, b))

    return pl.pallas_call(
        functools.partial(_attn_kernel, topk=topk),
        out_shape=jax.ShapeDtypeStruct((seq, batch * D_ATT), BF16),
        grid=(batch, nq),
        in_specs=[q_spec(D_ATT), q_spec(IDX_HEADS * IDX_DIM), q_spec(LANES),
                  kv_spec(4 * HEAD_DIM), kv_spec(LANES)],
        out_specs=q_spec(D_ATT),
        scratch_shapes=[
            pltpu.VMEM((N_HEADS, tq, LANES), BF16),
            pltpu.VMEM((IDX_HEADS, tq, LANES), BF16),
            pltpu.VMEM((LANES, tq), F32),
            pltpu.VMEM((seq // ck, VT_ROWS, ck), BF16),
            pltpu.VMEM((seq // ck, ck, tq), F32),
            pltpu.VMEM((seq // ck, ck, tq), BF16),
            pltpu.VMEM((seq // ck, N_HEADS, ck, tq), F32),
            pltpu.VMEM((N_HEADS, tq), F32),
            pltpu.VMEM((N_HEADS, VT_ROWS, tq), F32),
            pltpu.VMEM((N_HEADS, SUBLANES, tq), F32),
            pltpu.VMEM((1, tq), F32),
        ],
        compiler_params=pltpu.CompilerParams(
            dimension_semantics=("parallel", "arbitrary"),
            vmem_limit_bytes=VMEM_LIMIT),
        name="dsa_attention",
    )(q, qi, wi, kv, ki)


def _merge_kernel(att_ref, rnn_ref, ga_ref, gl_ref, x_ref, wa_ref, wr_ref,
                  wo_ref, o_ref, slab_scr, *, batch):
    att = _seq_to_rows(att_ref, slab_scr, batch).astype(BF16)
    pa = _dot(att, wa_ref[...].astype(BF16))
    pr = _dot(rnn_ref[...], wr_ref[...].astype(BF16))
    merged = (jax.nn.sigmoid(ga_ref[...].astype(F32)) * pa
              + jax.nn.sigmoid(gl_ref[...].astype(F32)) * pr)
    o_ref[...] = x_ref[...] + _dot(merged.astype(BF16),
                                   wo_ref[...].astype(BF16))


def _merge(att, rnn, ga, gl, x, wa, wr, wo, layer, batch):
    rows, d = x.shape
    tt = MERGE_ROW_TILE // batch

    def row_spec(c):
        return pl.BlockSpec((MERGE_ROW_TILE, c), lambda i: (i, 0))

    return pl.pallas_call(
        functools.partial(_merge_kernel, batch=batch),
        out_shape=jax.ShapeDtypeStruct((rows, d), F32),
        grid=(rows // MERGE_ROW_TILE,),
        in_specs=[pl.BlockSpec((tt, att.shape[1]), lambda i: (i, 0)),
                  row_spec(rnn.shape[1]), row_spec(d), row_spec(d),
                  row_spec(d), _resident(wa.shape[1:], layer),
                  _resident(wr.shape[1:], layer),
                  _resident(wo.shape[1:], layer)],
        out_specs=row_spec(d),
        scratch_shapes=[pltpu.VMEM((_slabs(D_ATT), MERGE_ROW_TILE, LANES), F32)],
        compiler_params=pltpu.CompilerParams(
            dimension_semantics=("parallel",), vmem_limit_bytes=VMEM_LIMIT),
        name="merge",
    )(att, rnn, ga, gl, x, wa, wr, wo)


def _block_diag_tiles(w):
    depth, n_blocks, bw, _ = w.shape
    per = MXU_DIM // bw
    w = w.reshape(depth, n_blocks // per, per, bw, bw)
    eye = jnp.eye(per, dtype=w.dtype)
    t = w[:, :, :, :, None, :] * eye[None, None, :, None, :, None]
    return t.reshape(depth, n_blocks // per, MXU_DIM, MXU_DIM).astype(BF16)


def kernel(x, ffn1_norm, ffn1_wg, ffn1_wu, ffn1_wd, mix_norm, w_in, conv_w,
           conv_b, rg_wa, rg_ba, rg_wx, rg_bx, rg_lam, w_att_proj, w_rnn_proj,
           w_out, ffn2_norm, ffn2_wg, ffn2_wu, ffn2_wd, final_norm):
    batch, seq, d = x.shape
    depth = ffn1_norm.shape[0]
    assert batch == SUBLANES and seq % Q_TILE == 0
    assert (batch * seq) % MERGE_ROW_TILE == 0 and ROW_TILE % batch == 0

    def row(v):
        return v.reshape(1, -1)

    wg1, wu1, wd1 = ffn1_wg, ffn1_wu, ffn1_wd
    wg2, wu2, wd2 = ffn2_wg, ffn2_wu, ffn2_wd
    w_in_t = jnp.swapaxes(w_in, 1, 2)
    wa_t, wx_t = _block_diag_tiles(rg_wa), _block_diag_tiles(rg_wx)

    h = x
    for l in range(depth):
        h = _ffn(h, row(ffn1_norm[l]), wg1, wu1, wd1, l, batch, first=(l == 0))
        q, kv, qi, ki, wi, ga, gl, rnn = _mixer_in(
            h, row(mix_norm[l]), w_in_t, conv_w[l], row(conv_b[l]), wa_t,
            row(rg_ba[l]), wx_t, row(rg_bx[l]), row(rg_lam[l]), l, batch)
        att = _attention(q, qi, wi, kv, ki, batch)
        h = _merge(att, rnn, ga, gl, h, w_att_proj, w_rnn_proj, w_out, l,
                   batch)
        last = l == depth - 1
        h = _ffn(h, row(ffn2_norm[l]), wg2, wu2, wd2, l, batch,
                 final_g=row(final_norm) if last else None)
    return h
```
